```python
import math
import jax, jax.numpy as jnp
from jax import lax
import numpy as np

D_MODEL = 1024
BATCH = 32
SEQ = 2048
DEPTH = 1

N_META = 16
NORM_EPS = 1e-6
HG_HEADS = 8
HG_DK = D_MODEL // HG_HEADS
HG_DV = D_MODEL // HG_HEADS
HG_WIDTH = HG_HEADS * HG_DK
HG_CHUNK = 16
MLA_HEADS = 8
QK_NOPE = 128
QK_ROPE = 64
V_HEAD = 128
Q_LORA = 256
KV_LORA = 256
ROPE_THETA = 10000.0
ATTN_BLOCK = 128
FFN_HIDDEN = ((8 * D_MODEL + 3 * 256 - 1) // (3 * 256)) * 256
IN_SIZES = (HG_WIDTH, HG_WIDTH, HG_WIDTH, HG_WIDTH,
            Q_LORA, KV_LORA, QK_ROPE,
            D_MODEL, D_MODEL)
IN_COLS = sum(IN_SIZES)

kernel_name = "hybrid_hgrn2_mla_gated_block"


def rms_norm(x, g):
    xf = x.astype(jnp.float32)
    y = xf * lax.rsqrt(jnp.mean(xf * xf, axis=-1, keepdims=True) + NORM_EPS)
    return (y * g.astype(jnp.float32)).astype(x.dtype)


def rope_tables(length):
    pos = jnp.arange(length, dtype=jnp.float32)
    inv_freq = 1.0 / (ROPE_THETA ** (jnp.arange(0, QK_ROPE, 2, dtype=jnp.float32) / QK_ROPE))
    ang = pos[:, None] * inv_freq[None, :]
    return jnp.cos(ang), jnp.sin(ang)


def apply_rope(t, cos, sin):
    t32 = t.astype(jnp.float32)
    t1, t2 = jnp.split(t32, 2, axis=-1)
    return jnp.concatenate([t1 * cos - t2 * sin, t2 * cos + t1 * sin], axis=-1).astype(t.dtype)


def hgrn2_chunk_scan(q, k, v, logf):
    B, L, H, _ = q.shape
    n = L // HG_CHUNK

    def to_chunks(t):
        return t.reshape(B, n, HG_CHUNK, H, t.shape[-1]).transpose(1, 0, 3, 2, 4)

    xs = (to_chunks(q), to_chunks(k), to_chunks(v), to_chunks(logf))
    causal = jnp.tril(jnp.ones((HG_CHUNK, HG_CHUNK), dtype=bool))[:, :, None]

    def step(S, inp):
        qb, kb, vb, gb = inp
        b = jnp.cumsum(gb, axis=2)
        o_inter = jnp.einsum('bhtk,bhkv->bhtv', qb * jnp.exp(b), S)
        diff = b[:, :, :, None, :] - b[:, :, None, :, :]
        decay = jnp.exp(jnp.where(causal, diff, -jnp.inf))
        A = jnp.einsum('bhtsk,bhsk->bhts', decay * qb[:, :, :, None, :], kb)
        o_intra = jnp.einsum('bhts,bhsv->bhtv', A, vb)
        b_last = b[:, :, -1:, :]
        S_new = jnp.exp(b_last[:, :, 0, :])[..., None] * S + jnp.einsum(
            'bhsk,bhsv->bhkv', kb * jnp.exp(b_last - b), vb)
        return S_new, o_inter + o_intra

    S0 = jnp.zeros((B, H, q.shape[-1], v.shape[-1]), jnp.float32)
    _, ys = lax.scan(step, S0, xs)
    return ys.transpose(1, 0, 3, 2, 4).reshape(B, L, H, v.shape[-1])


def hgrn2_mixer(q, f_pre, i, g, lb, norm_g):
    B, L, _ = q.shape
    split = lambda t: t.reshape(B, L, HG_HEADS, -1).astype(jnp.float32)
    qh = jax.nn.silu(split(q))
    lbh = lb.astype(jnp.float32).reshape(HG_HEADS, HG_DK)
    fgate = lbh + (1.0 - lbh) * jax.nn.sigmoid(split(f_pre))
    o = hgrn2_chunk_scan(qh, 1.0 - fgate, split(i), jnp.log(fgate))
    o = rms_norm(o, norm_g) * jax.nn.silu(split(g))
    return o.reshape(B, L, HG_WIDTH).astype(q.dtype)


def mla_mixer(c_q, c_kv, k_pe, q_norm_g, w_q_b, kv_norm_g, w_kv_b, cos, sin):
    B, L, _ = c_q.shape
    q = (rms_norm(c_q, q_norm_g) @ w_q_b).reshape(B, L, MLA_HEADS, QK_NOPE + QK_ROPE)
    q_nope = q[..., :QK_NOPE]
    q_pe = apply_rope(q[..., QK_NOPE:], cos[:, None, :], sin[:, None, :])
    kv = (rms_norm(c_kv, kv_norm_g) @ w_kv_b).reshape(B, L, MLA_HEADS, QK_NOPE + V_HEAD)
    k_nope, v = kv[..., :QK_NOPE], kv[..., QK_NOPE:]
    k_pe = apply_rope(k_pe, cos, sin)
    scale = (QK_NOPE + QK_ROPE) ** -0.5
    bounds = [(0, N_META)] + [(N_META + s, min(N_META + s + ATTN_BLOCK, L))
                              for s in range(0, L - N_META, ATTN_BLOCK)]
    outs = []
    for start, end in bounds:
        s = (jnp.einsum('bqhd,bkhd->bhqk', q_nope[:, start:end], k_nope[:, :end])
             + jnp.einsum('bqhr,bkr->bhqk', q_pe[:, start:end], k_pe[:, :end]))
        s = s.astype(jnp.float32) * scale
        qpos = jnp.arange(start, end)[:, None]
        kpos = jnp.arange(end)[None, :]
        s = jnp.where(kpos <= qpos, s, -jnp.inf)
        p = jax.nn.softmax(s, axis=-1).astype(v.dtype)
        outs.append(jnp.einsum('bhqk,bkhd->bqhd', p, v[:, :end]))
    o = jnp.concatenate(outs, axis=1)
    return o.reshape(B, L, MLA_HEADS * V_HEAD)


def setup_inputs(seed: int = 0) -> dict:
    key = jax.random.key(seed)
    ks = jax.random.split(key, 20)
    nrm = lambda k, shape, scale: jax.random.normal(k, shape, jnp.float32) * scale
    gain = lambda k, shape: 1.0 + 0.02 * jax.random.normal(k, shape, jnp.float32)
    return {
        "x": nrm(ks[0], (BATCH, SEQ, D_MODEL), 1.0),
        "meta_tokens": nrm(ks[1], (N_META, D_MODEL), 1.0),
        "w_in": nrm(ks[2], (DEPTH, D_MODEL, IN_COLS), D_MODEL ** -0.5),
        "b_gate": nrm(ks[3], (DEPTH, 2 * D_MODEL), 0.01),
        "lb_logits": nrm(ks[4], (DEPTH + 1, HG_WIDTH), 0.1),
        "hg_norm_g": gain(ks[5], (DEPTH, HG_DV)),
        "w_hg_o": nrm(ks[6], (DEPTH, HG_WIDTH, D_MODEL), HG_WIDTH ** -0.5),
        "q_a_norm_g": gain(ks[7], (DEPTH, Q_LORA)),
        "w_q_b": nrm(ks[8], (DEPTH, Q_LORA, MLA_HEADS * (QK_NOPE + QK_ROPE)), Q_LORA ** -0.5),
        "kv_a_norm_g": gain(ks[9], (DEPTH, KV_LORA)),
        "w_kv_b": nrm(ks[10], (DEPTH, KV_LORA, MLA_HEADS * (QK_NOPE + V_HEAD)), KV_LORA ** -0.5),
        "w_mla_o": nrm(ks[11], (DEPTH, MLA_HEADS * V_HEAD, D_MODEL), (MLA_HEADS * V_HEAD) ** -0.5),
        "w_out": nrm(ks[12], (DEPTH, D_MODEL, D_MODEL), D_MODEL ** -0.5),
        "mix_pre_g": gain(ks[13], (DEPTH, D_MODEL)),
        "mix_post_g": gain(ks[14], (DEPTH, D_MODEL)),
        "ffn_pre_g": gain(ks[15], (DEPTH, D_MODEL)),
        "ffn_post_g": gain(ks[16], (DEPTH, D_MODEL)),
        "w_ffn_in": nrm(ks[17], (DEPTH, D_MODEL, 2 * FFN_HIDDEN), D_MODEL ** -0.5),
        "w_ffn_out": nrm(ks[18], (DEPTH, FFN_HIDDEN, D_MODEL), FFN_HIDDEN ** -0.5),
    }


def reference(x, meta_tokens, w_in, b_gate, lb_logits, hg_norm_g, w_hg_o, q_a_norm_g, w_q_b,
              kv_a_norm_g, w_kv_b, w_mla_o, w_out, mix_pre_g, mix_post_g, ffn_pre_g, ffn_post_g,
              w_ffn_in, w_ffn_out):
    B = x.shape[0]
    meta = jnp.broadcast_to(meta_tokens[None].astype(x.dtype), (B, N_META, D_MODEL))
    h = jnp.concatenate([meta, x], axis=1)
    L = h.shape[1]
    cos, sin = rope_tables(L)
    lower_bounds = jnp.cumsum(jax.nn.softmax(lb_logits.astype(jnp.float32), axis=0), axis=0)
    splits = []
    acc = 0
    for sz in IN_SIZES[:-2]:
        acc += sz
        splits.append(acc)
    for l in range(DEPTH):
        u = rms_norm(h, mix_pre_g[l])
        proj = u @ w_in[l]
        hq, hf, hi, hg, cq, ckv, kpe, gates = jnp.split(proj, splits, axis=-1)
        y_a = hgrn2_mixer(hq, hf, hi, hg, lower_bounds[l], hg_norm_g[l]) @ w_hg_o[l]
        y_b = mla_mixer(cq, ckv, kpe, q_a_norm_g[l], w_q_b[l], kv_a_norm_g[l], w_kv_b[l],
                        cos, sin) @ w_mla_o[l]
        gate_a, gate_b = jnp.split(jax.nn.sigmoid(gates + b_gate[l]), 2, axis=-1)
        mixed = (gate_a * y_a + gate_b * y_b) @ w_out[l]
        h = h + rms_norm(mixed, mix_post_g[l])
        u = rms_norm(h, ffn_pre_g[l])
        gt, up = jnp.split(u @ w_ffn_in[l], 2, axis=-1)
        h = h + rms_norm((jax.nn.silu(gt) * up) @ w_ffn_out[l], ffn_post_g[l])
    return h[:, N_META:, :]
```

```python
import functools
import math

import jax
import jax.numpy as jnp
from jax import lax
from jax.experimental import pallas as pl
from jax.experimental.pallas import tpu as pltpu

N_META = 16
NORM_EPS = 1e-6
HG_HEADS = 8
HG_D = 128
MLA_HEADS = 8
QK_NOPE = 128
QK_ROPE = 64
V_HEAD = 128
Q_LORA = 256
KV_LORA = 256
ROPE_THETA = 10000.0

V7X_LANES = 128
V7X_VMEM_LIMIT_BYTES = 56 * 1024 * 1024

INPROJ_ROWS = 256
HG_BLOCK = 64
ATTN_Q = 256
ATTN_K = 256
OUT_ROWS = 256

F32 = jnp.float32
BF16 = jnp.bfloat16


def _dot(a, b):
    return jnp.dot(a, b, preferred_element_type=F32)


def _dot_nt(a, b):
    return lax.dot_general(a, b, (((1,), (1,)), ((), ())), preferred_element_type=F32)


def _dot_tn(a, b):
    return lax.dot_general(a, b, (((0,), (0,)), ((), ())), preferred_element_type=F32)


def _rms(x, g):
    ms = jnp.mean(x * x, axis=-1, keepdims=True)
    return x * lax.rsqrt(ms + NORM_EPS) * g


def _sigmoid(x):
    return 1.0 / (1.0 + jnp.exp(-x))


def _resident(shape):
    return pl.BlockSpec(shape, lambda *_: (0,) * len(shape), pipeline_mode=pl.Buffered(1))


def _inproj_kernel(x_ref, gpre_ref, whq_ref, whf_ref, whi_ref, whg_ref, wc_ref, wpe_ref,
                   wgate_ref, bgate_ref, lbl_ref, qg_ref, kvg_ref, wqn_ref, wqp_ref,
                   wqpr_ref, wkn_ref, wv_ref, cos_ref, sin_ref,
                   qh_ref, kk_ref, logf_ref, vh_ref, og_ref, gates_ref,
                   qn_ref, qp_ref, kn_ref, kp_ref, va_ref, *, scale):
    u = _rms(x_ref[...], gpre_ref[...]).astype(BF16)

    lbl = lbl_ref[...]
    e = jnp.exp(lbl - jnp.max(lbl, axis=0, keepdims=True))
    lb = e[0:1] / jnp.sum(e, axis=0, keepdims=True)

    hq = _dot(u, whq_ref[...])
    qh_ref[...] = (hq * _sigmoid(hq)).astype(BF16)
    sg = _sigmoid(_dot(u, whf_ref[...]))
    logf_ref[...] = jnp.log(lb + (1.0 - lb) * sg)
    kk_ref[...] = ((1.0 - lb) * (1.0 - sg)).astype(BF16)
    vh_ref[...] = _dot(u, whi_ref[...]).astype(BF16)
    hg = _dot(u, whg_ref[...])
    og_ref[...] = (hg * _sigmoid(hg)).astype(BF16)

    gates_ref[...] = _sigmoid(_dot(u, wgate_ref[...]) + bgate_ref[...]).astype(BF16)

    cos = cos_ref[...]
    sin = sin_ref[...]
    pe = _dot(u, wpe_ref[...])
    kp_ref[...] = (pe[:, :V7X_LANES] * cos + pe[:, V7X_LANES:] * sin).astype(BF16)

    c = _dot(u, wc_ref[...])
    cq = _rms(c[:, :Q_LORA], qg_ref[...]).astype(BF16)
    ckv = _rms(c[:, Q_LORA:], kvg_ref[...]).astype(BF16)
    qn_ref[...] = (_dot(cq, wqn_ref[...]) * scale).astype(BF16)
    qpe = _dot(cq, wqp_ref[...])
    qper = _dot(cq, wqpr_ref[...])
    for h in range(MLA_HEADS):
        sl = slice(h * V7X_LANES, (h + 1) * V7X_LANES)
        qp_ref[:, sl] = ((qpe[:, sl] * cos + qper[:, sl] * sin) * scale).astype(BF16)
    kn_ref[...] = _dot(ckv, wkn_ref[...]).astype(BF16)
    va_ref[...] = _dot(ckv, wv_ref[...]).astype(BF16)


def _inproj(x2d, rows, pos_blocks, params, cos_t, sin_t):
    t, d = x2d.shape
    n = t // rows
    row = lambda w: pl.BlockSpec((rows, w), lambda i: (i, 0))
    pos = pl.BlockSpec((rows, V7X_LANES), lambda i: (i % pos_blocks, 0))
    w_specs = [_resident(p.shape) for p in params]
    out_w = [(d, BF16), (d, BF16), (d, F32), (d, BF16), (d, BF16), (2 * d, BF16),
             (d, BF16), (d, BF16), (d, BF16), (V7X_LANES, BF16), (d, BF16)]
    scale = (QK_NOPE + QK_ROPE) ** -0.5
    return pl.pallas_call(
        functools.partial(_inproj_kernel, scale=scale),
        grid=(n,),
        in_specs=[row(d)] + w_specs + [pos, pos],
        out_specs=[row(w) for w, _ in out_w],
        out_shape=[jax.ShapeDtypeStruct((t, w), dt) for w, dt in out_w],
        compiler_params=pltpu.CompilerParams(
            dimension_semantics=("parallel",), vmem_limit_bytes=V7X_VMEM_LIMIT_BYTES),
        name="inproj",
    )(x2d, *params, cos_t, sin_t)


def _cumsum_rows(x):
    n = x.shape[0]
    row = lax.broadcasted_iota(jnp.int32, x.shape, 0)
    s = 1
    while s < n:
        x = x + jnp.where(row >= s, pltpu.roll(x, s, 0), 0.0)
        s *= 2
    return x


def _hgrn_kernel(q_ref, k_ref, logf_ref, v_ref, og_ref, km_ref, logfm_ref, vm_ref, g_ref,
                 o_ref, st_ref):
    n = pl.program_id(1)
    c_blk = q_ref.shape[0]
    mid = c_blk // 2 - 1

    @pl.when(n == 0)
    def _init():
        cm = _cumsum_rows(logfm_ref[...])
        kl = (km_ref[...].astype(F32) * jnp.exp(cm[N_META - 1:N_META] - cm)).astype(BF16)
        vm = vm_ref[...]
        for h in range(HG_HEADS):
            sl = slice(h * HG_D, (h + 1) * HG_D)
            st_ref[h] = _dot_tn(vm[:, sl], kl[:, sl])

    c = _cumsum_rows(logf_ref[...])
    cm = c[mid:mid + 1]
    cl = c[c_blk - 1:c_blk]
    e_q = jnp.exp(c - cm)
    e_k = jnp.exp(cm - c)
    qt = q_ref[...].astype(F32) * e_q
    kt = k_ref[...].astype(F32) * e_k
    qi = (qt * jnp.exp(cm)).astype(BF16)
    kl = (kt * jnp.exp(cl - cm)).astype(BF16)
    qt = qt.astype(BF16)
    kt = kt.astype(BF16)
    dec = jnp.exp(cl)
    v = v_ref[...]
    og = og_ref[...]
    g = g_ref[...]
    tri = (lax.broadcasted_iota(jnp.int32, (c_blk, c_blk), 0)
           >= lax.broadcasted_iota(jnp.int32, (c_blk, c_blk), 1))
    for h in range(HG_HEADS):
        sl = slice(h * HG_D, (h + 1) * HG_D)
        a = jnp.where(tri, _dot_nt(qt[:, sl], kt[:, sl]), 0.0).astype(BF16)
        st = st_ref[h]
        o = _dot(a, v[:, sl]) + _dot_nt(qi[:, sl], st.astype(BF16))
        st_ref[h] = st * dec[:, sl] + _dot_tn(v[:, sl], kl[:, sl])
        o_ref[:, sl] = (_rms(o, g) * og[:, sl].astype(F32)).astype(BF16)


def _hgrn(qh, kk, logf, vh, og, km, logfm, vm, g, batch, seq):
    w = qh.shape[-1]
    nblk = seq // HG_BLOCK
    tok = pl.BlockSpec((None, HG_BLOCK, w), lambda b, n: (b, n, 0))
    meta = pl.BlockSpec((N_META, w), lambda b, n: (0, 0))
    r3 = lambda a: a.reshape(batch, seq, w)
    return pl.pallas_call(
        _hgrn_kernel,
        grid=(batch, nblk),
        in_specs=[tok, tok, tok, tok, tok, meta, meta, meta,
                  pl.BlockSpec((1, HG_D), lambda b, n: (0, 0))],
        out_specs=tok,
        out_shape=jax.ShapeDtypeStruct((batch, seq, w), BF16),
        scratch_shapes=[pltpu.VMEM((HG_HEADS, HG_D, HG_D), F32)],
        compiler_params=pltpu.CompilerParams(
            dimension_semantics=("parallel", "arbitrary"),
            vmem_limit_bytes=V7X_VMEM_LIMIT_BYTES),
        name="hgrn2",
    )(r3(qh), r3(kk), r3(logf), r3(vh), r3(og), km, logfm, vm, g)


def _attn_kernel(qn_ref, qp_ref, kn_ref, kp_ref, v_ref, knm_ref, kpm_ref, vm_ref, o_ref):
    i = pl.program_id(2)
    tq = qn_ref.shape[0]
    q = jnp.concatenate([qn_ref[...], qp_ref[...]], axis=1)

    km = jnp.concatenate([knm_ref[...], kpm_ref[...]], axis=1)
    s = _dot_nt(q, km)
    m = jnp.max(s, axis=-1, keepdims=True)
    p = jnp.exp(s - m)
    l = jnp.sum(p, axis=-1, keepdims=True)
    acc = _dot(p.astype(BF16), vm_ref[...])

    def step(j, carry, masked):
        m, l, acc = carry
        ks = pl.multiple_of(j * ATTN_K, ATTN_K)
        kc = jnp.concatenate([kn_ref[pl.ds(ks, ATTN_K), :], kp_ref[pl.ds(ks, ATTN_K), :]],
                             axis=1)
        s = _dot_nt(q, kc)
        if masked:
            keep = (lax.broadcasted_iota(jnp.int32, s.shape, 0)
                    >= lax.broadcasted_iota(jnp.int32, s.shape, 1))
            s = jnp.where(keep, s, -jnp.inf)
        m_new = jnp.maximum(m, jnp.max(s, axis=-1, keepdims=True))
        alpha = jnp.exp(m - m_new)
        p = jnp.exp(s - m_new)
        l = alpha * l + jnp.sum(p, axis=-1, keepdims=True)
        acc = alpha * acc + _dot(p.astype(BF16), v_ref[pl.ds(ks, ATTN_K), :])
        return m_new, l, acc

    carry = lax.fori_loop(0, i, functools.partial(step, masked=False), (m, l, acc))
    m, l, acc = step(i, carry, masked=True)
    o_ref[...] = (acc / l).astype(BF16)


def _attention(qn, qp, kn, kp, va, knm, kpm, vam, batch, seq):
    assert ATTN_Q == ATTN_K
    w = qn.shape[-1]
    nq = seq // ATTN_Q
    r3 = lambda a: a.reshape(batch, seq, a.shape[-1])
    qblk = pl.BlockSpec((None, ATTN_Q, V7X_LANES), lambda b, h, i: (b, i, h))
    kfull = pl.BlockSpec((None, seq, V7X_LANES), lambda b, h, i: (b, 0, h))
    kpfull = pl.BlockSpec((None, seq, V7X_LANES), lambda b, h, i: (b, 0, 0))
    mh = pl.BlockSpec((N_META, V7X_LANES), lambda b, h, i: (0, h))
    mp = pl.BlockSpec((N_META, V7X_LANES), lambda b, h, i: (0, 0))
    return pl.pallas_call(
        _attn_kernel,
        grid=(batch, MLA_HEADS, nq),
        in_specs=[qblk, qblk, kfull, kpfull, kfull, mh, mp, mh],
        out_specs=qblk,
        out_shape=jax.ShapeDtypeStruct((batch, seq, w), BF16),
        compiler_params=pltpu.CompilerParams(
            dimension_semantics=("parallel", "parallel", "arbitrary"),
            vmem_limit_bytes=V7X_VMEM_LIMIT_BYTES),
        name="mla_attn",
    )(r3(qn), r3(qp), r3(kn), r3(kp), r3(va), knm, kpm, vam)


def _out_kernel(x_ref, oh_ref, oa_ref, gates_ref, who_ref, wmo_ref, wout_ref, gmix_ref,
                gfpre_ref, wfin_ref, wfout_ref, gfpost_ref, y_ref):
    d = x_ref.shape[-1]
    hidden = wfout_ref.shape[0]
    ya = _dot(oh_ref[...], who_ref[...])
    yb = _dot(oa_ref[...], wmo_ref[...])
    gates = gates_ref[...].astype(F32)
    merged = (gates[:, :d] * ya + gates[:, d:] * yb).astype(BF16)
    h1 = x_ref[...] + _rms(_dot(merged, wout_ref[...]), gmix_ref[...])
    u = _rms(h1, gfpre_ref[...]).astype(BF16)
    gu = _dot(u, wfin_ref[...])
    gt = gu[:, :hidden]
    act = (gt * _sigmoid(gt) * gu[:, hidden:]).astype(BF16)
    y_ref[...] = h1 + _rms(_dot(act, wfout_ref[...]), gfpost_ref[...])


def _out_block(x2d, oh, oa, gates, params):
    t, d = x2d.shape
    n = t // OUT_ROWS
    row = lambda w: pl.BlockSpec((OUT_ROWS, w), lambda i: (i, 0))
    return pl.pallas_call(
        _out_kernel,
        grid=(n,),
        in_specs=[row(d), row(d), row(d), row(2 * d)] + [_resident(p.shape) for p in params],
        out_specs=row(d),
        out_shape=jax.ShapeDtypeStruct((t, d), F32),
        compiler_params=pltpu.CompilerParams(
            dimension_semantics=("parallel",), vmem_limit_bytes=V7X_VMEM_LIMIT_BYTES),
        name="merge_ffn",
    )(x2d, oh, oa, gates, *params)


def _rope_tables(length):
    pos = jnp.arange(length, dtype=F32)
    inv_freq = 1.0 / (ROPE_THETA ** (jnp.arange(0, QK_ROPE, 2, dtype=F32) / QK_ROPE))
    ang = pos[:, None] * inv_freq[None, :]
    cos, sin = jnp.cos(ang), jnp.sin(ang)
    zero = jnp.zeros((length, V7X_LANES - QK_ROPE), F32)
    return (jnp.concatenate([cos, cos, zero], axis=1),
            jnp.concatenate([-sin, sin, zero], axis=1))


def _swap_halves(w):
    half = w.shape[-1] // 2
    return jnp.concatenate([w[..., half:], w[..., :half]], axis=-1)


def kernel(x, meta_tokens, w_in, b_gate, lb_logits, hg_norm_g, w_hg_o, q_a_norm_g, w_q_b,
           kv_a_norm_g, w_kv_b, w_mla_o, w_out, mix_pre_g, mix_post_g, ffn_pre_g, ffn_post_g,
           w_ffn_in, w_ffn_out):
    batch, seq, d = x.shape
    assert w_in.shape[0] == 1, "single-layer block"
    assert seq % INPROJ_ROWS == 0 and seq % ATTN_Q == 0 and seq % HG_BLOCK == 0
    assert (batch * seq) % OUT_ROWS == 0
    hgw = HG_HEADS * HG_D
    row = lambda a: a.reshape(1, -1).astype(F32)

    wi = w_in[0]
    o = 0
    parts = []
    for sz in (hgw, hgw, hgw, hgw, Q_LORA + KV_LORA, QK_ROPE, 2 * d):
        parts.append(wi[:, o:o + sz])
        o += sz
    whq, whf, whi, whg, wc, wkpe, wgate = parts
    zpad = jnp.zeros((d, V7X_LANES - QK_ROPE), wi.dtype)
    wpe = jnp.concatenate([wkpe, zpad, _swap_halves(wkpe), zpad], axis=1)
    wq = w_q_b[0].reshape(Q_LORA, MLA_HEADS, QK_NOPE + QK_ROPE)
    wqn = wq[:, :, :QK_NOPE].reshape(Q_LORA, MLA_HEADS * QK_NOPE)
    wq_pe = wq[:, :, QK_NOPE:]
    zq = jnp.zeros((Q_LORA, MLA_HEADS, V7X_LANES - QK_ROPE), wq.dtype)
    wqp = jnp.concatenate([wq_pe, zq], axis=2).reshape(Q_LORA, MLA_HEADS * V7X_LANES)
    wqpr = jnp.concatenate([_swap_halves(wq_pe), zq], axis=2).reshape(
        Q_LORA, MLA_HEADS * V7X_LANES)
    wkv = w_kv_b[0].reshape(KV_LORA, MLA_HEADS, QK_NOPE + V_HEAD)
    wkn = wkv[:, :, :QK_NOPE].reshape(KV_LORA, MLA_HEADS * QK_NOPE)
    wv = wkv[:, :, QK_NOPE:].reshape(KV_LORA, MLA_HEADS * V_HEAD)
    bf = lambda a: a.astype(BF16)
    inproj_params = [row(mix_pre_g[0]), bf(whq), bf(whf), bf(whi), bf(whg), bf(wc), bf(wpe),
                     bf(wgate), row(b_gate[0]), lb_logits.astype(F32), row(q_a_norm_g[0]),
                     row(kv_a_norm_g[0]), bf(wqn), bf(wqp), bf(wqpr), bf(wkn), bf(wv)]
    cos_t, sin_t = _rope_tables(N_META + seq)

    m_out = _inproj(meta_tokens.astype(F32), N_META, 1, inproj_params,
                    cos_t[:N_META], sin_t[:N_META])
    _, kk_m, logf_m, vh_m, _, _, _, _, kn_m, kp_m, va_m = m_out

    x2d = x.reshape(batch * seq, d)
    (qh, kk, logf, vh, og, gates, qn, qp, kn, kp, va) = _inproj(
        x2d, INPROJ_ROWS, seq // INPROJ_ROWS, inproj_params, cos_t[N_META:], sin_t[N_META:])

    o_hg = _hgrn(qh, kk, logf, vh, og, kk_m, logf_m, vh_m, row(hg_norm_g[0]), batch, seq)
    o_at = _attention(qn, qp, kn, kp, va, kn_m, kp_m, va_m, batch, seq)

    out_params = [bf(w_hg_o[0]), bf(w_mla_o[0]), bf(w_out[0]), row(mix_post_g[0]),
                  row(ffn_pre_g[0]), bf(w_ffn_in[0]), bf(w_ffn_out[0]), row(ffn_post_g[0])]
    y = _out_block(x2d, o_hg.reshape(batch * seq, hgw), o_at.reshape(batch * seq, -1),
                   gates, out_params)
    return y.reshape(batch, seq, d)
```

```python
import functools
import math

import jax
import jax.numpy as jnp
from jax import lax
from jax.experimental import pallas as pl
from jax.experimental.pallas import tpu as pltpu

N_META = 16
NORM_EPS = 1e-6
HG_HEADS = 8
HG_D = 128
MLA_HEADS = 8
QK_NOPE = 128
QK_ROPE = 64
V_HEAD = 128
Q_LORA = 256
KV_LORA = 256
ROPE_THETA = 10000.0

V7X_LANES = 128
V7X_VMEM_LIMIT_BYTES = 56 * 1024 * 1024

INPROJ_ROWS = 256
HG_BLOCK = 64
ATTN_Q = 256
ATTN_K = 256
OUT_ROWS = 256

F32 = jnp.float32
BF16 = jnp.bfloat16


def _dot(a, b):
    return jnp.dot(a, b, preferred_element_type=F32)


def _dot_nt(a, b):
    return lax.dot_general(a, b, (((1,), (1,)), ((), ())), preferred_element_type=F32)


def _dot_tn(a, b):
    return lax.dot_general(a, b, (((0,), (0,)), ((), ())), preferred_element_type=F32)


def _rms(x, g):
    ms = jnp.mean(x * x, axis=-1, keepdims=True)
    return x * lax.rsqrt(ms + NORM_EPS) * g


def _sigmoid(x):
    return 1.0 / (1.0 + jnp.exp(-x))


def _resident(shape):
    return pl.BlockSpec(shape, lambda *_: (0,) * len(shape), pipeline_mode=pl.Buffered(1))


def _inproj_kernel(x_ref, gpre_ref, whq_ref, whf_ref, whi_ref, whg_ref, wc_ref, wpe_ref,
                   wgate_ref, bgate_ref, lbl_ref, qg_ref, kvg_ref, wqn_ref, wqp_ref,
                   wqpr_ref, wkn_ref, wv_ref, cos_ref, sin_ref,
                   qh_ref, kk_ref, logf_ref, vh_ref, og_ref, gates_ref,
                   qn_ref, qp_ref, kn_ref, kp_ref, va_ref, *, scale):
    u = _rms(x_ref[...], gpre_ref[...]).astype(BF16)

    lbl = lbl_ref[...]
    e = jnp.exp(lbl - jnp.max(lbl, axis=0, keepdims=True))
    lb = e[0:1] / jnp.sum(e, axis=0, keepdims=True)

    hq = _dot(u, whq_ref[...])
    qh_ref[...] = (hq * _sigmoid(hq)).astype(BF16)
    sg = _sigmoid(_dot(u, whf_ref[...]))
    logf_ref[...] = jnp.log(lb + (1.0 - lb) * sg)
    kk_ref[...] = ((1.0 - lb) * (1.0 - sg)).astype(BF16)
    vh_ref[...] = _dot(u, whi_ref[...]).astype(BF16)
    hg = _dot(u, whg_ref[...])
    og_ref[...] = (hg * _sigmoid(hg)).astype(BF16)

    gates_ref[...] = _sigmoid(_dot(u, wgate_ref[...]) + bgate_ref[...]).astype(BF16)

    cos = cos_ref[...]
    sin = sin_ref[...]
    pe = _dot(u, wpe_ref[...])
    kp_ref[...] = (pe[:, :V7X_LANES] * cos + pe[:, V7X_LANES:] * sin).astype(BF16)

    c = _dot(u, wc_ref[...])
    cq = _rms(c[:, :Q_LORA], qg_ref[...]).astype(BF16)
    ckv = _rms(c[:, Q_LORA:], kvg_ref[...]).astype(BF16)
    qn_ref[...] = (_dot(cq, wqn_ref[...]) * scale).astype(BF16)
    qpe = _dot(cq, wqp_ref[...])
    qper = _dot(cq, wqpr_ref[...])
    for h in range(MLA_HEADS):
        sl = slice(h * V7X_LANES, (h + 1) * V7X_LANES)
        qp_ref[:, sl] = ((qpe[:, sl] * cos + qper[:, sl] * sin) * scale).astype(BF16)
    kn_ref[...] = _dot(ckv, wkn_ref[...]).astype(BF16)
    va_ref[...] = _dot(ckv, wv_ref[...]).astype(BF16)


def _inproj(x2d, rows, pos_blocks, params, cos_t, sin_t):
    t, d = x2d.shape
    n = t // rows
    row = lambda w: pl.BlockSpec((rows, w), lambda i: (i, 0))
    pos = pl.BlockSpec((rows, V7X_LANES), lambda i: (i % pos_blocks, 0))
    w_specs = [_resident(p.shape) for p in params]
    out_w = [(d, BF16), (d, BF16), (d, F32), (d, BF16), (d, BF16), (2 * d, BF16),
             (d, BF16), (d, BF16), (d, BF16), (V7X_LANES, BF16), (d, BF16)]
    scale = (QK_NOPE + QK_ROPE) ** -0.5
    return pl.pallas_call(
        functools.partial(_inproj_kernel, scale=scale),
        grid=(n,),
        in_specs=[row(d)] + w_specs + [pos, pos],
        out_specs=[row(w) for w, _ in out_w],
        out_shape=[jax.ShapeDtypeStruct((t, w), dt) for w, dt in out_w],
        compiler_params=pltpu.CompilerParams(
            dimension_semantics=("parallel",), vmem_limit_bytes=V7X_VMEM_LIMIT_BYTES),
        name="inproj",
    )(x2d, *params, cos_t, sin_t)


def _cumsum_rows(x):
    n = x.shape[0]
    row = lax.broadcasted_iota(jnp.int32, x.shape, 0)
    s = 1
    while s < n:
        x = x + jnp.where(row >= s, pltpu.roll(x, s, 0), 0.0)
        s *= 2
    return x


def _hgrn_kernel(q_ref, k_ref, logf_ref, v_ref, og_ref, km_ref, logfm_ref, vm_ref, g_ref,
                 o_ref, st_ref):
    n = pl.program_id(1)
    c_blk = q_ref.shape[0]
    mid = c_blk // 2 - 1

    @pl.when(n == 0)
    def _init():
        cm = _cumsum_rows(logfm_ref[...])
        kl = (km_ref[...].astype(F32) * jnp.exp(cm[N_META - 1:N_META] - cm)).astype(BF16)
        vm = vm_ref[...]
        for h in range(HG_HEADS):
            sl = slice(h * HG_D, (h + 1) * HG_D)
            st_ref[h] = _dot_tn(vm[:, sl], kl[:, sl])

    c = _cumsum_rows(logf_ref[...])
    cm = c[mid:mid + 1]
    cl = c[c_blk - 1:c_blk]
    e_q = jnp.exp(c - cm)
    e_k = jnp.exp(cm - c)
    qt = q_ref[...].astype(F32) * e_q
    kt = k_ref[...].astype(F32) * e_k
    qi = (qt * jnp.exp(cm)).astype(BF16)
    kl = (kt * jnp.exp(cl - cm)).astype(BF16)
    qt = qt.astype(BF16)
    kt = kt.astype(BF16)
    dec = jnp.exp(cl)
    v = v_ref[...]
    og = og_ref[...]
    g = g_ref[...]
    tri = (lax.broadcasted_iota(jnp.int32, (c_blk, c_blk), 0)
           >= lax.broadcasted_iota(jnp.int32, (c_blk, c_blk), 1))
    for h in range(HG_HEADS):
        sl = slice(h * HG_D, (h + 1) * HG_D)
        a = jnp.where(tri, _dot_nt(qt[:, sl], kt[:, sl]), 0.0).astype(BF16)
        st = st_ref[h]
        o = _dot(a, v[:, sl]) + _dot_nt(qi[:, sl], st.astype(BF16))
        st_ref[h] = st * dec[:, sl] + _dot_tn(v[:, sl], kl[:, sl])
        o_ref[:, sl] = (_rms(o, g) * og[:, sl].astype(F32)).astype(BF16)


def _hgrn(qh, kk, logf, vh, og, km, logfm, vm, g, batch, seq):
    w = qh.shape[-1]
    nblk = seq // HG_BLOCK
    tok = pl.BlockSpec((None, HG_BLOCK, w), lambda b, n: (b, n, 0))
    meta = pl.BlockSpec((N_META, w), lambda b, n: (0, 0))
    r3 = lambda a: a.reshape(batch, seq, w)
    return pl.pallas_call(
        _hgrn_kernel,
        grid=(batch, nblk),
        in_specs=[tok, tok, tok, tok, tok, meta, meta, meta,
                  pl.BlockSpec((1, HG_D), lambda b, n: (0, 0))],
        out_specs=tok,
        out_shape=jax.ShapeDtypeStruct((batch, seq, w), BF16),
        scratch_shapes=[pltpu.VMEM((HG_HEADS, HG_D, HG_D), F32)],
        compiler_params=pltpu.CompilerParams(
            dimension_semantics=("parallel", "arbitrary"),
            vmem_limit_bytes=V7X_VMEM_LIMIT_BYTES),
        name="hgrn2",
    )(r3(qh), r3(kk), r3(logf), r3(vh), r3(og), km, logfm, vm, g)


def _attn_kernel(qn_ref, qp_ref, kn_ref, kp_ref, v_ref, knm_ref, kpm_ref, vm_ref, o_ref):
    i = pl.program_id(1)
    tq = qn_ref.shape[0]
    hs = [slice(h * V7X_LANES, (h + 1) * V7X_LANES) for h in range(MLA_HEADS)]
    qs = [jnp.concatenate([qn_ref[:, sl], qp_ref[:, sl]], axis=1) for sl in hs]

    def update(carry, s_list, v_list):
        m, l, acc = carry
        m_new = m
        for s in s_list:
            m_new = jnp.maximum(m_new, jnp.max(s, axis=-1, keepdims=True))
        alpha = jnp.exp(m - m_new)
        l = alpha * l
        acc = alpha * acc
        for s, v in zip(s_list, v_list):
            p = jnp.exp(s - m_new)
            l = l + jnp.sum(p, axis=-1, keepdims=True)
            acc = acc + _dot(p.astype(BF16), v)
        return m_new, l, acc

    def step(j, carries):
        ks = pl.multiple_of(j * ATTN_K, ATTN_K)
        kp = kp_ref[pl.ds(ks, ATTN_K), :]
        out = []
        for h, sl in enumerate(hs):
            kc = jnp.concatenate([kn_ref[pl.ds(ks, ATTN_K), sl], kp], axis=1)
            out.append(update(carries[h], [_dot_nt(qs[h], kc)], [v_ref[pl.ds(ks, ATTN_K), sl]]))
        return tuple(out)

    init = tuple((jnp.full((tq, 1), -jnp.inf, F32), jnp.zeros((tq, 1), F32),
                  jnp.zeros((tq, V_HEAD), F32)) for _ in hs)
    carries = lax.fori_loop(0, i, step, init)

    ks = pl.multiple_of(i * ATTN_K, ATTN_K)
    kp = kp_ref[pl.ds(ks, ATTN_K), :]
    kpm = kpm_ref[...]
    keep = (lax.broadcasted_iota(jnp.int32, (tq, ATTN_K), 0)
            >= lax.broadcasted_iota(jnp.int32, (tq, ATTN_K), 1))
    for h, sl in enumerate(hs):
        kc = jnp.concatenate([kn_ref[pl.ds(ks, ATTN_K), sl], kp], axis=1)
        km = jnp.concatenate([knm_ref[:, sl], kpm], axis=1)
        s_d = jnp.where(keep, _dot_nt(qs[h], kc), -jnp.inf)
        s_m = _dot_nt(qs[h], km)
        _, l, acc = update(carries[h], [s_d, s_m], [v_ref[pl.ds(ks, ATTN_K), sl], vm_ref[:, sl]])
        o_ref[:, sl] = (acc / l).astype(BF16)


def _attention(qn, qp, kn, kp, va, knm, kpm, vam, batch, seq):
    assert ATTN_Q == ATTN_K
    w = qn.shape[-1]
    nq = seq // ATTN_Q
    r3 = lambda a: a.reshape(batch, seq, a.shape[-1])
    qblk = pl.BlockSpec((None, ATTN_Q, w), lambda b, i: (b, i, 0))
    kfull = pl.BlockSpec((None, seq, w), lambda b, i: (b, 0, 0))
    kpfull = pl.BlockSpec((None, seq, V7X_LANES), lambda b, i: (b, 0, 0))
    mh = pl.BlockSpec((N_META, w), lambda b, i: (0, 0))
    mp = pl.BlockSpec((N_META, V7X_LANES), lambda b, i: (0, 0))
    return pl.pallas_call(
        _attn_kernel,
        grid=(batch, nq),
        in_specs=[qblk, qblk, kfull, kpfull, kfull, mh, mp, mh],
        out_specs=qblk,
        out_shape=jax.ShapeDtypeStruct((batch, seq, w), BF16),
        compiler_params=pltpu.CompilerParams(
            dimension_semantics=("parallel", "arbitrary"),
            vmem_limit_bytes=V7X_VMEM_LIMIT_BYTES),
        name="mla_attn",
    )(r3(qn), r3(qp), r3(kn), r3(kp), r3(va), knm, kpm, vam)


def _out_kernel(x_ref, oh_ref, oa_ref, gates_ref, who_ref, wmo_ref, wout_ref, gmix_ref,
                gfpre_ref, wfin_ref, wfout_ref, gfpost_ref, y_ref):
    d = x_ref.shape[-1]
    hidden = wfout_ref.shape[0]
    ya = _dot(oh_ref[...], who_ref[...])
    yb = _dot(oa_ref[...], wmo_ref[...])
    gates = gates_ref[...].astype(F32)
    merged = (gates[:, :d] * ya + gates[:, d:] * yb).astype(BF16)
    h1 = x_ref[...] + _rms(_dot(merged, wout_ref[...]), gmix_ref[...])
    u = _rms(h1, gfpre_ref[...]).astype(BF16)
    gu = _dot(u, wfin_ref[...])
    gt = gu[:, :hidden]
    act = (gt * _sigmoid(gt) * gu[:, hidden:]).astype(BF16)
    y_ref[...] = h1 + _rms(_dot(act, wfout_ref[...]), gfpost_ref[...])


def _out_block(x2d, oh, oa, gates, params):
    t, d = x2d.shape
    n = t // OUT_ROWS
    row = lambda w: pl.BlockSpec((OUT_ROWS, w), lambda i: (i, 0))
    return pl.pallas_call(
        _out_kernel,
        grid=(n,),
        in_specs=[row(d), row(d), row(d), row(2 * d)] + [_resident(p.shape) for p in params],
        out_specs=row(d),
        out_shape=jax.ShapeDtypeStruct((t, d), F32),
        compiler_params=pltpu.CompilerParams(
            dimension_semantics=("parallel",), vmem_limit_bytes=V7X_VMEM_LIMIT_BYTES),
        name="merge_ffn",
    )(x2d, oh, oa, gates, *params)


def _rope_tables(length):
    pos = jnp.arange(length, dtype=F32)
    inv_freq = 1.0 / (ROPE_THETA ** (jnp.arange(0, QK_ROPE, 2, dtype=F32) / QK_ROPE))
    ang = pos[:, None] * inv_freq[None, :]
    cos, sin = jnp.cos(ang), jnp.sin(ang)
    zero = jnp.zeros((length, V7X_LANES - QK_ROPE), F32)
    return (jnp.concatenate([cos, cos, zero], axis=1),
            jnp.concatenate([-sin, sin, zero], axis=1))


def _swap_halves(w):
    half = w.shape[-1] // 2
    return jnp.concatenate([w[..., half:], w[..., :half]], axis=-1)


def kernel(x, meta_tokens, w_in, b_gate, lb_logits, hg_norm_g, w_hg_o, q_a_norm_g, w_q_b,
           kv_a_norm_g, w_kv_b, w_mla_o, w_out, mix_pre_g, mix_post_g, ffn_pre_g, ffn_post_g,
           w_ffn_in, w_ffn_out):
    batch, seq, d = x.shape
    assert w_in.shape[0] == 1, "single-layer block"
    assert seq % INPROJ_ROWS == 0 and seq % ATTN_Q == 0 and seq % HG_BLOCK == 0
    assert (batch * seq) % OUT_ROWS == 0
    hgw = HG_HEADS * HG_D
    row = lambda a: a.reshape(1, -1).astype(F32)

    wi = w_in[0]
    o = 0
    parts = []
    for sz in (hgw, hgw, hgw, hgw, Q_LORA + KV_LORA, QK_ROPE, 2 * d):
        parts.append(wi[:, o:o + sz])
        o += sz
    whq, whf, whi, whg, wc, wkpe, wgate = parts
    zpad = jnp.zeros((d, V7X_LANES - QK_ROPE), wi.dtype)
    wpe = jnp.concatenate([wkpe, zpad, _swap_halves(wkpe), zpad], axis=1)
    wq = w_q_b[0].reshape(Q_LORA, MLA_HEADS, QK_NOPE + QK_ROPE)
    wqn = wq[:, :, :QK_NOPE].reshape(Q_LORA, MLA_HEADS * QK_NOPE)
    wq_pe = wq[:, :, QK_NOPE:]
    zq = jnp.zeros((Q_LORA, MLA_HEADS, V7X_LANES - QK_ROPE), wq.dtype)
    wqp = jnp.concatenate([wq_pe, zq], axis=2).reshape(Q_LORA, MLA_HEADS * V7X_LANES)
    wqpr = jnp.concatenate([_swap_halves(wq_pe), zq], axis=2).reshape(
        Q_LORA, MLA_HEADS * V7X_LANES)
    wkv = w_kv_b[0].reshape(KV_LORA, MLA_HEADS, QK_NOPE + V_HEAD)
    wkn = wkv[:, :, :QK_NOPE].reshape(KV_LORA, MLA_HEADS * QK_NOPE)
    wv = wkv[:, :, QK_NOPE:].reshape(KV_LORA, MLA_HEADS * V_HEAD)
    bf = lambda a: a.astype(BF16)
    inproj_params = [row(mix_pre_g[0]), bf(whq), bf(whf), bf(whi), bf(whg), bf(wc), bf(wpe),
                     bf(wgate), row(b_gate[0]), lb_logits.astype(F32), row(q_a_norm_g[0]),
                     row(kv_a_norm_g[0]), bf(wqn), bf(wqp), bf(wqpr), bf(wkn), bf(wv)]
    cos_t, sin_t = _rope_tables(N_META + seq)

    m_out = _inproj(meta_tokens.astype(F32), N_META, 1, inproj_params,
                    cos_t[:N_META], sin_t[:N_META])
    _, kk_m, logf_m, vh_m, _, _, _, _, kn_m, kp_m, va_m = m_out

    x2d = x.reshape(batch * seq, d)
    (qh, kk, logf, vh, og, gates, qn, qp, kn, kp, va) = _inproj(
        x2d, INPROJ_ROWS, seq // INPROJ_ROWS, inproj_params, cos_t[N_META:], sin_t[N_META:])

    o_hg = _hgrn(qh, kk, logf, vh, og, kk_m, logf_m, vh_m, row(hg_norm_g[0]), batch, seq)
    o_at = _attention(qn, qp, kn, kp, va, kn_m, kp_m, va_m, batch, seq)

    out_params = [bf(w_hg_o[0]), bf(w_mla_o[0]), bf(w_out[0]), row(mix_post_g[0]),
                  row(ffn_pre_g[0]), bf(w_ffn_in[0]), bf(w_ffn_out[0]), row(ffn_post_g[0])]
    y = _out_block(x2d, o_hg.reshape(batch * seq, hgw), o_at.reshape(batch * seq, -1),
                   gates, out_params)
    return y.reshape(batch, seq, d)
```

```python
import functools
import math

import jax
import jax.numpy as jnp
from jax import lax
from jax.experimental import pallas as pl
from jax.experimental.pallas import tpu as pltpu

N_META = 16
NORM_EPS = 1e-6
HG_HEADS = 8
HG_D = 128
MLA_HEADS = 8
QK_NOPE = 128
QK_ROPE = 64
V_HEAD = 128
Q_LORA = 256
KV_LORA = 256
ROPE_THETA = 10000.0

V7X_LANES = 128
V7X_VMEM_LIMIT_BYTES = 56 * 1024 * 1024

INPROJ_ROWS = 256
HG_BLOCK = 64
ATTN_Q = 256
ATTN_K = 256
ATTN_SKEW = 4
OUT_ROWS = 256

F32 = jnp.float32
BF16 = jnp.bfloat16


def _dot(a, b):
    return jnp.dot(a, b, preferred_element_type=F32)


def _dot_nt(a, b):
    return lax.dot_general(a, b, (((1,), (1,)), ((), ())), preferred_element_type=F32)


def _dot_tn(a, b):
    return lax.dot_general(a, b, (((0,), (0,)), ((), ())), preferred_element_type=F32)


def _rms(x, g):
    ms = jnp.mean(x * x, axis=-1, keepdims=True)
    return x * lax.rsqrt(ms + NORM_EPS) * g


def _sigmoid(x):
    return 1.0 / (1.0 + jnp.exp(-x))


def _resident(shape):
    return pl.BlockSpec(shape, lambda *_: (0,) * len(shape), pipeline_mode=pl.Buffered(1))


def _inproj_kernel(x_ref, gpre_ref, whq_ref, whf_ref, whi_ref, whg_ref, wc_ref, wpe_ref,
                   wgate_ref, bgate_ref, lbl_ref, qg_ref, kvg_ref, wqn_ref, wqp_ref,
                   wqpr_ref, wkn_ref, wv_ref, cos_ref, sin_ref,
                   qh_ref, kk_ref, logf_ref, vh_ref, og_ref, gates_ref,
                   qn_ref, qp_ref, kn_ref, kp_ref, va_ref, *, scale):
    u = _rms(x_ref[...], gpre_ref[...]).astype(BF16)

    lbl = lbl_ref[...]
    e = jnp.exp(lbl - jnp.max(lbl, axis=0, keepdims=True))
    lb = e[0:1] / jnp.sum(e, axis=0, keepdims=True)

    hq = _dot(u, whq_ref[...])
    qh_ref[...] = (hq * _sigmoid(hq)).astype(BF16)
    sg = _sigmoid(_dot(u, whf_ref[...]))
    logf_ref[...] = jnp.log(lb + (1.0 - lb) * sg)
    kk_ref[...] = ((1.0 - lb) * (1.0 - sg)).astype(BF16)
    vh_ref[...] = _dot(u, whi_ref[...]).astype(BF16)
    hg = _dot(u, whg_ref[...])
    og_ref[...] = (hg * _sigmoid(hg)).astype(BF16)

    gates_ref[...] = _sigmoid(_dot(u, wgate_ref[...]) + bgate_ref[...]).astype(BF16)

    cos = cos_ref[...]
    sin = sin_ref[...]
    pe = _dot(u, wpe_ref[...])
    kp_ref[...] = (pe[:, :V7X_LANES] * cos + pe[:, V7X_LANES:] * sin).astype(BF16)

    c = _dot(u, wc_ref[...])
    cq = _rms(c[:, :Q_LORA], qg_ref[...]).astype(BF16)
    ckv = _rms(c[:, Q_LORA:], kvg_ref[...]).astype(BF16)
    qn_ref[...] = (_dot(cq, wqn_ref[...]) * scale).astype(BF16)
    qpe = _dot(cq, wqp_ref[...])
    qper = _dot(cq, wqpr_ref[...])
    for h in range(MLA_HEADS):
        sl = slice(h * V7X_LANES, (h + 1) * V7X_LANES)
        qp_ref[:, sl] = ((qpe[:, sl] * cos + qper[:, sl] * sin) * scale).astype(BF16)
    kn_ref[...] = _dot(ckv, wkn_ref[...]).astype(BF16)
    va_ref[...] = _dot_nt(wv_ref[...], ckv).astype(BF16)


def _inproj(x2d, rows, pos_blocks, params, cos_t, sin_t):
    t, d = x2d.shape
    n = t // rows
    row = lambda w: pl.BlockSpec((rows, w), lambda i: (i, 0))
    pos = pl.BlockSpec((rows, V7X_LANES), lambda i: (i % pos_blocks, 0))
    w_specs = [_resident(p.shape) for p in params]
    out_w = [(d, BF16), (d, BF16), (d, F32), (d, BF16), (d, BF16), (2 * d, BF16),
             (d, BF16), (d, BF16), (d, BF16), (V7X_LANES, BF16)]
    vt_w = MLA_HEADS * V_HEAD
    scale = (QK_NOPE + QK_ROPE) ** -0.5 * math.log2(math.e)
    return pl.pallas_call(
        functools.partial(_inproj_kernel, scale=scale),
        grid=(n,),
        in_specs=[row(d)] + w_specs + [pos, pos],
        out_specs=[row(w) for w, _ in out_w]
        + [pl.BlockSpec((None, vt_w, rows), lambda i: (i, 0, 0))],
        out_shape=[jax.ShapeDtypeStruct((t, w), dt) for w, dt in out_w]
        + [jax.ShapeDtypeStruct((n, vt_w, rows), BF16)],
        compiler_params=pltpu.CompilerParams(
            dimension_semantics=("parallel",), vmem_limit_bytes=V7X_VMEM_LIMIT_BYTES),
        name="inproj",
    )(x2d, *params, cos_t, sin_t)


def _cumsum_rows(x):
    n = x.shape[0]
    row = lax.broadcasted_iota(jnp.int32, x.shape, 0)
    s = 1
    while s < n:
        x = x + jnp.where(row >= s, pltpu.roll(x, s, 0), 0.0)
        s *= 2
    return x


def _hgrn_kernel(q_ref, k_ref, logf_ref, v_ref, og_ref, km_ref, logfm_ref, vm_ref, g_ref,
                 o_ref, st_ref):
    n = pl.program_id(1)
    c_blk = q_ref.shape[0]
    mid = c_blk // 2 - 1

    @pl.when(n == 0)
    def _init():
        cm = _cumsum_rows(logfm_ref[...])
        kl = (km_ref[...].astype(F32) * jnp.exp(cm[N_META - 1:N_META] - cm)).astype(BF16)
        vm = vm_ref[...]
        for h in range(HG_HEADS):
            sl = slice(h * HG_D, (h + 1) * HG_D)
            st_ref[h] = _dot_tn(vm[:, sl], kl[:, sl])

    c = _cumsum_rows(logf_ref[...])
    cm = c[mid:mid + 1]
    cl = c[c_blk - 1:c_blk]
    e_q = jnp.exp(c - cm)
    e_k = jnp.exp(cm - c)
    qt = q_ref[...].astype(F32) * e_q
    kt = k_ref[...].astype(F32) * e_k
    qi = (qt * jnp.exp(cm)).astype(BF16)
    kl = (kt * jnp.exp(cl - cm)).astype(BF16)
    qt = qt.astype(BF16)
    kt = kt.astype(BF16)
    dec = jnp.exp(cl)
    v = v_ref[...]
    og = og_ref[...]
    g = g_ref[...]
    tri = (lax.broadcasted_iota(jnp.int32, (c_blk, c_blk), 0)
           >= lax.broadcasted_iota(jnp.int32, (c_blk, c_blk), 1))
    for h in range(HG_HEADS):
        sl = slice(h * HG_D, (h + 1) * HG_D)
        a = jnp.where(tri, _dot_nt(qt[:, sl], kt[:, sl]), 0.0).astype(BF16)
        st = st_ref[h]
        o = _dot(a, v[:, sl]) + _dot_nt(qi[:, sl], st.astype(BF16))
        st_ref[h] = st * dec[:, sl] + _dot_tn(v[:, sl], kl[:, sl])
        o_ref[:, sl] = (_rms(o, g) * og[:, sl].astype(F32)).astype(BF16)


def _hgrn(qh, kk, logf, vh, og, km, logfm, vm, g, batch, seq):
    w = qh.shape[-1]
    nblk = seq // HG_BLOCK
    tok = pl.BlockSpec((None, HG_BLOCK, w), lambda b, n: (b, n, 0))
    meta = pl.BlockSpec((N_META, w), lambda b, n: (0, 0))
    r3 = lambda a: a.reshape(batch, seq, w)
    return pl.pallas_call(
        _hgrn_kernel,
        grid=(batch, nblk),
        in_specs=[tok, tok, tok, tok, tok, meta, meta, meta,
                  pl.BlockSpec((1, HG_D), lambda b, n: (0, 0))],
        out_specs=tok,
        out_shape=jax.ShapeDtypeStruct((batch, seq, w), BF16),
        scratch_shapes=[pltpu.VMEM((HG_HEADS, HG_D, HG_D), F32)],
        compiler_params=pltpu.CompilerParams(
            dimension_semantics=("parallel", "arbitrary"),
            vmem_limit_bytes=V7X_VMEM_LIMIT_BYTES),
        name="hgrn2",
    )(r3(qh), r3(kk), r3(logf), r3(vh), r3(og), km, logfm, vm, g)


def _attn_kernel(qn_ref, qp_ref, kn_ref, kp_ref, vt_ref, knm_ref, kpm_ref, vtm_ref, o_ref,
                 m_ref, l_ref, acc_ref):
    i = pl.program_id(1)
    tq = qn_ref.shape[0]
    hs = [slice(h * V7X_LANES, (h + 1) * V7X_LANES) for h in range(MLA_HEADS)]
    qs = [jnp.concatenate([qn_ref[:, sl], qp_ref[:, sl]], axis=1) for sl in hs]

    def update(carry, s_list, vt_list):
        m, l, acc = carry
        m_new = m
        for s in s_list:
            m_new = jnp.maximum(m_new, jnp.max(s, axis=0, keepdims=True))
        alpha = jnp.exp2(m - m_new)
        l = alpha * l
        acc = alpha * acc
        for s, vt in zip(s_list, vt_list):
            p = jnp.exp2(s - m_new)
            l = l + jnp.sum(p, axis=0, keepdims=True)
            acc = acc + _dot(vt, p.astype(BF16))
        return m_new, l, acc

    def skewed(score_fn, consume_fn):
        scores = []
        for h in range(MLA_HEADS + ATTN_SKEW):
            if h < MLA_HEADS:
                scores.append(score_fn(h))
            g = h - ATTN_SKEW
            if g >= 0:
                consume_fn(g, scores[g])

    for h in range(MLA_HEADS):
        m_ref[h] = jnp.full((1, tq), -jnp.inf, F32)
        l_ref[h] = jnp.zeros((1, tq), F32)
        acc_ref[h] = jnp.zeros((V_HEAD, tq), F32)

    def kv_step(blocks):
        starts = [pl.multiple_of(b * ATTN_K, ATTN_K) for b in blocks]
        kps = [kp_ref[pl.ds(ks, ATTN_K), :] for ks in starts]

        def score(h):
            return [_dot_nt(jnp.concatenate([kn_ref[pl.ds(ks, ATTN_K), hs[h]], kp], axis=1),
                            qs[h]) for ks, kp in zip(starts, kps)]

        def consume(g, s_list):
            m, l, acc = update((m_ref[g], l_ref[g], acc_ref[g]), s_list,
                               [vt_ref[b, hs[g], :] for b in blocks])
            m_ref[g] = m
            l_ref[g] = l
            acc_ref[g] = acc

        skewed(score, consume)

    def pair_step(j2, _):
        kv_step([2 * j2, 2 * j2 + 1])
        return 0

    lax.fori_loop(0, i // 2, pair_step, 0)

    @pl.when(i % 2 == 1)
    def _odd_block():
        kv_step([i - 1])

    ks = pl.multiple_of(i * ATTN_K, ATTN_K)
    kp = kp_ref[pl.ds(ks, ATTN_K), :]
    kpm = kpm_ref[...]
    keep = (lax.broadcasted_iota(jnp.int32, (ATTN_K, tq), 0)
            <= lax.broadcasted_iota(jnp.int32, (ATTN_K, tq), 1))

    def score_diag(h):
        kc = jnp.concatenate([kn_ref[pl.ds(ks, ATTN_K), hs[h]], kp], axis=1)
        km = jnp.concatenate([knm_ref[:, hs[h]], kpm], axis=1)
        return jnp.where(keep, _dot_nt(kc, qs[h]), -jnp.inf), _dot_nt(km, qs[h])

    def finish(g, s):
        _, l, acc = update((m_ref[g], l_ref[g], acc_ref[g]), list(s),
                           [vt_ref[i, hs[g], :], vtm_ref[hs[g], :]])
        o_ref[:, hs[g]] = (acc / l).T.astype(BF16)

    skewed(score_diag, finish)


def _attention(qn, qp, kn, kp, vt, knm, kpm, vtm, batch, seq):
    assert ATTN_Q == ATTN_K == INPROJ_ROWS
    w = qn.shape[-1]
    nq = seq // ATTN_Q
    r3 = lambda a: a.reshape(batch, seq, a.shape[-1])
    qblk = pl.BlockSpec((None, ATTN_Q, w), lambda b, i: (b, i, 0))
    kfull = pl.BlockSpec((None, seq, w), lambda b, i: (b, 0, 0))
    kpfull = pl.BlockSpec((None, seq, V7X_LANES), lambda b, i: (b, 0, 0))
    vtfull = pl.BlockSpec((None, nq, w, ATTN_K), lambda b, i: (b, 0, 0, 0))
    mh = pl.BlockSpec((N_META, w), lambda b, i: (0, 0))
    mp = pl.BlockSpec((N_META, V7X_LANES), lambda b, i: (0, 0))
    mvt = pl.BlockSpec((w, N_META), lambda b, i: (0, 0))
    return pl.pallas_call(
        _attn_kernel,
        grid=(batch, nq),
        in_specs=[qblk, qblk, kfull, kpfull, vtfull, mh, mp, mvt],
        out_specs=qblk,
        out_shape=jax.ShapeDtypeStruct((batch, seq, w), BF16),
        scratch_shapes=[pltpu.VMEM((MLA_HEADS, 1, ATTN_Q), F32),
                        pltpu.VMEM((MLA_HEADS, 1, ATTN_Q), F32),
                        pltpu.VMEM((MLA_HEADS, V_HEAD, ATTN_Q), F32)],
        compiler_params=pltpu.CompilerParams(
            dimension_semantics=("parallel", "arbitrary"),
            vmem_limit_bytes=V7X_VMEM_LIMIT_BYTES),
        name="mla_attn",
    )(r3(qn), r3(qp), r3(kn), r3(kp), vt.reshape(batch, nq, w, ATTN_K), knm, kpm,
      vtm.reshape(w, N_META))


def _out_kernel(x_ref, oh_ref, oa_ref, gates_ref, who_ref, wmo_ref, wout_ref, gmix_ref,
                gfpre_ref, wfin_ref, wfout_ref, gfpost_ref, y_ref):
    d = x_ref.shape[-1]
    hidden = wfout_ref.shape[0]
    ya = _dot(oh_ref[...], who_ref[...])
    yb = _dot(oa_ref[...], wmo_ref[...])
    gates = gates_ref[...].astype(F32)
    merged = (gates[:, :d] * ya + gates[:, d:] * yb).astype(BF16)
    h1 = x_ref[...] + _rms(_dot(merged, wout_ref[...]), gmix_ref[...])
    u = _rms(h1, gfpre_ref[...]).astype(BF16)
    gu = _dot(u, wfin_ref[...])
    gt = gu[:, :hidden]
    act = (gt * _sigmoid(gt) * gu[:, hidden:]).astype(BF16)
    y_ref[...] = h1 + _rms(_dot(act, wfout_ref[...]), gfpost_ref[...])


def _out_block(x2d, oh, oa, gates, params):
    t, d = x2d.shape
    n = t // OUT_ROWS
    row = lambda w: pl.BlockSpec((OUT_ROWS, w), lambda i: (i, 0))
    return pl.pallas_call(
        _out_kernel,
        grid=(n,),
        in_specs=[row(d), row(d), row(d), row(2 * d)] + [_resident(p.shape) for p in params],
        out_specs=row(d),
        out_shape=jax.ShapeDtypeStruct((t, d), F32),
        compiler_params=pltpu.CompilerParams(
            dimension_semantics=("parallel",), vmem_limit_bytes=V7X_VMEM_LIMIT_BYTES),
        name="merge_ffn",
    )(x2d, oh, oa, gates, *params)


def _rope_tables(length):
    pos = jnp.arange(length, dtype=F32)
    inv_freq = 1.0 / (ROPE_THETA ** (jnp.arange(0, QK_ROPE, 2, dtype=F32) / QK_ROPE))
    ang = pos[:, None] * inv_freq[None, :]
    cos, sin = jnp.cos(ang), jnp.sin(ang)
    zero = jnp.zeros((length, V7X_LANES - QK_ROPE), F32)
    return (jnp.concatenate([cos, cos, zero], axis=1),
            jnp.concatenate([-sin, sin, zero], axis=1))


def _swap_halves(w):
    half = w.shape[-1] // 2
    return jnp.concatenate([w[..., half:], w[..., :half]], axis=-1)


def kernel(x, meta_tokens, w_in, b_gate, lb_logits, hg_norm_g, w_hg_o, q_a_norm_g, w_q_b,
           kv_a_norm_g, w_kv_b, w_mla_o, w_out, mix_pre_g, mix_post_g, ffn_pre_g, ffn_post_g,
           w_ffn_in, w_ffn_out):
    batch, seq, d = x.shape
    assert w_in.shape[0] == 1, "single-layer block"
    assert seq % INPROJ_ROWS == 0 and seq % ATTN_Q == 0 and seq % HG_BLOCK == 0
    assert (batch * seq) % OUT_ROWS == 0
    hgw = HG_HEADS * HG_D
    row = lambda a: a.reshape(1, -1).astype(F32)

    wi = w_in[0]
    o = 0
    parts = []
    for sz in (hgw, hgw, hgw, hgw, Q_LORA + KV_LORA, QK_ROPE, 2 * d):
        parts.append(wi[:, o:o + sz])
        o += sz
    whq, whf, whi, whg, wc, wkpe, wgate = parts
    zpad = jnp.zeros((d, V7X_LANES - QK_ROPE), wi.dtype)
    wpe = jnp.concatenate([wkpe, zpad, _swap_halves(wkpe), zpad], axis=1)
    wq = w_q_b[0].reshape(Q_LORA, MLA_HEADS, QK_NOPE + QK_ROPE)
    wqn = wq[:, :, :QK_NOPE].reshape(Q_LORA, MLA_HEADS * QK_NOPE)
    wq_pe = wq[:, :, QK_NOPE:]
    zq = jnp.zeros((Q_LORA, MLA_HEADS, V7X_LANES - QK_ROPE), wq.dtype)
    wqp = jnp.concatenate([wq_pe, zq], axis=2).reshape(Q_LORA, MLA_HEADS * V7X_LANES)
    wqpr = jnp.concatenate([_swap_halves(wq_pe), zq], axis=2).reshape(
        Q_LORA, MLA_HEADS * V7X_LANES)
    wkv = w_kv_b[0].reshape(KV_LORA, MLA_HEADS, QK_NOPE + V_HEAD)
    wkn = wkv[:, :, :QK_NOPE].reshape(KV_LORA, MLA_HEADS * QK_NOPE)
    wv = wkv[:, :, QK_NOPE:].reshape(KV_LORA, MLA_HEADS * V_HEAD).T
    bf = lambda a: a.astype(BF16)
    inproj_params = [row(mix_pre_g[0]), bf(whq), bf(whf), bf(whi), bf(whg), bf(wc), bf(wpe),
                     bf(wgate), row(b_gate[0]), lb_logits.astype(F32), row(q_a_norm_g[0]),
                     row(kv_a_norm_g[0]), bf(wqn), bf(wqp), bf(wqpr), bf(wkn), bf(wv)]
    cos_t, sin_t = _rope_tables(N_META + seq)

    m_out = _inproj(meta_tokens.astype(F32), N_META, 1, inproj_params,
                    cos_t[:N_META], sin_t[:N_META])
    _, kk_m, logf_m, vh_m, _, _, _, _, kn_m, kp_m, va_m = m_out

    x2d = x.reshape(batch * seq, d)
    (qh, kk, logf, vh, og, gates, qn, qp, kn, kp, va) = _inproj(
        x2d, INPROJ_ROWS, seq // INPROJ_ROWS, inproj_params, cos_t[N_META:], sin_t[N_META:])

    o_hg = _hgrn(qh, kk, logf, vh, og, kk_m, logf_m, vh_m, row(hg_norm_g[0]), batch, seq)
    o_at = _attention(qn, qp, kn, kp, va, kn_m, kp_m, va_m, batch, seq)

    out_params = [bf(w_hg_o[0]), bf(w_mla_o[0]), bf(w_out[0]), row(mix_post_g[0]),
                  row(ffn_pre_g[0]), bf(w_ffn_in[0]), bf(w_ffn_out[0]), row(ffn_post_g[0])]
    y = _out_block(x2d, o_hg.reshape(batch * seq, hgw), o_at.reshape(batch * seq, -1),
                   gates, out_params)
    return y.reshape(batch, seq, d)
```

```python
import functools
import math

import jax
import jax.numpy as jnp
from jax import lax
from jax.experimental import pallas as pl
from jax.experimental.pallas import tpu as pltpu

N_META = 16
NORM_EPS = 1e-6
HG_HEADS = 8
HG_D = 128
MLA_HEADS = 8
QK_NOPE = 128
QK_ROPE = 64
V_HEAD = 128
Q_LORA = 256
KV_LORA = 256
ROPE_THETA = 10000.0

V7X_LANES = 128
V7X_VMEM_LIMIT_BYTES = 56 * 1024 * 1024

INPROJ_ROWS = 256
HG_BLOCK = 64
HG_STEP = 256
ATTN_Q = 256
ATTN_K = 256
ATTN_SKEW = 4
OUT_ROWS = 256

F32 = jnp.float32
BF16 = jnp.bfloat16


def _dot(a, b):
    return jnp.dot(a, b, preferred_element_type=F32)


def _dot_nt(a, b):
    return lax.dot_general(a, b, (((1,), (1,)), ((), ())), preferred_element_type=F32)


def _dot_tn(a, b):
    return lax.dot_general(a, b, (((0,), (0,)), ((), ())), preferred_element_type=F32)


def _rms(x, g):
    ms = jnp.mean(x * x, axis=-1, keepdims=True)
    return x * lax.rsqrt(ms + NORM_EPS) * g


def _sigmoid(x):
    return 1.0 / (1.0 + jnp.exp(-x))


def _resident(shape):
    return pl.BlockSpec(shape, lambda *_: (0,) * len(shape), pipeline_mode=pl.Buffered(1))


def _cumsum_blocks(x, block):
    pos = lax.broadcasted_iota(jnp.int32, x.shape, 0) % block
    s = 1
    while s < block:
        x = x + jnp.where(pos >= s, pltpu.roll(x, s, 0), 0.0)
        s *= 2
    return x


def _inproj_kernel(x_ref, gpre_ref, whq_ref, whf_ref, whi_ref, whg_ref, wc_ref, wpe_ref,
                   wgate_ref, bgate_ref, lbl_ref, qg_ref, kvg_ref, wqn_ref, wqp_ref,
                   wqpr_ref, wkn_ref, wv_ref, cos_ref, sin_ref,
                   qi_ref, kl_ref, dec_ref, vh_ref, og_ref, gates_ref,
                   qn_ref, qp_ref, kn_ref, kp_ref, va_ref, *maybe_a_ref, scale, hg_block):
    u = _rms(x_ref[...], gpre_ref[...]).astype(BF16)
    rows = x_ref.shape[0]

    lbl = lbl_ref[...]
    e = jnp.exp(lbl - jnp.max(lbl, axis=0, keepdims=True))
    lb = e[0:1] / jnp.sum(e, axis=0, keepdims=True)

    d = x_ref.shape[1]
    hq = _dot(u, whq_ref[...])
    hf = _dot(u, whf_ref[...])
    vh_ref[...] = _dot(u, whi_ref[...]).astype(BF16)
    pe = _dot(u, wpe_ref[...])
    cmla = _dot(u, wc_ref[...])
    q = hq * _sigmoid(hq)
    sg = _sigmoid(hf)
    k = (1.0 - lb) * (1.0 - sg)
    c = _cumsum_blocks(jnp.log(lb + (1.0 - lb) * sg), hg_block)
    cos = cos_ref[...]
    sin = sin_ref[...]

    def gate_half(half):
        cols = slice(half * d, (half + 1) * d)
        gates_ref[:, cols] = _sigmoid(
            _dot(u, wgate_ref[:, cols]) + bgate_ref[:, cols]).astype(BF16)

    def out_gate():
        hg = _dot(u, whg_ref[...])
        og_ref[...] = (hg * _sigmoid(hg)).astype(BF16)

    def mla_q():
        cq = _rms(cmla[:, :Q_LORA], qg_ref[...]).astype(BF16)
        qn_ref[...] = (_dot(cq, wqn_ref[...]) * scale).astype(BF16)
        qpe = _dot(cq, wqp_ref[...])
        qper = _dot(cq, wqpr_ref[...])
        for h in range(MLA_HEADS):
            sl = slice(h * V7X_LANES, (h + 1) * V7X_LANES)
            qp_ref[:, sl] = ((qpe[:, sl] * cos + qper[:, sl] * sin) * scale).astype(BF16)

    fillers = [out_gate, functools.partial(gate_half, 0), functools.partial(gate_half, 1), mla_q]

    mid = hg_block // 2 - 1
    pair_w = 2 * HG_D
    zero = jnp.zeros((hg_block, HG_D), BF16)
    if maybe_a_ref:
        tri = (lax.broadcasted_iota(jnp.int32, (hg_block, 2 * hg_block), 0)
               >= lax.broadcasted_iota(jnp.int32, (hg_block, 2 * hg_block), 1) % hg_block)
    for blk in range(rows // hg_block):
        r = slice(blk * hg_block, (blk + 1) * hg_block)
        cb = c[r]
        cm = cb[mid:mid + 1]
        cl = cb[hg_block - 1:hg_block]
        qt = q[r] * jnp.exp(cb - cm)
        kt = k[r] * jnp.exp(cm - cb)
        qi_ref[r, :] = (qt * jnp.exp(cm)).astype(BF16)
        kl_ref[r, :] = (kt * jnp.exp(cl - cm)).astype(BF16)
        dec_ref[blk:blk + 1, :] = jnp.exp(cl)
        if maybe_a_ref:
            qt = qt.astype(BF16)
            kt = kt.astype(BF16)
            for p in range(HG_HEADS // 2):
                ls = slice(p * pair_w, (p + 1) * pair_w)
                kb = jnp.concatenate(
                    [jnp.concatenate([kt[:, ls][:, :HG_D], zero], axis=1),
                     jnp.concatenate([zero, kt[:, ls][:, HG_D:]], axis=1)], axis=0)
                a = jnp.where(tri, _dot_nt(qt[:, ls], kb), 0.0)
                maybe_a_ref[0][r, p * 2 * hg_block:(p + 1) * 2 * hg_block] = a.astype(BF16)
        if fillers:
            fillers.pop(0)()
    while fillers:
        fillers.pop(0)()

    kp_ref[...] = (pe[:, :V7X_LANES] * cos + pe[:, V7X_LANES:] * sin).astype(BF16)
    ckv = _rms(cmla[:, Q_LORA:], kvg_ref[...]).astype(BF16)
    kn_ref[...] = _dot(ckv, wkn_ref[...]).astype(BF16)
    va_ref[...] = _dot_nt(wv_ref[...], ckv).astype(BF16)


def _inproj(x2d, rows, pos_blocks, params, cos_t, sin_t, with_intra):
    t, d = x2d.shape
    n = t // rows
    row = lambda w: pl.BlockSpec((rows, w), lambda i: (i, 0))
    pos = pl.BlockSpec((rows, V7X_LANES), lambda i: (i % pos_blocks, 0))
    w_specs = [_resident(p.shape) for p in params]
    hg_block = min(HG_BLOCK, rows)
    nb = rows // hg_block
    vt_w = MLA_HEADS * V_HEAD
    lead = lambda *shape: pl.BlockSpec((None,) + shape, lambda i: (i,) + (0,) * len(shape))
    outs = [(row(d), (t, d), BF16), (row(d), (t, d), BF16),
            (lead(nb, d), (n, nb, d), F32),
            (row(d), (t, d), BF16), (row(d), (t, d), BF16), (row(2 * d), (t, 2 * d), BF16),
            (row(d), (t, d), BF16), (row(d), (t, d), BF16), (row(d), (t, d), BF16),
            (row(V7X_LANES), (t, V7X_LANES), BF16),
            (lead(vt_w, rows), (n, vt_w, rows), BF16)]
    if with_intra:
        a_w = HG_HEADS * hg_block
        outs.append((row(a_w), (t, a_w), BF16))
    scale = (QK_NOPE + QK_ROPE) ** -0.5 * math.log2(math.e)
    return pl.pallas_call(
        functools.partial(_inproj_kernel, scale=scale, hg_block=hg_block),
        grid=(n,),
        in_specs=[row(d)] + w_specs + [pos, pos],
        out_specs=[o[0] for o in outs],
        out_shape=[jax.ShapeDtypeStruct(o[1], o[2]) for o in outs],
        compiler_params=pltpu.CompilerParams(
            dimension_semantics=("parallel",), vmem_limit_bytes=V7X_VMEM_LIMIT_BYTES),
        name="inproj",
    )(x2d, *params, cos_t, sin_t)


def _hgrn_kernel(a_ref, qi_ref, kl_ref, dec_ref, v_ref, og_ref, klm_ref, vm_ref, g_ref,
                 o_ref, st_ref):
    n = pl.program_id(1)
    c_blk = qi_ref.shape[0]
    hs = [slice(h * HG_D, (h + 1) * HG_D) for h in range(HG_HEADS)]

    @pl.when(n == 0)
    def _init():
        vm = vm_ref[...]
        klm = klm_ref[...]
        for h, sl in enumerate(hs):
            st_ref[h] = _dot_tn(vm[:, sl], klm[:, sl])

    g = g_ref[...]
    zero = jnp.zeros((HG_BLOCK, HG_D), BF16)
    for blk in range(c_blk // HG_BLOCK):
        r = slice(blk * HG_BLOCK, (blk + 1) * HG_BLOCK)
        a = a_ref[r, :]
        qi = qi_ref[r, :]
        kl = kl_ref[r, :]
        v = v_ref[r, :]
        dec = dec_ref[blk]
        outs = []
        for p in range(HG_HEADS // 2):
            ha, hb = hs[2 * p], hs[2 * p + 1]
            vb = jnp.concatenate([jnp.concatenate([v[:, ha], zero], axis=1),
                                  jnp.concatenate([zero, v[:, hb]], axis=1)], axis=0)
            o_pair = _dot(a[:, p * 2 * HG_BLOCK:(p + 1) * 2 * HG_BLOCK], vb)
            for e, sl in enumerate((ha, hb)):
                h = 2 * p + e
                st = st_ref[h]
                outs.append(o_pair[:, e * HG_D:(e + 1) * HG_D]
                            + _dot_nt(qi[:, sl], st.astype(BF16)))
                st_ref[h] = st * dec[:, sl] + _dot_tn(v[:, sl], kl[:, sl])
        for o, sl in zip(outs, hs):
            o_ref[r, sl] = (_rms(o, g) * og_ref[r, sl].astype(F32)).astype(BF16)


def _hgrn(a, qi, kl, dec, vh, og, klm, vm, g, batch, seq):
    w = qi.shape[-1]
    nstep = seq // HG_STEP
    sub = HG_STEP // HG_BLOCK
    tok = lambda width: pl.BlockSpec((None, HG_STEP, width), lambda b, n: (b, n, 0))
    meta = pl.BlockSpec((N_META, w), lambda b, n: (0, 0))
    r3 = lambda x: x.reshape(batch, seq, x.shape[-1])
    return pl.pallas_call(
        _hgrn_kernel,
        grid=(batch, nstep),
        in_specs=[tok(a.shape[-1]), tok(w), tok(w),
                  pl.BlockSpec((None, sub, 1, w), lambda b, n: (b, n, 0, 0)),
                  tok(w), tok(w), meta, meta,
                  pl.BlockSpec((1, HG_D), lambda b, n: (0, 0))],
        out_specs=tok(w),
        out_shape=jax.ShapeDtypeStruct((batch, seq, w), BF16),
        scratch_shapes=[pltpu.VMEM((HG_HEADS, HG_D, HG_D), F32)],
        compiler_params=pltpu.CompilerParams(
            dimension_semantics=("parallel", "arbitrary"),
            vmem_limit_bytes=V7X_VMEM_LIMIT_BYTES),
        name="hgrn2",
    )(r3(a), r3(qi), r3(kl), dec.reshape(batch, seq // HG_BLOCK, 1, w), r3(vh), r3(og), klm,
      vm, g)


def _attn_kernel(qn_ref, qp_ref, kn_ref, kp_ref, vt_ref, knm_ref, kpm_ref, vtm_ref, o_ref,
                 m_ref, l_ref, acc_ref):
    i = pl.program_id(1)
    tq = qn_ref.shape[0]
    hs = [slice(h * V7X_LANES, (h + 1) * V7X_LANES) for h in range(MLA_HEADS)]
    qs = [jnp.concatenate([qn_ref[:, sl], qp_ref[:, sl]], axis=1) for sl in hs]

    def update(carry, s_list, vt_list):
        m, l, acc = carry
        m_new = m
        for s in s_list:
            m_new = jnp.maximum(m_new, jnp.max(s, axis=0, keepdims=True))
        alpha = jnp.exp2(m - m_new)
        l = alpha * l
        acc = alpha * acc
        for s, vt in zip(s_list, vt_list):
            p = jnp.exp2(s - m_new)
            l = l + jnp.sum(p, axis=0, keepdims=True)
            acc = acc + _dot(vt, p.astype(BF16))
        return m_new, l, acc

    def skewed(score_fn, consume_fn):
        scores = []
        for h in range(MLA_HEADS + ATTN_SKEW):
            if h < MLA_HEADS:
                scores.append(score_fn(h))
            g = h - ATTN_SKEW
            if g >= 0:
                consume_fn(g, scores[g])

    for h in range(MLA_HEADS):
        m_ref[h] = jnp.full((1, tq), -jnp.inf, F32)
        l_ref[h] = jnp.zeros((1, tq), F32)
        acc_ref[h] = jnp.zeros((V_HEAD, tq), F32)

    def kv_step(blocks):
        starts = [pl.multiple_of(b * ATTN_K, ATTN_K) for b in blocks]
        kps = [kp_ref[pl.ds(ks, ATTN_K), :] for ks in starts]

        def score(h):
            return [_dot_nt(jnp.concatenate([kn_ref[pl.ds(ks, ATTN_K), hs[h]], kp], axis=1),
                            qs[h]) for ks, kp in zip(starts, kps)]

        def consume(g, s_list):
            m, l, acc = update((m_ref[g], l_ref[g], acc_ref[g]), s_list,
                               [vt_ref[b, hs[g], :] for b in blocks])
            m_ref[g] = m
            l_ref[g] = l
            acc_ref[g] = acc

        skewed(score, consume)

    def pair_step(j2, _):
        kv_step([2 * j2, 2 * j2 + 1])
        return 0

    lax.fori_loop(0, i // 2, pair_step, 0)

    @pl.when(i % 2 == 1)
    def _odd_block():
        kv_step([i - 1])

    ks = pl.multiple_of(i * ATTN_K, ATTN_K)
    kp = kp_ref[pl.ds(ks, ATTN_K), :]
    kpm = kpm_ref[...]
    keep = (lax.broadcasted_iota(jnp.int32, (ATTN_K, tq), 0)
            <= lax.broadcasted_iota(jnp.int32, (ATTN_K, tq), 1))

    def score_diag(h):
        kc = jnp.concatenate([kn_ref[pl.ds(ks, ATTN_K), hs[h]], kp], axis=1)
        km = jnp.concatenate([knm_ref[:, hs[h]], kpm], axis=1)
        return jnp.where(keep, _dot_nt(kc, qs[h]), -jnp.inf), _dot_nt(km, qs[h])

    def finish(g, s):
        _, l, acc = update((m_ref[g], l_ref[g], acc_ref[g]), list(s),
                           [vt_ref[i, hs[g], :], vtm_ref[hs[g], :]])
        o_ref[:, hs[g]] = (acc / l).T.astype(BF16)

    skewed(score_diag, finish)


def _attention(qn, qp, kn, kp, vt, knm, kpm, vtm, batch, seq):
    assert ATTN_Q == ATTN_K == INPROJ_ROWS
    w = qn.shape[-1]
    nq = seq // ATTN_Q
    r3 = lambda a: a.reshape(batch, seq, a.shape[-1])
    qblk = pl.BlockSpec((None, ATTN_Q, w), lambda b, i: (b, i, 0))
    kfull = pl.BlockSpec((None, seq, w), lambda b, i: (b, 0, 0))
    kpfull = pl.BlockSpec((None, seq, V7X_LANES), lambda b, i: (b, 0, 0))
    vtfull = pl.BlockSpec((None, nq, w, ATTN_K), lambda b, i: (b, 0, 0, 0))
    mh = pl.BlockSpec((N_META, w), lambda b, i: (0, 0))
    mp = pl.BlockSpec((N_META, V7X_LANES), lambda b, i: (0, 0))
    mvt = pl.BlockSpec((w, N_META), lambda b, i: (0, 0))
    return pl.pallas_call(
        _attn_kernel,
        grid=(batch, nq),
        in_specs=[qblk, qblk, kfull, kpfull, vtfull, mh, mp, mvt],
        out_specs=qblk,
        out_shape=jax.ShapeDtypeStruct((batch, seq, w), BF16),
        scratch_shapes=[pltpu.VMEM((MLA_HEADS, 1, ATTN_Q), F32),
                        pltpu.VMEM((MLA_HEADS, 1, ATTN_Q), F32),
                        pltpu.VMEM((MLA_HEADS, V_HEAD, ATTN_Q), F32)],
        compiler_params=pltpu.CompilerParams(
            dimension_semantics=("parallel", "arbitrary"),
            vmem_limit_bytes=V7X_VMEM_LIMIT_BYTES),
        name="mla_attn",
    )(r3(qn), r3(qp), r3(kn), r3(kp), vt.reshape(batch, nq, w, ATTN_K), knm, kpm,
      vtm.reshape(w, N_META))


def _out_kernel(x_ref, oh_ref, oa_ref, gates_ref, who_ref, wmo_ref, wout_ref, gmix_ref,
                gfpre_ref, wfin_ref, wfout_ref, gfpost_ref, y_ref):
    d = x_ref.shape[-1]
    hidden = wfout_ref.shape[0]
    ya = _dot(oh_ref[...], who_ref[...])
    yb = _dot(oa_ref[...], wmo_ref[...])
    gates = gates_ref[...].astype(F32)
    merged = (gates[:, :d] * ya + gates[:, d:] * yb).astype(BF16)
    h1 = x_ref[...] + _rms(_dot(merged, wout_ref[...]), gmix_ref[...])
    u = _rms(h1, gfpre_ref[...]).astype(BF16)
    gu = _dot(u, wfin_ref[...])
    gt = gu[:, :hidden]
    act = (gt * _sigmoid(gt) * gu[:, hidden:]).astype(BF16)
    y_ref[...] = h1 + _rms(_dot(act, wfout_ref[...]), gfpost_ref[...])


def _out_block(x2d, oh, oa, gates, params):
    t, d = x2d.shape
    n = t // OUT_ROWS
    row = lambda w: pl.BlockSpec((OUT_ROWS, w), lambda i: (i, 0))
    return pl.pallas_call(
        _out_kernel,
        grid=(n,),
        in_specs=[row(d), row(d), row(d), row(2 * d)] + [_resident(p.shape) for p in params],
        out_specs=row(d),
        out_shape=jax.ShapeDtypeStruct((t, d), F32),
        compiler_params=pltpu.CompilerParams(
            dimension_semantics=("parallel",), vmem_limit_bytes=V7X_VMEM_LIMIT_BYTES),
        name="merge_ffn",
    )(x2d, oh, oa, gates, *params)


def _rope_tables(length):
    pos = jnp.arange(length, dtype=F32)
    inv_freq = 1.0 / (ROPE_THETA ** (jnp.arange(0, QK_ROPE, 2, dtype=F32) / QK_ROPE))
    ang = pos[:, None] * inv_freq[None, :]
    cos, sin = jnp.cos(ang), jnp.sin(ang)
    zero = jnp.zeros((length, V7X_LANES - QK_ROPE), F32)
    return (jnp.concatenate([cos, cos, zero], axis=1),
            jnp.concatenate([-sin, sin, zero], axis=1))


def _swap_halves(w):
    half = w.shape[-1] // 2
    return jnp.concatenate([w[..., half:], w[..., :half]], axis=-1)


def kernel(x, meta_tokens, w_in, b_gate, lb_logits, hg_norm_g, w_hg_o, q_a_norm_g, w_q_b,
           kv_a_norm_g, w_kv_b, w_mla_o, w_out, mix_pre_g, mix_post_g, ffn_pre_g, ffn_post_g,
           w_ffn_in, w_ffn_out):
    batch, seq, d = x.shape
    assert w_in.shape[0] == 1, "single-layer block"
    assert seq % INPROJ_ROWS == 0 and seq % ATTN_Q == 0 and seq % HG_STEP == 0
    assert INPROJ_ROWS % HG_BLOCK == 0 and HG_STEP % HG_BLOCK == 0
    assert (batch * seq) % OUT_ROWS == 0
    hgw = HG_HEADS * HG_D
    row = lambda a: a.reshape(1, -1).astype(F32)

    wi = w_in[0]
    o = 0
    parts = []
    for sz in (hgw, hgw, hgw, hgw, Q_LORA + KV_LORA, QK_ROPE, 2 * d):
        parts.append(wi[:, o:o + sz])
        o += sz
    whq, whf, whi, whg, wc, wkpe, wgate = parts
    zpad = jnp.zeros((d, V7X_LANES - QK_ROPE), wi.dtype)
    wpe = jnp.concatenate([wkpe, zpad, _swap_halves(wkpe), zpad], axis=1)
    wq = w_q_b[0].reshape(Q_LORA, MLA_HEADS, QK_NOPE + QK_ROPE)
    wqn = wq[:, :, :QK_NOPE].reshape(Q_LORA, MLA_HEADS * QK_NOPE)
    wq_pe = wq[:, :, QK_NOPE:]
    zq = jnp.zeros((Q_LORA, MLA_HEADS, V7X_LANES - QK_ROPE), wq.dtype)
    wqp = jnp.concatenate([wq_pe, zq], axis=2).reshape(Q_LORA, MLA_HEADS * V7X_LANES)
    wqpr = jnp.concatenate([_swap_halves(wq_pe), zq], axis=2).reshape(
        Q_LORA, MLA_HEADS * V7X_LANES)
    wkv = w_kv_b[0].reshape(KV_LORA, MLA_HEADS, QK_NOPE + V_HEAD)
    wkn = wkv[:, :, :QK_NOPE].reshape(KV_LORA, MLA_HEADS * QK_NOPE)
    wv = wkv[:, :, QK_NOPE:].reshape(KV_LORA, MLA_HEADS * V_HEAD).T
    bf = lambda a: a.astype(BF16)
    inproj_params = [row(mix_pre_g[0]), bf(whq), bf(whf), bf(whi), bf(whg), bf(wc), bf(wpe),
                     bf(wgate), row(b_gate[0]), lb_logits.astype(F32), row(q_a_norm_g[0]),
                     row(kv_a_norm_g[0]), bf(wqn), bf(wqp), bf(wqpr), bf(wkn), bf(wv)]
    cos_t, sin_t = _rope_tables(N_META + seq)

    m_out = _inproj(meta_tokens.astype(F32), N_META, 1, inproj_params,
                    cos_t[:N_META], sin_t[:N_META], with_intra=False)
    _, kl_m, _, vh_m, _, _, _, _, kn_m, kp_m, va_m = m_out

    x2d = x.reshape(batch * seq, d)
    (qi, kl, dec, vh, og, gates, qn, qp, kn, kp, va, a_hg) = _inproj(
        x2d, INPROJ_ROWS, seq // INPROJ_ROWS, inproj_params, cos_t[N_META:], sin_t[N_META:],
        with_intra=True)

    o_hg = _hgrn(a_hg, qi, kl, dec, vh, og, kl_m, vh_m, row(hg_norm_g[0]), batch, seq)
    o_at = _attention(qn, qp, kn, kp, va, kn_m, kp_m, va_m, batch, seq)

    out_params = [bf(w_hg_o[0]), bf(w_mla_o[0]), bf(w_out[0]), row(mix_post_g[0]),
                  row(ffn_pre_g[0]), bf(w_ffn_in[0]), bf(w_ffn_out[0]), row(ffn_post_g[0])]
    y = _out_block(x2d, o_hg.reshape(batch * seq, hgw), o_at.reshape(batch * seq, -1),
                   gates, out_params)
    return y.reshape(batch, seq, d)
```

```python
import functools
import math

import jax
import jax.numpy as jnp
from jax import lax
from jax.experimental import pallas as pl
from jax.experimental.pallas import tpu as pltpu

N_META = 16
NORM_EPS = 1e-6
HG_HEADS = 8
HG_D = 128
MLA_HEADS = 8
QK_NOPE = 128
QK_ROPE = 64
V_HEAD = 128
Q_LORA = 256
KV_LORA = 256
ROPE_THETA = 10000.0

V7X_LANES = 128
V7X_VMEM_LIMIT_BYTES = 56 * 1024 * 1024

INPROJ_ROWS = 256
HG_BLOCK = 64
HG_STEP = 256
HG_SAFE_EXPONENT = 60.0
ATTN_Q = 256
ATTN_K = 256
ATTN_SKEW = 8
OUT_ROWS = 256

F32 = jnp.float32
BF16 = jnp.bfloat16


def _dot(a, b):
    return jnp.dot(a, b, preferred_element_type=F32)


def _dot_nt(a, b):
    return lax.dot_general(a, b, (((1,), (1,)), ((), ())), preferred_element_type=F32)


def _dot_tn(a, b):
    return lax.dot_general(a, b, (((0,), (0,)), ((), ())), preferred_element_type=F32)


def _rms(x, g):
    ms = jnp.mean(x * x, axis=-1, keepdims=True)
    return x * lax.rsqrt(ms + NORM_EPS) * g


def _sigmoid(x):
    return 1.0 / (1.0 + jnp.exp(-x))


def _resident(shape):
    return pl.BlockSpec(shape, lambda *_: (0,) * len(shape), pipeline_mode=pl.Buffered(1))


def _cumsum_blocks(x, block):
    pos = lax.broadcasted_iota(jnp.int32, x.shape, 0) % block
    s = 1
    while s < block:
        x = x + jnp.where(pos >= s, pltpu.roll(x, s, 0), 0.0)
        s *= 2
    return x


def _inproj_kernel(x_ref, gpre_ref, whq_ref, whf_ref, whi_ref, whg_ref, wc_ref, wpe_ref,
                   wgate_ref, bgate_ref, lbl_ref, qg_ref, kvg_ref, wqn_ref, wqp_ref,
                   wqpr_ref, wkn_ref, wv_ref, cos_ref, sin_ref,
                   qi_ref, kl_ref, dec_ref, vh_ref, og_ref, gates_ref,
                   q_ref, kn_ref, kp_ref, va_ref, *intra_refs, scale, hg_block):
    u = _rms(x_ref[...], gpre_ref[...]).astype(BF16)
    rows = x_ref.shape[0]

    lbl = lbl_ref[...]
    e = jnp.exp(lbl - jnp.max(lbl, axis=0, keepdims=True))
    lb = e[0:1] / jnp.sum(e, axis=0, keepdims=True)

    d = x_ref.shape[1]
    hq = _dot(u, whq_ref[...])
    hf = _dot(u, whf_ref[...])
    vh_ref[...] = _dot(u, whi_ref[...]).astype(BF16)
    pe = _dot(u, wpe_ref[...])
    cmla = _dot(u, wc_ref[...])
    q = hq * _sigmoid(hq)
    sg = _sigmoid(hf)
    k = (1.0 - lb) * (1.0 - sg)
    c = _cumsum_blocks(jnp.log(lb + (1.0 - lb) * sg), hg_block)
    cos = cos_ref[...]
    sin = sin_ref[...]

    def gate_half(half):
        cols = slice(half * d, (half + 1) * d)
        gates_ref[:, cols] = _sigmoid(
            _dot(u, wgate_ref[:, cols]) + bgate_ref[:, cols]).astype(BF16)

    def out_gate():
        hg = _dot(u, whg_ref[...])
        og_ref[...] = (hg * _sigmoid(hg)).astype(BF16)

    def mla_q():
        cq = _rms(cmla[:, :Q_LORA], qg_ref[...]).astype(BF16)
        qn = _dot(cq, wqn_ref[...])
        qpe = _dot(cq, wqp_ref[...])
        qper = _dot(cq, wqpr_ref[...])
        for h in range(MLA_HEADS):
            sl = slice(h * V7X_LANES, (h + 1) * V7X_LANES)
            base = 2 * h * V7X_LANES
            q_ref[:, base:base + V7X_LANES] = (qn[:, sl] * scale).astype(BF16)
            q_ref[:, base + V7X_LANES:base + 2 * V7X_LANES] = (
                (qpe[:, sl] * cos + qper[:, sl] * sin) * scale).astype(BF16)

    fillers = [out_gate, functools.partial(gate_half, 0), functools.partial(gate_half, 1), mla_q]

    mid = hg_block // 2 - 1
    pair_w = 2 * HG_D
    zero = jnp.zeros((hg_block, HG_D), BF16)
    nblk = rows // hg_block
    if intra_refs:
        a_ref, ksc_ref, csc_ref = intra_refs
        tri = (lax.broadcasted_iota(jnp.int32, (hg_block, 2 * hg_block), 0)
               >= lax.broadcasted_iota(jnp.int32, (hg_block, 2 * hg_block), 1) % hg_block)
        worst = jnp.zeros((1, c.shape[1]), F32)
    for blk in range(nblk):
        r = slice(blk * hg_block, (blk + 1) * hg_block)
        cb = c[r]
        cm = cb[mid:mid + 1]
        cl = cb[hg_block - 1:hg_block]
        qt = q[r] * jnp.exp(cb - cm)
        kt = k[r] * jnp.exp(cm - cb)
        qi_ref[r, :] = (q[r] * jnp.exp(cb)).astype(BF16)
        kl_ref[r, :] = (k[r] * jnp.exp(cl - cb)).astype(BF16)
        dec_ref[blk:blk + 1, :] = jnp.exp(cl)
        if intra_refs:
            worst = jnp.maximum(worst, jnp.maximum(cb[0:1] - cm, cm - cl))
            qt = qt.astype(BF16)
            kt = kt.astype(BF16)
            for p in range(HG_HEADS // 2):
                ls = slice(p * pair_w, (p + 1) * pair_w)
                kb = jnp.concatenate(
                    [jnp.concatenate([kt[:, ls][:, :HG_D], zero], axis=1),
                     jnp.concatenate([zero, kt[:, ls][:, HG_D:]], axis=1)], axis=0)
                a = jnp.where(tri, _dot_nt(qt[:, ls], kb), 0.0)
                a_ref[r, p * 2 * hg_block:(p + 1) * 2 * hg_block] = a.astype(BF16)
        if fillers:
            fillers.pop(0)()
    while fillers:
        fillers.pop(0)()

    kp_ref[...] = (pe[:, :V7X_LANES] * cos + pe[:, V7X_LANES:] * sin).astype(BF16)
    ckv = _rms(cmla[:, Q_LORA:], kvg_ref[...]).astype(BF16)
    kn_ref[...] = _dot(ckv, wkn_ref[...]).astype(BF16)
    va_ref[...] = _dot_nt(wv_ref[...], ckv).astype(BF16)

    if intra_refs:
        @pl.when(jnp.max(worst) > HG_SAFE_EXPONENT)
        def _exact_intra():
            width = c.shape[1]
            a_w = HG_HEADS * hg_block
            ksc_ref[...] = k
            csc_ref[...] = c
            head_of_lane = lax.broadcasted_iota(jnp.int32, (width, HG_HEADS), 0) // HG_D
            head_sum = (head_of_lane
                        == lax.broadcasted_iota(jnp.int32, (width, HG_HEADS), 1)).astype(F32)
            col_base = lax.broadcasted_iota(jnp.int32, (HG_HEADS, a_w), 0) * hg_block
            col_lane = lax.broadcasted_iota(jnp.int32, (HG_HEADS, a_w), 1)
            causal = (lax.broadcasted_iota(jnp.int32, (hg_block, a_w), 0)
                      >= lax.broadcasted_iota(jnp.int32, (hg_block, a_w), 1) % hg_block)
            for blk in range(nblk):
                r = slice(blk * hg_block, (blk + 1) * hg_block)
                qb = q[r]
                cb = c[r]

                def key_column(s, acc):
                    row = blk * hg_block + s
                    w = (qb * jnp.exp(jnp.minimum(cb - csc_ref[pl.ds(row, 1), :], 0.0))
                         * ksc_ref[pl.ds(row, 1), :])
                    cols = _dot(w, head_sum)
                    place = (col_lane == col_base + s).astype(F32)
                    return acc + _dot(cols, place)

                acc = lax.fori_loop(0, hg_block, key_column, jnp.zeros((hg_block, a_w), F32))
                a_ref[r, :] = jnp.where(causal, acc, 0.0).astype(BF16)


def _inproj(x2d, rows, pos_blocks, params, cos_t, sin_t, with_intra):
    t, d = x2d.shape
    n = t // rows
    row = lambda w: pl.BlockSpec((rows, w), lambda i: (i, 0))
    pos = pl.BlockSpec((rows, V7X_LANES), lambda i: (i % pos_blocks, 0))
    w_specs = [_resident(p.shape) for p in params]
    hg_block = min(HG_BLOCK, rows)
    nb = rows // hg_block
    vt_w = MLA_HEADS * V_HEAD
    lead = lambda *shape: pl.BlockSpec((None,) + shape, lambda i: (i,) + (0,) * len(shape))
    outs = [(row(d), (t, d), BF16), (row(d), (t, d), BF16),
            (lead(nb, d), (n, nb, d), F32),
            (row(d), (t, d), BF16), (row(d), (t, d), BF16), (row(2 * d), (t, 2 * d), BF16),
            (row(2 * d), (t, 2 * d), BF16), (row(d), (t, d), BF16),
            (row(V7X_LANES), (t, V7X_LANES), BF16),
            (lead(vt_w, rows), (n, vt_w, rows), BF16)]
    scratch = []
    if with_intra:
        a_w = HG_HEADS * hg_block
        outs.append((row(a_w), (t, a_w), BF16))
        scratch = [pltpu.VMEM((rows, d), F32), pltpu.VMEM((rows, d), F32)]
    scale = (QK_NOPE + QK_ROPE) ** -0.5 * math.log2(math.e)
    return pl.pallas_call(
        functools.partial(_inproj_kernel, scale=scale, hg_block=hg_block),
        grid=(n,),
        in_specs=[row(d)] + w_specs + [pos, pos],
        out_specs=[o[0] for o in outs],
        out_shape=[jax.ShapeDtypeStruct(o[1], o[2]) for o in outs],
        scratch_shapes=scratch,
        compiler_params=pltpu.CompilerParams(
            dimension_semantics=("parallel",), vmem_limit_bytes=V7X_VMEM_LIMIT_BYTES),
        name="inproj",
    )(x2d, *params, cos_t, sin_t)


def _hgrn_kernel(a_ref, qi_ref, kl_ref, dec_ref, v_ref, og_ref, klm_ref, vm_ref, g_ref,
                 o_ref, st_ref):
    n = pl.program_id(1)
    c_blk = qi_ref.shape[0]
    hs = [slice(h * HG_D, (h + 1) * HG_D) for h in range(HG_HEADS)]

    @pl.when(n == 0)
    def _init():
        vm = vm_ref[...]
        klm = klm_ref[...]
        for h, sl in enumerate(hs):
            st_ref[h] = _dot_tn(vm[:, sl], klm[:, sl])

    g = g_ref[...]
    zero = jnp.zeros((HG_BLOCK, HG_D), BF16)
    for blk in range(c_blk // HG_BLOCK):
        r = slice(blk * HG_BLOCK, (blk + 1) * HG_BLOCK)
        a = a_ref[r, :]
        qi = qi_ref[r, :]
        kl = kl_ref[r, :]
        v = v_ref[r, :]
        dec = dec_ref[blk]
        outs = []
        for p in range(HG_HEADS // 2):
            ha, hb = hs[2 * p], hs[2 * p + 1]
            vb = jnp.concatenate([jnp.concatenate([v[:, ha], zero], axis=1),
                                  jnp.concatenate([zero, v[:, hb]], axis=1)], axis=0)
            o_pair = _dot(a[:, p * 2 * HG_BLOCK:(p + 1) * 2 * HG_BLOCK], vb)
            for e, sl in enumerate((ha, hb)):
                h = 2 * p + e
                st = st_ref[h]
                outs.append(o_pair[:, e * HG_D:(e + 1) * HG_D]
                            + _dot_nt(qi[:, sl], st.astype(BF16)))
                st_ref[h] = st * dec[:, sl] + _dot_tn(v[:, sl], kl[:, sl])
        for o, sl in zip(outs, hs):
            o_ref[r, sl] = (_rms(o, g) * og_ref[r, sl].astype(F32)).astype(BF16)


def _hgrn(a, qi, kl, dec, vh, og, klm, vm, g, batch, seq):
    w = qi.shape[-1]
    nstep = seq // HG_STEP
    sub = HG_STEP // HG_BLOCK
    tok = lambda width: pl.BlockSpec((None, HG_STEP, width), lambda b, n: (b, n, 0))
    meta = pl.BlockSpec((N_META, w), lambda b, n: (0, 0))
    r3 = lambda x: x.reshape(batch, seq, x.shape[-1])
    return pl.pallas_call(
        _hgrn_kernel,
        grid=(batch, nstep),
        in_specs=[tok(a.shape[-1]), tok(w), tok(w),
                  pl.BlockSpec((None, sub, 1, w), lambda b, n: (b, n, 0, 0)),
                  tok(w), tok(w), meta, meta,
                  pl.BlockSpec((1, HG_D), lambda b, n: (0, 0))],
        out_specs=tok(w),
        out_shape=jax.ShapeDtypeStruct((batch, seq, w), BF16),
        scratch_shapes=[pltpu.VMEM((HG_HEADS, HG_D, HG_D), F32)],
        compiler_params=pltpu.CompilerParams(
            dimension_semantics=("parallel", "arbitrary"),
            vmem_limit_bytes=V7X_VMEM_LIMIT_BYTES),
        name="hgrn2",
    )(r3(a), r3(qi), r3(kl), dec.reshape(batch, seq // HG_BLOCK, 1, w), r3(vh), r3(og), klm,
      vm, g)


def _attn_kernel(q_ref, kn_ref, kp_ref, vt_ref, knm_ref, kpm_ref, vtm_ref, o_ref,
                 m_ref, l_ref, acc_ref):
    i = pl.program_id(1)
    tq = q_ref.shape[0]
    hs = [slice(h * V7X_LANES, (h + 1) * V7X_LANES) for h in range(MLA_HEADS)]

    def q_of(h):
        return q_ref[:, 2 * h * V7X_LANES:2 * (h + 1) * V7X_LANES]

    def update(carry, s_list, vt_list):
        m, l, acc = carry
        m_new = m
        for s in s_list:
            m_new = jnp.maximum(m_new, jnp.max(s, axis=0, keepdims=True))
        alpha = jnp.exp2(m - m_new)
        l = alpha * l
        acc = alpha * acc
        for s, vt in zip(s_list, vt_list):
            p = jnp.exp2(s - m_new)
            l = l + jnp.sum(p, axis=0, keepdims=True)
            acc = acc + _dot(vt, p.astype(BF16))
        return m_new, l, acc

    def skewed(score_fn, consume_fn):
        scores = []
        for h in range(MLA_HEADS + ATTN_SKEW):
            if h < MLA_HEADS:
                scores.append(score_fn(h))
            g = h - ATTN_SKEW
            if g >= 0:
                consume_fn(g, scores[g])

    for h in range(MLA_HEADS):
        m_ref[h] = jnp.full((1, tq), -jnp.inf, F32)
        l_ref[h] = jnp.zeros((1, tq), F32)
        acc_ref[h] = jnp.zeros((V_HEAD, tq), F32)

    def kv_step(blocks):
        starts = [pl.multiple_of(b * ATTN_K, ATTN_K) for b in blocks]
        kps = [kp_ref[pl.ds(ks, ATTN_K), :] for ks in starts]

        def score(h):
            return [_dot_nt(jnp.concatenate([kn_ref[pl.ds(ks, ATTN_K), hs[h]], kp], axis=1),
                            q_of(h)) for ks, kp in zip(starts, kps)]

        def consume(g, s_list):
            m, l, acc = update((m_ref[g], l_ref[g], acc_ref[g]), s_list,
                               [vt_ref[b, hs[g], :] for b in blocks])
            m_ref[g] = m
            l_ref[g] = l
            acc_ref[g] = acc

        skewed(score, consume)

    def pair_step(j2, _):
        kv_step([2 * j2, 2 * j2 + 1])
        return 0

    lax.fori_loop(0, i // 2, pair_step, 0)

    @pl.when(i % 2 == 1)
    def _odd_block():
        kv_step([i - 1])

    ks = pl.multiple_of(i * ATTN_K, ATTN_K)
    kp = kp_ref[pl.ds(ks, ATTN_K), :]
    kpm = kpm_ref[...]
    keep = (lax.broadcasted_iota(jnp.int32, (ATTN_K, tq), 0)
            <= lax.broadcasted_iota(jnp.int32, (ATTN_K, tq), 1))

    def score_diag(h):
        kc = jnp.concatenate([kn_ref[pl.ds(ks, ATTN_K), hs[h]], kp], axis=1)
        km = jnp.concatenate([knm_ref[:, hs[h]], kpm], axis=1)
        return jnp.where(keep, _dot_nt(kc, q_of(h)), -jnp.inf), _dot_nt(km, q_of(h))

    def finish(g, s):
        _, l, acc = update((m_ref[g], l_ref[g], acc_ref[g]), list(s),
                           [vt_ref[i, hs[g], :], vtm_ref[hs[g], :]])
        o_ref[:, hs[g]] = (acc / l).T.astype(BF16)

    skewed(score_diag, finish)


def _attention(q, kn, kp, vt, knm, kpm, vtm, batch, seq):
    assert ATTN_Q == ATTN_K == INPROJ_ROWS
    w = kn.shape[-1]
    nq = seq // ATTN_Q
    r3 = lambda a: a.reshape(batch, seq, a.shape[-1])
    qblk = pl.BlockSpec((None, ATTN_Q, 2 * w), lambda b, i: (b, i, 0))
    oblk = pl.BlockSpec((None, ATTN_Q, w), lambda b, i: (b, i, 0))
    kfull = pl.BlockSpec((None, seq, w), lambda b, i: (b, 0, 0))
    kpfull = pl.BlockSpec((None, seq, V7X_LANES), lambda b, i: (b, 0, 0))
    vtfull = pl.BlockSpec((None, nq, w, ATTN_K), lambda b, i: (b, 0, 0, 0))
    mh = pl.BlockSpec((N_META, w), lambda b, i: (0, 0))
    mp = pl.BlockSpec((N_META, V7X_LANES), lambda b, i: (0, 0))
    mvt = pl.BlockSpec((w, N_META), lambda b, i: (0, 0))
    return pl.pallas_call(
        _attn_kernel,
        grid=(batch, nq),
        in_specs=[qblk, kfull, kpfull, vtfull, mh, mp, mvt],
        out_specs=oblk,
        out_shape=jax.ShapeDtypeStruct((batch, seq, w), BF16),
        scratch_shapes=[pltpu.VMEM((MLA_HEADS, 1, ATTN_Q), F32),
                        pltpu.VMEM((MLA_HEADS, 1, ATTN_Q), F32),
                        pltpu.VMEM((MLA_HEADS, V_HEAD, ATTN_Q), F32)],
        compiler_params=pltpu.CompilerParams(
            dimension_semantics=("parallel", "arbitrary"),
            vmem_limit_bytes=V7X_VMEM_LIMIT_BYTES),
        name="mla_attn",
    )(r3(q), r3(kn), r3(kp), vt.reshape(batch, nq, w, ATTN_K), knm, kpm,
      vtm.reshape(w, N_META))


def _out_kernel(x_ref, oh_ref, oa_ref, gates_ref, who_ref, wmo_ref, wout_ref, gmix_ref,
                gfpre_ref, wfin_ref, wfout_ref, gfpost_ref, y_ref):
    d = x_ref.shape[-1]
    hidden = wfout_ref.shape[0]
    ya = _dot(oh_ref[...], who_ref[...])
    yb = _dot(oa_ref[...], wmo_ref[...])
    gates = gates_ref[...].astype(F32)
    merged = (gates[:, :d] * ya + gates[:, d:] * yb).astype(BF16)
    h1 = x_ref[...] + _rms(_dot(merged, wout_ref[...]), gmix_ref[...])
    u = _rms(h1, gfpre_ref[...]).astype(BF16)
    gu = _dot(u, wfin_ref[...])
    gt = gu[:, :hidden]
    act = (gt * _sigmoid(gt) * gu[:, hidden:]).astype(BF16)
    y_ref[...] = h1 + _rms(_dot(act, wfout_ref[...]), gfpost_ref[...])


def _out_block(x2d, oh, oa, gates, params):
    t, d = x2d.shape
    n = t // OUT_ROWS
    row = lambda w: pl.BlockSpec((OUT_ROWS, w), lambda i: (i, 0))
    return pl.pallas_call(
        _out_kernel,
        grid=(n,),
        in_specs=[row(d), row(d), row(d), row(2 * d)] + [_resident(p.shape) for p in params],
        out_specs=row(d),
        out_shape=jax.ShapeDtypeStruct((t, d), F32),
        compiler_params=pltpu.CompilerParams(
            dimension_semantics=("parallel",), vmem_limit_bytes=V7X_VMEM_LIMIT_BYTES),
        name="merge_ffn",
    )(x2d, oh, oa, gates, *params)


def _rope_tables(length):
    pos = jnp.arange(length, dtype=F32)
    inv_freq = 1.0 / (ROPE_THETA ** (jnp.arange(0, QK_ROPE, 2, dtype=F32) / QK_ROPE))
    ang = pos[:, None] * inv_freq[None, :]
    cos, sin = jnp.cos(ang), jnp.sin(ang)
    zero = jnp.zeros((length, V7X_LANES - QK_ROPE), F32)
    return (jnp.concatenate([cos, cos, zero], axis=1),
            jnp.concatenate([-sin, sin, zero], axis=1))


def _swap_halves(w):
    half = w.shape[-1] // 2
    return jnp.concatenate([w[..., half:], w[..., :half]], axis=-1)


def kernel(x, meta_tokens, w_in, b_gate, lb_logits, hg_norm_g, w_hg_o, q_a_norm_g, w_q_b,
           kv_a_norm_g, w_kv_b, w_mla_o, w_out, mix_pre_g, mix_post_g, ffn_pre_g, ffn_post_g,
           w_ffn_in, w_ffn_out):
    batch, seq, d = x.shape
    assert w_in.shape[0] == 1, "single-layer block"
    assert seq % INPROJ_ROWS == 0 and seq % ATTN_Q == 0 and seq % HG_STEP == 0
    assert INPROJ_ROWS % HG_BLOCK == 0 and HG_STEP % HG_BLOCK == 0
    assert (batch * seq) % OUT_ROWS == 0
    hgw = HG_HEADS * HG_D
    row = lambda a: a.reshape(1, -1).astype(F32)

    wi = w_in[0]
    o = 0
    parts = []
    for sz in (hgw, hgw, hgw, hgw, Q_LORA + KV_LORA, QK_ROPE, 2 * d):
        parts.append(wi[:, o:o + sz])
        o += sz
    whq, whf, whi, whg, wc, wkpe, wgate = parts
    zpad = jnp.zeros((d, V7X_LANES - QK_ROPE), wi.dtype)
    wpe = jnp.concatenate([wkpe, zpad, _swap_halves(wkpe), zpad], axis=1)
    wq = w_q_b[0].reshape(Q_LORA, MLA_HEADS, QK_NOPE + QK_ROPE)
    wqn = wq[:, :, :QK_NOPE].reshape(Q_LORA, MLA_HEADS * QK_NOPE)
    wq_pe = wq[:, :, QK_NOPE:]
    zq = jnp.zeros((Q_LORA, MLA_HEADS, V7X_LANES - QK_ROPE), wq.dtype)
    wqp = jnp.concatenate([wq_pe, zq], axis=2).reshape(Q_LORA, MLA_HEADS * V7X_LANES)
    wqpr = jnp.concatenate([_swap_halves(wq_pe), zq], axis=2).reshape(
        Q_LORA, MLA_HEADS * V7X_LANES)
    wkv = w_kv_b[0].reshape(KV_LORA, MLA_HEADS, QK_NOPE + V_HEAD)
    wkn = wkv[:, :, :QK_NOPE].reshape(KV_LORA, MLA_HEADS * QK_NOPE)
    wv = wkv[:, :, QK_NOPE:].reshape(KV_LORA, MLA_HEADS * V_HEAD).T
    bf = lambda a: a.astype(BF16)
    inproj_params = [row(mix_pre_g[0]), bf(whq), bf(whf), bf(whi), bf(whg), bf(wc), bf(wpe),
                     bf(wgate), row(b_gate[0]), lb_logits.astype(F32), row(q_a_norm_g[0]),
                     row(kv_a_norm_g[0]), bf(wqn), bf(wqp), bf(wqpr), bf(wkn), bf(wv)]
    cos_t, sin_t = _rope_tables(N_META + seq)

    m_out = _inproj(meta_tokens.astype(F32), N_META, 1, inproj_params,
                    cos_t[:N_META], sin_t[:N_META], with_intra=False)
    _, kl_m, _, vh_m, _, _, _, kn_m, kp_m, va_m = m_out

    x2d = x.reshape(batch * seq, d)
    (qi, kl, dec, vh, og, gates, q, kn, kp, va, a_hg) = _inproj(
        x2d, INPROJ_ROWS, seq // INPROJ_ROWS, inproj_params, cos_t[N_META:], sin_t[N_META:],
        with_intra=True)

    o_hg = _hgrn(a_hg, qi, kl, dec, vh, og, kl_m, vh_m, row(hg_norm_g[0]), batch, seq)
    o_at = _attention(q, kn, kp, va, kn_m, kp_m, va_m, batch, seq)

    out_params = [bf(w_hg_o[0]), bf(w_mla_o[0]), bf(w_out[0]), row(mix_post_g[0]),
                  row(ffn_pre_g[0]), bf(w_ffn_in[0]), bf(w_ffn_out[0]), row(ffn_post_g[0])]
    y = _out_block(x2d, o_hg.reshape(batch * seq, hgw), o_at.reshape(batch * seq, -1),
                   gates, out_params)
    return y.reshape(batch, seq, d)
```

```python
import functools
import math

import jax
import jax.numpy as jnp
from jax import lax
from jax.experimental import pallas as pl
from jax.experimental.pallas import tpu as pltpu

N_META = 16
NORM_EPS = 1e-6
HG_HEADS = 8
HG_D = 128
MLA_HEADS = 8
QK_NOPE = 128
QK_ROPE = 64
V_HEAD = 128
Q_LORA = 256
KV_LORA = 256
ROPE_THETA = 10000.0

V7X_LANES = 128
V7X_VMEM_LIMIT_BYTES = 56 * 1024 * 1024

INPROJ_ROWS = 256
HG_BLOCK = 64
HG_STEP = 256
HG_SAFE_EXPONENT = 60.0
ATTN_Q = 256
ATTN_K = 256
ATTN_SKEW = 8
OUT_ROWS = 512
OUT_SPLIT = 2
V_AUG = V_HEAD + 16

F32 = jnp.float32
BF16 = jnp.bfloat16


def _dot(a, b):
    return jnp.dot(a, b, preferred_element_type=F32)


def _dot_nt(a, b):
    return lax.dot_general(a, b, (((1,), (1,)), ((), ())), preferred_element_type=F32)


def _dot_tn(a, b):
    return lax.dot_general(a, b, (((0,), (0,)), ((), ())), preferred_element_type=F32)


def _rms(x, g):
    ms = jnp.mean(x * x, axis=-1, keepdims=True)
    return x * lax.rsqrt(ms + NORM_EPS) * g


def _sigmoid(x):
    return 1.0 / (1.0 + jnp.exp(-x))


def _resident(shape):
    return pl.BlockSpec(shape, lambda *_: (0,) * len(shape), pipeline_mode=pl.Buffered(1))


def _cumsum_blocks(x, block):
    pos = lax.broadcasted_iota(jnp.int32, x.shape, 0) % block
    s = 1
    while s < block:
        x = x + jnp.where(pos >= s, pltpu.roll(x, s, 0), 0.0)
        s *= 2
    return x


def _inproj_kernel(x_ref, gpre_ref, whq_ref, whf_ref, whi_ref, whg_ref, wc_ref, wpe_ref,
                   wgate_ref, bgate_ref, lbl_ref, qg_ref, kvg_ref, wqn_ref, wqp_ref,
                   wqpr_ref, wkn_ref, wv_ref, cos_ref, sin_ref,
                   qi_ref, kl_ref, dec_ref, vh_ref, og_ref, gates_ref,
                   q_ref, kn_ref, kp_ref, va_ref, *intra_refs, scale, hg_block):
    u = _rms(x_ref[...], gpre_ref[...]).astype(BF16)
    rows = x_ref.shape[0]

    lbl = lbl_ref[...]
    e = jnp.exp(lbl - jnp.max(lbl, axis=0, keepdims=True))
    lb = e[0:1] / jnp.sum(e, axis=0, keepdims=True)

    d = x_ref.shape[1]
    hq = _dot(u, whq_ref[...])
    hf = _dot(u, whf_ref[...])
    vh_ref[...] = _dot(u, whi_ref[...]).astype(BF16)
    pe = _dot(u, wpe_ref[...])
    cmla = _dot(u, wc_ref[...])
    q = hq * _sigmoid(hq)
    sg = _sigmoid(hf)
    k = (1.0 - lb) * (1.0 - sg)
    c = _cumsum_blocks(jnp.log(lb + (1.0 - lb) * sg), hg_block)
    cos = cos_ref[...]
    sin = sin_ref[...]

    def gate_half(half):
        cols = slice(half * d, (half + 1) * d)
        gates_ref[:, cols] = _sigmoid(
            _dot(u, wgate_ref[:, cols]) + bgate_ref[:, cols]).astype(BF16)

    def out_gate():
        hg = _dot(u, whg_ref[...])
        og_ref[...] = (hg * _sigmoid(hg)).astype(BF16)

    def mla_q():
        cq = _rms(cmla[:, :Q_LORA], qg_ref[...]).astype(BF16)
        qn = _dot(cq, wqn_ref[...])
        qpe = _dot(cq, wqp_ref[...])
        qper = _dot(cq, wqpr_ref[...])
        for h in range(MLA_HEADS):
            sl = slice(h * V7X_LANES, (h + 1) * V7X_LANES)
            base = 2 * h * V7X_LANES
            q_ref[:, base:base + V7X_LANES] = (qn[:, sl] * scale).astype(BF16)
            q_ref[:, base + V7X_LANES:base + 2 * V7X_LANES] = (
                (qpe[:, sl] * cos + qper[:, sl] * sin) * scale).astype(BF16)

    fillers = [out_gate, functools.partial(gate_half, 0), functools.partial(gate_half, 1), mla_q]

    mid = hg_block // 2 - 1
    pair_w = 2 * HG_D
    zero = jnp.zeros((hg_block, HG_D), BF16)
    nblk = rows // hg_block
    if intra_refs:
        a_ref, ksc_ref, csc_ref = intra_refs
        tri = (lax.broadcasted_iota(jnp.int32, (hg_block, 2 * hg_block), 0)
               >= lax.broadcasted_iota(jnp.int32, (hg_block, 2 * hg_block), 1) % hg_block)
        worst = jnp.zeros((1, c.shape[1]), F32)
    for blk in range(nblk):
        r = slice(blk * hg_block, (blk + 1) * hg_block)
        cb = c[r]
        cm = cb[mid:mid + 1]
        cl = cb[hg_block - 1:hg_block]
        qt = q[r] * jnp.exp(cb - cm)
        kt = k[r] * jnp.exp(cm - cb)
        qi_ref[r, :] = (q[r] * jnp.exp(cb)).astype(BF16)
        kl_ref[r, :] = (k[r] * jnp.exp(cl - cb)).astype(BF16)
        dec_ref[blk:blk + 1, :] = jnp.exp(cl)
        if intra_refs:
            worst = jnp.maximum(worst, jnp.maximum(cb[0:1] - cm, cm - cl))
            qt = qt.astype(BF16)
            kt = kt.astype(BF16)
            for p in range(HG_HEADS // 2):
                ls = slice(p * pair_w, (p + 1) * pair_w)
                kb = jnp.concatenate(
                    [jnp.concatenate([kt[:, ls][:, :HG_D], zero], axis=1),
                     jnp.concatenate([zero, kt[:, ls][:, HG_D:]], axis=1)], axis=0)
                a = jnp.where(tri, _dot_nt(qt[:, ls], kb), 0.0)
                a_ref[r, p * 2 * hg_block:(p + 1) * 2 * hg_block] = a.astype(BF16)
        if fillers:
            fillers.pop(0)()
    while fillers:
        fillers.pop(0)()

    kp_ref[...] = (pe[:, :V7X_LANES] * cos + pe[:, V7X_LANES:] * sin).astype(BF16)
    ckv = _rms(cmla[:, Q_LORA:], kvg_ref[...]).astype(BF16)
    kn_ref[...] = _dot(ckv, wkn_ref[...]).astype(BF16)
    va_ref[...] = _dot_nt(wv_ref[...], ckv).astype(BF16)
    for h in range(MLA_HEADS):
        va_ref[h * V_AUG + V_HEAD:(h + 1) * V_AUG, :] = jnp.ones((V_AUG - V_HEAD, rows), BF16)

    if intra_refs:
        @pl.when(jnp.max(worst) > HG_SAFE_EXPONENT)
        def _exact_intra():
            width = c.shape[1]
            a_w = HG_HEADS * hg_block
            ksc_ref[...] = k
            csc_ref[...] = c
            head_of_lane = lax.broadcasted_iota(jnp.int32, (width, HG_HEADS), 0) // HG_D
            head_sum = (head_of_lane
                        == lax.broadcasted_iota(jnp.int32, (width, HG_HEADS), 1)).astype(F32)
            col_base = lax.broadcasted_iota(jnp.int32, (HG_HEADS, a_w), 0) * hg_block
            col_lane = lax.broadcasted_iota(jnp.int32, (HG_HEADS, a_w), 1)
            causal = (lax.broadcasted_iota(jnp.int32, (hg_block, a_w), 0)
                      >= lax.broadcasted_iota(jnp.int32, (hg_block, a_w), 1) % hg_block)
            for blk in range(nblk):
                r = slice(blk * hg_block, (blk + 1) * hg_block)
                qb = q[r]
                cb = c[r]

                def key_column(s, acc):
                    row = blk * hg_block + s
                    w = (qb * jnp.exp(jnp.minimum(cb - csc_ref[pl.ds(row, 1), :], 0.0))
                         * ksc_ref[pl.ds(row, 1), :])
                    cols = _dot(w, head_sum)
                    place = (col_lane == col_base + s).astype(F32)
                    return acc + _dot(cols, place)

                acc = lax.fori_loop(0, hg_block, key_column, jnp.zeros((hg_block, a_w), F32))
                a_ref[r, :] = jnp.where(causal, acc, 0.0).astype(BF16)


def _inproj(x2d, rows, pos_blocks, params, cos_t, sin_t, with_intra):
    t, d = x2d.shape
    n = t // rows
    row = lambda w: pl.BlockSpec((rows, w), lambda i: (i, 0))
    pos = pl.BlockSpec((rows, V7X_LANES), lambda i: (i % pos_blocks, 0))
    w_specs = [_resident(p.shape) for p in params]
    hg_block = min(HG_BLOCK, rows)
    nb = rows // hg_block
    vt_w = MLA_HEADS * V_AUG
    lead = lambda *shape: pl.BlockSpec((None,) + shape, lambda i: (i,) + (0,) * len(shape))
    outs = [(row(d), (t, d), BF16), (row(d), (t, d), BF16),
            (lead(nb, d), (n, nb, d), F32),
            (row(d), (t, d), BF16), (row(d), (t, d), BF16), (row(2 * d), (t, 2 * d), BF16),
            (row(2 * d), (t, 2 * d), BF16), (row(d), (t, d), BF16),
            (row(V7X_LANES), (t, V7X_LANES), BF16),
            (lead(vt_w, rows), (n, vt_w, rows), BF16)]
    scratch = []
    if with_intra:
        a_w = HG_HEADS * hg_block
        outs.append((row(a_w), (t, a_w), BF16))
        scratch = [pltpu.VMEM((rows, d), F32), pltpu.VMEM((rows, d), F32)]
    scale = (QK_NOPE + QK_ROPE) ** -0.5 * math.log2(math.e)
    return pl.pallas_call(
        functools.partial(_inproj_kernel, scale=scale, hg_block=hg_block),
        grid=(n,),
        in_specs=[row(d)] + w_specs + [pos, pos],
        out_specs=[o[0] for o in outs],
        out_shape=[jax.ShapeDtypeStruct(o[1], o[2]) for o in outs],
        scratch_shapes=scratch,
        compiler_params=pltpu.CompilerParams(
            dimension_semantics=("parallel",), vmem_limit_bytes=V7X_VMEM_LIMIT_BYTES),
        name="inproj",
    )(x2d, *params, cos_t, sin_t)


def _hgrn_kernel(a_ref, qi_ref, kl_ref, dec_ref, v_ref, og_ref, klm_ref, vm_ref, g_ref,
                 o_ref, st_ref):
    n = pl.program_id(1)
    c_blk = qi_ref.shape[0]
    hs = [slice(h * HG_D, (h + 1) * HG_D) for h in range(HG_HEADS)]

    @pl.when(n == 0)
    def _init():
        vm = vm_ref[...]
        klm = klm_ref[...]
        for h, sl in enumerate(hs):
            st_ref[h] = _dot_tn(vm[:, sl], klm[:, sl])

    g = g_ref[...]
    zero = jnp.zeros((HG_BLOCK, HG_D), BF16)
    for blk in range(c_blk // HG_BLOCK):
        r = slice(blk * HG_BLOCK, (blk + 1) * HG_BLOCK)
        a = a_ref[r, :]
        qi = qi_ref[r, :]
        kl = kl_ref[r, :]
        v = v_ref[r, :]
        dec = dec_ref[blk]
        outs = []
        for p in range(HG_HEADS // 2):
            ha, hb = hs[2 * p], hs[2 * p + 1]
            vb = jnp.concatenate([jnp.concatenate([v[:, ha], zero], axis=1),
                                  jnp.concatenate([zero, v[:, hb]], axis=1)], axis=0)
            o_pair = _dot(a[:, p * 2 * HG_BLOCK:(p + 1) * 2 * HG_BLOCK], vb)
            for e, sl in enumerate((ha, hb)):
                h = 2 * p + e
                st = st_ref[h]
                outs.append(o_pair[:, e * HG_D:(e + 1) * HG_D]
                            + _dot_nt(qi[:, sl], st.astype(BF16)))
                st_ref[h] = st * dec[:, sl] + _dot_tn(v[:, sl], kl[:, sl])
        for o, sl in zip(outs, hs):
            o_ref[r, sl] = (_rms(o, g) * og_ref[r, sl].astype(F32)).astype(BF16)


def _hgrn(a, qi, kl, dec, vh, og, klm, vm, g, batch, seq):
    w = qi.shape[-1]
    nstep = seq // HG_STEP
    sub = HG_STEP // HG_BLOCK
    tok = lambda width: pl.BlockSpec((None, HG_STEP, width), lambda b, n: (b, n, 0))
    meta = pl.BlockSpec((N_META, w), lambda b, n: (0, 0))
    r3 = lambda x: x.reshape(batch, seq, x.shape[-1])
    return pl.pallas_call(
        _hgrn_kernel,
        grid=(batch, nstep),
        in_specs=[tok(a.shape[-1]), tok(w), tok(w),
                  pl.BlockSpec((None, sub, 1, w), lambda b, n: (b, n, 0, 0)),
                  tok(w), tok(w), meta, meta,
                  pl.BlockSpec((1, HG_D), lambda b, n: (0, 0))],
        out_specs=tok(w),
        out_shape=jax.ShapeDtypeStruct((batch, seq, w), BF16),
        scratch_shapes=[pltpu.VMEM((HG_HEADS, HG_D, HG_D), F32)],
        compiler_params=pltpu.CompilerParams(
            dimension_semantics=("parallel", "arbitrary"),
            vmem_limit_bytes=V7X_VMEM_LIMIT_BYTES),
        name="hgrn2",
    )(r3(a), r3(qi), r3(kl), dec.reshape(batch, seq // HG_BLOCK, 1, w), r3(vh), r3(og), klm,
      vm, g)


def _attn_kernel(q_ref, kn_ref, kp_ref, vt_ref, knm_ref, kpm_ref, vtm_ref, o_ref,
                 m_ref, acc_ref):
    i = pl.program_id(1)
    tq = q_ref.shape[0]
    hs = [slice(h * V7X_LANES, (h + 1) * V7X_LANES) for h in range(MLA_HEADS)]
    vs = [slice(h * V_AUG, (h + 1) * V_AUG) for h in range(MLA_HEADS)]

    def q_of(h):
        return q_ref[:, 2 * h * V7X_LANES:2 * (h + 1) * V7X_LANES]

    def update(carry, s_list, vt_list):
        m, acc = carry
        m_new = m
        for s in s_list:
            m_new = jnp.maximum(m_new, jnp.max(s, axis=0, keepdims=True))
        acc = jnp.exp2(m - m_new) * acc
        for s, vt in zip(s_list, vt_list):
            acc = acc + _dot(vt, jnp.exp2(s - m_new).astype(BF16))
        return m_new, acc

    def skewed(score_fn, consume_fn):
        scores = []
        for h in range(MLA_HEADS + ATTN_SKEW):
            if h < MLA_HEADS:
                scores.append(score_fn(h))
            g = h - ATTN_SKEW
            if g >= 0:
                consume_fn(g, scores[g])

    for h in range(MLA_HEADS):
        m_ref[h] = jnp.full((1, tq), -jnp.inf, F32)
        acc_ref[h] = jnp.zeros((V_AUG, tq), F32)

    def kv_step(blocks):
        starts = [pl.multiple_of(b * ATTN_K, ATTN_K) for b in blocks]
        kps = [kp_ref[pl.ds(ks, ATTN_K), :] for ks in starts]

        def score(h):
            return [_dot_nt(jnp.concatenate([kn_ref[pl.ds(ks, ATTN_K), hs[h]], kp], axis=1),
                            q_of(h)) for ks, kp in zip(starts, kps)]

        def consume(g, s_list):
            m, acc = update((m_ref[g], acc_ref[g]), s_list, [vt_ref[b, vs[g], :] for b in blocks])
            m_ref[g] = m
            acc_ref[g] = acc

        skewed(score, consume)

    def pair_step(j2, _):
        kv_step([2 * j2, 2 * j2 + 1])
        return 0

    lax.fori_loop(0, i // 2, pair_step, 0)

    @pl.when(i % 2 == 1)
    def _odd_block():
        kv_step([i - 1])

    ks = pl.multiple_of(i * ATTN_K, ATTN_K)
    kp = kp_ref[pl.ds(ks, ATTN_K), :]
    kpm = kpm_ref[...]
    keep = (lax.broadcasted_iota(jnp.int32, (ATTN_K, tq), 0)
            <= lax.broadcasted_iota(jnp.int32, (ATTN_K, tq), 1))

    def score_diag(h):
        kc = jnp.concatenate([kn_ref[pl.ds(ks, ATTN_K), hs[h]], kp], axis=1)
        km = jnp.concatenate([knm_ref[:, hs[h]], kpm], axis=1)
        return jnp.where(keep, _dot_nt(kc, q_of(h)), -jnp.inf), _dot_nt(km, q_of(h))

    def finish(g, s):
        _, acc = update((m_ref[g], acc_ref[g]), list(s), [vt_ref[i, vs[g], :], vtm_ref[vs[g], :]])
        o_ref[:, hs[g]] = (acc[:V_HEAD] / acc[V_HEAD:V_HEAD + 1]).T.astype(BF16)

    skewed(score_diag, finish)


def _attention(q, kn, kp, vt, knm, kpm, vtm, batch, seq):
    assert ATTN_Q == ATTN_K == INPROJ_ROWS
    w = kn.shape[-1]
    nq = seq // ATTN_Q
    r3 = lambda a: a.reshape(batch, seq, a.shape[-1])
    qblk = pl.BlockSpec((None, ATTN_Q, 2 * w), lambda b, i: (b, i, 0))
    oblk = pl.BlockSpec((None, ATTN_Q, w), lambda b, i: (b, i, 0))
    kfull = pl.BlockSpec((None, seq, w), lambda b, i: (b, 0, 0))
    kpfull = pl.BlockSpec((None, seq, V7X_LANES), lambda b, i: (b, 0, 0))
    vt_w = MLA_HEADS * V_AUG
    vtfull = pl.BlockSpec((None, nq, vt_w, ATTN_K), lambda b, i: (b, 0, 0, 0))
    mh = pl.BlockSpec((N_META, w), lambda b, i: (0, 0))
    mp = pl.BlockSpec((N_META, V7X_LANES), lambda b, i: (0, 0))
    mvt = pl.BlockSpec((vt_w, N_META), lambda b, i: (0, 0))
    return pl.pallas_call(
        _attn_kernel,
        grid=(batch, nq),
        in_specs=[qblk, kfull, kpfull, vtfull, mh, mp, mvt],
        out_specs=oblk,
        out_shape=jax.ShapeDtypeStruct((batch, seq, w), BF16),
        scratch_shapes=[pltpu.VMEM((MLA_HEADS, 1, ATTN_Q), F32),
                        pltpu.VMEM((MLA_HEADS, V_AUG, ATTN_Q), F32)],
        compiler_params=pltpu.CompilerParams(
            dimension_semantics=("parallel", "arbitrary"),
            vmem_limit_bytes=V7X_VMEM_LIMIT_BYTES),
        name="mla_attn",
    )(r3(q), r3(kn), r3(kp), vt.reshape(batch, nq, vt_w, ATTN_K), knm, kpm,
      vtm.reshape(vt_w, N_META))


def _out_kernel(x_ref, oh_ref, oa_ref, gates_ref, who_ref, wmo_ref, wout_ref, gmix_ref,
                gfpre_ref, wfin_ref, wfout_ref, gfpost_ref, y_ref):
    rows, d = x_ref.shape
    hidden = wfout_ref.shape[0]
    halves = [slice(i * rows // OUT_SPLIT, (i + 1) * rows // OUT_SPLIT) for i in range(OUT_SPLIT)]

    def merge(r):
        ya = _dot(oh_ref[r, :], who_ref[...])
        yb = _dot(oa_ref[r, :], wmo_ref[...])
        gates = gates_ref[r, :].astype(F32)
        return (gates[:, :d] * ya + gates[:, d:] * yb).astype(BF16)

    def mix_residual(r, merged):
        h1 = x_ref[r, :] + _rms(_dot(merged, wout_ref[...]), gmix_ref[...])
        return h1, _rms(h1, gfpre_ref[...]).astype(BF16)

    def ffn_act(u):
        gu = _dot(u, wfin_ref[...])
        gt = gu[:, :hidden]
        return (gt * _sigmoid(gt) * gu[:, hidden:]).astype(BF16)

    merged = [merge(r) for r in halves]
    h1_u = [mix_residual(r, m) for r, m in zip(halves, merged)]
    acts = [ffn_act(u) for _, u in h1_u]
    for r, (h1, _), act in zip(halves, h1_u, acts):
        y_ref[r, :] = h1 + _rms(_dot(act, wfout_ref[...]), gfpost_ref[...])


def _out_block(x2d, oh, oa, gates, params):
    t, d = x2d.shape
    n = t // OUT_ROWS
    row = lambda w: pl.BlockSpec((OUT_ROWS, w), lambda i: (i, 0))
    return pl.pallas_call(
        _out_kernel,
        grid=(n,),
        in_specs=[row(d), row(d), row(d), row(2 * d)] + [_resident(p.shape) for p in params],
        out_specs=row(d),
        out_shape=jax.ShapeDtypeStruct((t, d), F32),
        compiler_params=pltpu.CompilerParams(
            dimension_semantics=("parallel",), vmem_limit_bytes=V7X_VMEM_LIMIT_BYTES),
        name="merge_ffn",
    )(x2d, oh, oa, gates, *params)


def _rope_tables(length):
    pos = jnp.arange(length, dtype=F32)
    inv_freq = 1.0 / (ROPE_THETA ** (jnp.arange(0, QK_ROPE, 2, dtype=F32) / QK_ROPE))
    ang = pos[:, None] * inv_freq[None, :]
    cos, sin = jnp.cos(ang), jnp.sin(ang)
    zero = jnp.zeros((length, V7X_LANES - QK_ROPE), F32)
    return (jnp.concatenate([cos, cos, zero], axis=1),
            jnp.concatenate([-sin, sin, zero], axis=1))


def _swap_halves(w):
    half = w.shape[-1] // 2
    return jnp.concatenate([w[..., half:], w[..., :half]], axis=-1)


def kernel(x, meta_tokens, w_in, b_gate, lb_logits, hg_norm_g, w_hg_o, q_a_norm_g, w_q_b,
           kv_a_norm_g, w_kv_b, w_mla_o, w_out, mix_pre_g, mix_post_g, ffn_pre_g, ffn_post_g,
           w_ffn_in, w_ffn_out):
    batch, seq, d = x.shape
    assert w_in.shape[0] == 1, "single-layer block"
    assert seq % INPROJ_ROWS == 0 and seq % ATTN_Q == 0 and seq % HG_STEP == 0
    assert INPROJ_ROWS % HG_BLOCK == 0 and HG_STEP % HG_BLOCK == 0
    assert (batch * seq) % OUT_ROWS == 0
    hgw = HG_HEADS * HG_D
    row = lambda a: a.reshape(1, -1).astype(F32)

    wi = w_in[0]
    o = 0
    parts = []
    for sz in (hgw, hgw, hgw, hgw, Q_LORA + KV_LORA, QK_ROPE, 2 * d):
        parts.append(wi[:, o:o + sz])
        o += sz
    whq, whf, whi, whg, wc, wkpe, wgate = parts
    zpad = jnp.zeros((d, V7X_LANES - QK_ROPE), wi.dtype)
    wpe = jnp.concatenate([wkpe, zpad, _swap_halves(wkpe), zpad], axis=1)
    wq = w_q_b[0].reshape(Q_LORA, MLA_HEADS, QK_NOPE + QK_ROPE)
    wqn = wq[:, :, :QK_NOPE].reshape(Q_LORA, MLA_HEADS * QK_NOPE)
    wq_pe = wq[:, :, QK_NOPE:]
    zq = jnp.zeros((Q_LORA, MLA_HEADS, V7X_LANES - QK_ROPE), wq.dtype)
    wqp = jnp.concatenate([wq_pe, zq], axis=2).reshape(Q_LORA, MLA_HEADS * V7X_LANES)
    wqpr = jnp.concatenate([_swap_halves(wq_pe), zq], axis=2).reshape(
        Q_LORA, MLA_HEADS * V7X_LANES)
    wkv = w_kv_b[0].reshape(KV_LORA, MLA_HEADS, QK_NOPE + V_HEAD)
    wkn = wkv[:, :, :QK_NOPE].reshape(KV_LORA, MLA_HEADS * QK_NOPE)
    wv = jnp.pad(wkv[:, :, QK_NOPE:], ((0, 0), (0, 0), (0, V_AUG - V_HEAD))).reshape(
        KV_LORA, MLA_HEADS * V_AUG).T
    bf = lambda a: a.astype(BF16)
    inproj_params = [row(mix_pre_g[0]), bf(whq), bf(whf), bf(whi), bf(whg), bf(wc), bf(wpe),
                     bf(wgate), row(b_gate[0]), lb_logits.astype(F32), row(q_a_norm_g[0]),
                     row(kv_a_norm_g[0]), bf(wqn), bf(wqp), bf(wqpr), bf(wkn), bf(wv)]
    cos_t, sin_t = _rope_tables(N_META + seq)

    m_out = _inproj(meta_tokens.astype(F32), N_META, 1, inproj_params,
                    cos_t[:N_META], sin_t[:N_META], with_intra=False)
    _, kl_m, _, vh_m, _, _, _, kn_m, kp_m, va_m = m_out

    x2d = x.reshape(batch * seq, d)
    (qi, kl, dec, vh, og, gates, q, kn, kp, va, a_hg) = _inproj(
        x2d, INPROJ_ROWS, seq // INPROJ_ROWS, inproj_params, cos_t[N_META:], sin_t[N_META:],
        with_intra=True)

    o_hg = _hgrn(a_hg, qi, kl, dec, vh, og, kl_m, vh_m, row(hg_norm_g[0]), batch, seq)
    o_at = _attention(q, kn, kp, va, kn_m, kp_m, va_m, batch, seq)

    out_params = [bf(w_hg_o[0]), bf(w_mla_o[0]), bf(w_out[0]), row(mix_post_g[0]),
                  row(ffn_pre_g[0]), bf(w_ffn_in[0]), bf(w_ffn_out[0]), row(ffn_post_g[0])]
    y = _out_block(x2d, o_hg.reshape(batch * seq, hgw), o_at.reshape(batch * seq, -1),
                   gates, out_params)
    return y.reshape(batch, seq, d)
```

```python
import functools
import math

import jax
import jax.numpy as jnp
from jax import lax
from jax.experimental import pallas as pl
from jax.experimental.pallas import tpu as pltpu

N_META = 16
NORM_EPS = 1e-6
HG_HEADS = 8
HG_D = 128
MLA_HEADS = 8
QK_NOPE = 128
QK_ROPE = 64
V_HEAD = 128
Q_LORA = 256
KV_LORA = 256
ROPE_THETA = 10000.0

V7X_LANES = 128
V7X_VMEM_LIMIT_BYTES = 56 * 1024 * 1024

INPROJ_ROWS = 512
INPROJ_SPLIT = 2
HG_BLOCK = 64
HG_STEP = 512
HG_SAFE_EXPONENT = 60.0
ATTN_Q = 256
ATTN_K = 256
ATTN_SKEW = 8
OUT_ROWS = 512
OUT_SPLIT = 2

F32 = jnp.float32
BF16 = jnp.bfloat16


def _dot(a, b):
    return jnp.dot(a, b, preferred_element_type=F32)


def _dot_nt(a, b):
    return lax.dot_general(a, b, (((1,), (1,)), ((), ())), preferred_element_type=F32)


def _dot_tn(a, b):
    return lax.dot_general(a, b, (((0,), (0,)), ((), ())), preferred_element_type=F32)


def _rms(x, g):
    ms = jnp.mean(x * x, axis=-1, keepdims=True)
    return x * lax.rsqrt(ms + NORM_EPS) * g


def _sigmoid(x):
    return 1.0 / (1.0 + jnp.exp(-x))


def _resident(shape):
    return pl.BlockSpec(shape, lambda *_: (0,) * len(shape), pipeline_mode=pl.Buffered(1))


def _cumsum_blocks(x, block):
    pos = lax.broadcasted_iota(jnp.int32, x.shape, 0) % block
    s = 1
    while s < block:
        x = x + jnp.where(pos >= s, pltpu.roll(x, s, 0), 0.0)
        s *= 2
    return x


def _inproj_kernel(x_ref, gpre_ref, whq_ref, whf_ref, whi_ref, whg_ref, wc_ref, wpe_ref,
                   wgate_ref, bgate_ref, lbl_ref, qg_ref, kvg_ref, wqn_ref, wqp_ref,
                   wqpr_ref, wkn_ref, wv_ref, cos_ref, sin_ref,
                   qi_ref, kl_ref, dec_ref, vh_ref, og_ref, gates_ref,
                   q_ref, kn_ref, kp_ref, va_ref, *intra_refs, scale, hg_block, split):
    rows, d = x_ref.shape
    part_rows = rows // split
    parts = [slice(i * part_rows, (i + 1) * part_rows) for i in range(split)]
    nblk = part_rows // hg_block
    mid = hg_block // 2 - 1
    pair_w = 2 * HG_D

    lbl = lbl_ref[...]
    e = jnp.exp(lbl - jnp.max(lbl, axis=0, keepdims=True))
    lb = e[0:1] / jnp.sum(e, axis=0, keepdims=True)

    stage1 = []
    for r in parts:
        u = _rms(x_ref[r, :], gpre_ref[...]).astype(BF16)
        hq = _dot(u, whq_ref[...])
        hf = _dot(u, whf_ref[...])
        vh_ref[r, :] = _dot(u, whi_ref[...]).astype(BF16)
        pe = _dot(u, wpe_ref[...])
        cmla = _dot(u, wc_ref[...])
        stage1.append((u, hq, hf, pe, cmla))

    stage2 = []
    for _, hq, hf, _, _ in stage1:
        q = hq * _sigmoid(hq)
        sg = _sigmoid(hf)
        k = (1.0 - lb) * (1.0 - sg)
        c = _cumsum_blocks(jnp.log(lb + (1.0 - lb) * sg), hg_block)
        stage2.append((q, k, c))

    zero = jnp.zeros((hg_block, HG_D), BF16)
    if intra_refs:
        a_ref, ksc_ref, csc_ref = intra_refs
        tri = (lax.broadcasted_iota(jnp.int32, (hg_block, 2 * hg_block), 0)
               >= lax.broadcasted_iota(jnp.int32, (hg_block, 2 * hg_block), 1) % hg_block)
        worst = jnp.zeros((1, d), F32)
    for idx, r in enumerate(parts):
        u, _, _, pe, cmla = stage1[idx]
        q, k, c = stage2[idx]
        cos = cos_ref[r, :]
        sin = sin_ref[r, :]

        def gate_half(half):
            cols = slice(half * d, (half + 1) * d)
            gates_ref[r, cols] = _sigmoid(
                _dot(u, wgate_ref[:, cols]) + bgate_ref[:, cols]).astype(BF16)

        def out_gate():
            hg = _dot(u, whg_ref[...])
            og_ref[r, :] = (hg * _sigmoid(hg)).astype(BF16)

        def mla_q():
            cq = _rms(cmla[:, :Q_LORA], qg_ref[...]).astype(BF16)
            qn = _dot(cq, wqn_ref[...])
            qpe = _dot(cq, wqp_ref[...])
            qper = _dot(cq, wqpr_ref[...])
            for h in range(MLA_HEADS):
                sl = slice(h * V7X_LANES, (h + 1) * V7X_LANES)
                base = 2 * h * V7X_LANES
                q_ref[r, base:base + V7X_LANES] = (qn[:, sl] * scale).astype(BF16)
                q_ref[r, base + V7X_LANES:base + 2 * V7X_LANES] = (
                    (qpe[:, sl] * cos + qper[:, sl] * sin) * scale).astype(BF16)

        def mla_kv():
            kp_ref[r, :] = (pe[:, :V7X_LANES] * cos + pe[:, V7X_LANES:] * sin).astype(BF16)
            ckv = _rms(cmla[:, Q_LORA:], kvg_ref[...]).astype(BF16)
            kn_ref[r, :] = _dot(ckv, wkn_ref[...]).astype(BF16)
            va_ref[idx] = _dot_nt(wv_ref[...], ckv).astype(BF16)

        fillers = [out_gate, functools.partial(gate_half, 0), functools.partial(gate_half, 1),
                   mla_q, mla_kv]
        for blk in range(nblk):
            rb = slice(blk * hg_block, (blk + 1) * hg_block)
            ro = slice(r.start + rb.start, r.start + rb.stop)
            cb = c[rb]
            cm = cb[mid:mid + 1]
            cl = cb[hg_block - 1:hg_block]
            qi_ref[ro, :] = (q[rb] * jnp.exp(cb)).astype(BF16)
            kl_ref[ro, :] = (k[rb] * jnp.exp(cl - cb)).astype(BF16)
            dec_ref[idx * nblk + blk:idx * nblk + blk + 1, :] = jnp.exp(cl)
            if intra_refs:
                worst = jnp.maximum(worst, jnp.maximum(cb[0:1] - cm, cm - cl))
                qt = (q[rb] * jnp.exp(cb - cm)).astype(BF16)
                kt = (k[rb] * jnp.exp(cm - cb)).astype(BF16)
                for p in range(HG_HEADS // 2):
                    ls = slice(p * pair_w, (p + 1) * pair_w)
                    kb = jnp.concatenate(
                        [jnp.concatenate([kt[:, ls][:, :HG_D], zero], axis=1),
                         jnp.concatenate([zero, kt[:, ls][:, HG_D:]], axis=1)], axis=0)
                    a = jnp.where(tri, _dot_nt(qt[:, ls], kb), 0.0)
                    a_ref[ro, p * 2 * hg_block:(p + 1) * 2 * hg_block] = a.astype(BF16)
            if fillers:
                fillers.pop(0)()
        while fillers:
            fillers.pop(0)()

    if intra_refs:
        @pl.when(jnp.max(worst) > HG_SAFE_EXPONENT)
        def _exact_intra():
            a_w = HG_HEADS * hg_block
            head_of_lane = lax.broadcasted_iota(jnp.int32, (d, HG_HEADS), 0) // HG_D
            head_sum = (head_of_lane
                        == lax.broadcasted_iota(jnp.int32, (d, HG_HEADS), 1)).astype(F32)
            col_base = lax.broadcasted_iota(jnp.int32, (HG_HEADS, a_w), 0) * hg_block
            col_lane = lax.broadcasted_iota(jnp.int32, (HG_HEADS, a_w), 1)
            causal = (lax.broadcasted_iota(jnp.int32, (hg_block, a_w), 0)
                      >= lax.broadcasted_iota(jnp.int32, (hg_block, a_w), 1) % hg_block)
            for r, (q, k, c) in zip(parts, stage2):
                ksc_ref[...] = k
                csc_ref[...] = c
                for blk in range(nblk):
                    rb = slice(blk * hg_block, (blk + 1) * hg_block)
                    first = r.start + rb.start
                    qb = q[rb]
                    cb = c[rb]

                    def key_column(s, acc):
                        row = rb.start + s
                        w = (qb * jnp.exp(jnp.minimum(cb - csc_ref[pl.ds(row, 1), :], 0.0))
                             * ksc_ref[pl.ds(row, 1), :])
                        cols = _dot(w, head_sum)
                        place = (col_lane == col_base + s).astype(F32)
                        return acc + _dot(cols, place)

                    acc = lax.fori_loop(0, hg_block, key_column,
                                        jnp.zeros((hg_block, a_w), F32))
                    a_ref[first:first + hg_block, :] = jnp.where(causal, acc, 0.0).astype(BF16)


def _inproj(x2d, rows, pos_blocks, params, cos_t, sin_t, with_intra):
    t, d = x2d.shape
    n = t // rows
    row = lambda w: pl.BlockSpec((rows, w), lambda i: (i, 0))
    pos = pl.BlockSpec((rows, V7X_LANES), lambda i: (i % pos_blocks, 0))
    w_specs = [_resident(p.shape) for p in params]
    hg_block = min(HG_BLOCK, rows)
    nb = rows // hg_block
    split = INPROJ_SPLIT if rows % (INPROJ_SPLIT * hg_block) == 0 else 1
    vt_w = MLA_HEADS * V_HEAD
    lead = lambda *shape: pl.BlockSpec((None,) + shape, lambda i: (i,) + (0,) * len(shape))
    outs = [(row(d), (t, d), BF16), (row(d), (t, d), BF16),
            (lead(nb, d), (n, nb, d), F32),
            (row(d), (t, d), BF16), (row(d), (t, d), BF16), (row(2 * d), (t, 2 * d), BF16),
            (row(2 * d), (t, 2 * d), BF16), (row(d), (t, d), BF16),
            (row(V7X_LANES), (t, V7X_LANES), BF16),
            (lead(split, vt_w, rows // split), (n, split, vt_w, rows // split), BF16)]
    scratch = []
    if with_intra:
        a_w = HG_HEADS * hg_block
        outs.append((row(a_w), (t, a_w), BF16))
        scratch = [pltpu.VMEM((rows // split, d), F32), pltpu.VMEM((rows // split, d), F32)]
    scale = (QK_NOPE + QK_ROPE) ** -0.5 * math.log2(math.e)
    return pl.pallas_call(
        functools.partial(_inproj_kernel, scale=scale, hg_block=hg_block, split=split),
        grid=(n,),
        in_specs=[row(d)] + w_specs + [pos, pos],
        out_specs=[o[0] for o in outs],
        out_shape=[jax.ShapeDtypeStruct(o[1], o[2]) for o in outs],
        scratch_shapes=scratch,
        compiler_params=pltpu.CompilerParams(
            dimension_semantics=("parallel",), vmem_limit_bytes=V7X_VMEM_LIMIT_BYTES),
        name="inproj",
    )(x2d, *params, cos_t, sin_t)


def _hgrn_kernel(a_ref, qi_ref, kl_ref, dec_ref, v_ref, og_ref, klm_ref, vm_ref, g_ref,
                 o_ref, st_ref):
    n = pl.program_id(1)
    c_blk = qi_ref.shape[0]
    hs = [slice(h * HG_D, (h + 1) * HG_D) for h in range(HG_HEADS)]

    @pl.when(n == 0)
    def _init():
        vm = vm_ref[...]
        klm = klm_ref[...]
        for h, sl in enumerate(hs):
            st_ref[h] = _dot_tn(vm[:, sl], klm[:, sl])

    g = g_ref[...]
    zero = jnp.zeros((HG_BLOCK, HG_D), BF16)
    for blk in range(c_blk // HG_BLOCK):
        r = slice(blk * HG_BLOCK, (blk + 1) * HG_BLOCK)
        a = a_ref[r, :]
        qi = qi_ref[r, :]
        kl = kl_ref[r, :]
        v = v_ref[r, :]
        dec = dec_ref[blk]
        outs = []
        for p in range(HG_HEADS // 2):
            ha, hb = hs[2 * p], hs[2 * p + 1]
            vb = jnp.concatenate([jnp.concatenate([v[:, ha], zero], axis=1),
                                  jnp.concatenate([zero, v[:, hb]], axis=1)], axis=0)
            o_pair = _dot(a[:, p * 2 * HG_BLOCK:(p + 1) * 2 * HG_BLOCK], vb)
            for e, sl in enumerate((ha, hb)):
                h = 2 * p + e
                st = st_ref[h]
                outs.append(o_pair[:, e * HG_D:(e + 1) * HG_D]
                            + _dot_nt(qi[:, sl], st.astype(BF16)))
                st_ref[h] = st * dec[:, sl] + _dot_tn(v[:, sl], kl[:, sl])
        for o, sl in zip(outs, hs):
            o_ref[r, sl] = (_rms(o, g) * og_ref[r, sl].astype(F32)).astype(BF16)


def _hgrn(a, qi, kl, dec, vh, og, klm, vm, g, batch, seq):
    w = qi.shape[-1]
    nstep = seq // HG_STEP
    sub = HG_STEP // HG_BLOCK
    tok = lambda width: pl.BlockSpec((None, HG_STEP, width), lambda b, n: (b, n, 0))
    meta = pl.BlockSpec((N_META, w), lambda b, n: (0, 0))
    r3 = lambda x: x.reshape(batch, seq, x.shape[-1])
    return pl.pallas_call(
        _hgrn_kernel,
        grid=(batch, nstep),
        in_specs=[tok(a.shape[-1]), tok(w), tok(w),
                  pl.BlockSpec((None, sub, 1, w), lambda b, n: (b, n, 0, 0)),
                  tok(w), tok(w), meta, meta,
                  pl.BlockSpec((1, HG_D), lambda b, n: (0, 0))],
        out_specs=tok(w),
        out_shape=jax.ShapeDtypeStruct((batch, seq, w), BF16),
        scratch_shapes=[pltpu.VMEM((HG_HEADS, HG_D, HG_D), F32)],
        compiler_params=pltpu.CompilerParams(
            dimension_semantics=("parallel", "arbitrary"),
            vmem_limit_bytes=V7X_VMEM_LIMIT_BYTES),
        name="hgrn2",
    )(r3(a), r3(qi), r3(kl), dec.reshape(batch, seq // HG_BLOCK, 1, w), r3(vh), r3(og), klm,
      vm, g)


def _attn_kernel(q_ref, kn_ref, kp_ref, vt_ref, knm_ref, kpm_ref, vtm_ref, o_ref,
                 m_ref, l_ref, acc_ref):
    i = pl.program_id(1)
    tq = q_ref.shape[0]
    hs = [slice(h * V7X_LANES, (h + 1) * V7X_LANES) for h in range(MLA_HEADS)]

    def q_of(h):
        return q_ref[:, 2 * h * V7X_LANES:2 * (h + 1) * V7X_LANES]

    def update(carry, s_list, vt_list):
        m, l, acc = carry
        m_new = m
        for s in s_list:
            m_new = jnp.maximum(m_new, jnp.max(s, axis=0, keepdims=True))
        alpha = jnp.exp2(m - m_new)
        l = alpha * l
        acc = alpha * acc
        for s, vt in zip(s_list, vt_list):
            p = jnp.exp2(s - m_new)
            l = l + jnp.sum(p, axis=0, keepdims=True)
            acc = acc + _dot(vt, p.astype(BF16))
        return m_new, l, acc

    def skewed(score_fn, consume_fn):
        scores = []
        for h in range(MLA_HEADS + ATTN_SKEW):
            if h < MLA_HEADS:
                scores.append(score_fn(h))
            g = h - ATTN_SKEW
            if g >= 0:
                consume_fn(g, scores[g])

    for h in range(MLA_HEADS):
        m_ref[h] = jnp.full((1, tq), -jnp.inf, F32)
        l_ref[h] = jnp.zeros((1, tq), F32)
        acc_ref[h] = jnp.zeros((V_HEAD, tq), F32)

    def kv_step(blocks):
        starts = [pl.multiple_of(b * ATTN_K, ATTN_K) for b in blocks]
        kps = [kp_ref[pl.ds(ks, ATTN_K), :] for ks in starts]

        def score(h):
            return [_dot_nt(jnp.concatenate([kn_ref[pl.ds(ks, ATTN_K), hs[h]], kp], axis=1),
                            q_of(h)) for ks, kp in zip(starts, kps)]

        def consume(g, s_list):
            m, l, acc = update((m_ref[g], l_ref[g], acc_ref[g]), s_list,
                               [vt_ref[b, hs[g], :] for b in blocks])
            m_ref[g] = m
            l_ref[g] = l
            acc_ref[g] = acc

        skewed(score, consume)

    def pair_step(j2, _):
        kv_step([2 * j2, 2 * j2 + 1])
        return 0

    lax.fori_loop(0, i // 2, pair_step, 0)

    @pl.when(i % 2 == 1)
    def _odd_block():
        kv_step([i - 1])

    ks = pl.multiple_of(i * ATTN_K, ATTN_K)
    kp = kp_ref[pl.ds(ks, ATTN_K), :]
    kpm = kpm_ref[...]
    keep = (lax.broadcasted_iota(jnp.int32, (ATTN_K, tq), 0)
            <= lax.broadcasted_iota(jnp.int32, (ATTN_K, tq), 1))

    def score_diag(h):
        kc = jnp.concatenate([kn_ref[pl.ds(ks, ATTN_K), hs[h]], kp], axis=1)
        km = jnp.concatenate([knm_ref[:, hs[h]], kpm], axis=1)
        return jnp.where(keep, _dot_nt(kc, q_of(h)), -jnp.inf), _dot_nt(km, q_of(h))

    def finish(g, s):
        _, l, acc = update((m_ref[g], l_ref[g], acc_ref[g]), list(s),
                           [vt_ref[i, hs[g], :], vtm_ref[hs[g], :]])
        o_ref[:, hs[g]] = (acc / l).T.astype(BF16)

    skewed(score_diag, finish)


def _attention(q, kn, kp, vt, knm, kpm, vtm, batch, seq):
    assert ATTN_Q == ATTN_K == INPROJ_ROWS // INPROJ_SPLIT
    w = kn.shape[-1]
    nq = seq // ATTN_Q
    r3 = lambda a: a.reshape(batch, seq, a.shape[-1])
    qblk = pl.BlockSpec((None, ATTN_Q, 2 * w), lambda b, i: (b, i, 0))
    oblk = pl.BlockSpec((None, ATTN_Q, w), lambda b, i: (b, i, 0))
    kfull = pl.BlockSpec((None, seq, w), lambda b, i: (b, 0, 0))
    kpfull = pl.BlockSpec((None, seq, V7X_LANES), lambda b, i: (b, 0, 0))
    vtfull = pl.BlockSpec((None, nq, w, ATTN_K), lambda b, i: (b, 0, 0, 0))
    mh = pl.BlockSpec((N_META, w), lambda b, i: (0, 0))
    mp = pl.BlockSpec((N_META, V7X_LANES), lambda b, i: (0, 0))
    mvt = pl.BlockSpec((w, N_META), lambda b, i: (0, 0))
    return pl.pallas_call(
        _attn_kernel,
        grid=(batch, nq),
        in_specs=[qblk, kfull, kpfull, vtfull, mh, mp, mvt],
        out_specs=oblk,
        out_shape=jax.ShapeDtypeStruct((batch, seq, w), BF16),
        scratch_shapes=[pltpu.VMEM((MLA_HEADS, 1, ATTN_Q), F32),
                        pltpu.VMEM((MLA_HEADS, 1, ATTN_Q), F32),
                        pltpu.VMEM((MLA_HEADS, V_HEAD, ATTN_Q), F32)],
        compiler_params=pltpu.CompilerParams(
            dimension_semantics=("parallel", "arbitrary"),
            vmem_limit_bytes=V7X_VMEM_LIMIT_BYTES),
        name="mla_attn",
    )(r3(q), r3(kn), r3(kp), vt.reshape(batch, nq, w, ATTN_K), knm, kpm,
      vtm.reshape(w, N_META))


def _out_kernel(x_ref, oh_ref, oa_ref, gates_ref, who_ref, wmo_ref, wout_ref, gmix_ref,
                gfpre_ref, wfin_ref, wfout_ref, gfpost_ref, y_ref):
    rows, d = x_ref.shape
    hidden = wfout_ref.shape[0]
    halves = [slice(i * rows // OUT_SPLIT, (i + 1) * rows // OUT_SPLIT) for i in range(OUT_SPLIT)]

    def merge(r):
        ya = _dot(oh_ref[r, :], who_ref[...])
        yb = _dot(oa_ref[r, :], wmo_ref[...])
        gates = gates_ref[r, :].astype(F32)
        return (gates[:, :d] * ya + gates[:, d:] * yb).astype(BF16)

    def mix_residual(r, merged):
        h1 = x_ref[r, :] + _rms(_dot(merged, wout_ref[...]), gmix_ref[...])
        return h1, _rms(h1, gfpre_ref[...]).astype(BF16)

    def ffn_act(u):
        gu = _dot(u, wfin_ref[...])
        gt = gu[:, :hidden]
        return (gt * _sigmoid(gt) * gu[:, hidden:]).astype(BF16)

    merged = [merge(r) for r in halves]
    h1_u = [mix_residual(r, m) for r, m in zip(halves, merged)]
    acts = [ffn_act(u) for _, u in h1_u]
    for r, (h1, _), act in zip(halves, h1_u, acts):
        y_ref[r, :] = h1 + _rms(_dot(act, wfout_ref[...]), gfpost_ref[...])


def _out_block(x2d, oh, oa, gates, params):
    t, d = x2d.shape
    n = t // OUT_ROWS
    row = lambda w: pl.BlockSpec((OUT_ROWS, w), lambda i: (i, 0))
    return pl.pallas_call(
        _out_kernel,
        grid=(n,),
        in_specs=[row(d), row(d), row(d), row(2 * d)] + [_resident(p.shape) for p in params],
        out_specs=row(d),
        out_shape=jax.ShapeDtypeStruct((t, d), F32),
        compiler_params=pltpu.CompilerParams(
            dimension_semantics=("parallel",), vmem_limit_bytes=V7X_VMEM_LIMIT_BYTES),
        name="merge_ffn",
    )(x2d, oh, oa, gates, *params)


def _rope_tables(length):
    pos = jnp.arange(length, dtype=F32)
    inv_freq = 1.0 / (ROPE_THETA ** (jnp.arange(0, QK_ROPE, 2, dtype=F32) / QK_ROPE))
    ang = pos[:, None] * inv_freq[None, :]
    cos, sin = jnp.cos(ang), jnp.sin(ang)
    zero = jnp.zeros((length, V7X_LANES - QK_ROPE), F32)
    return (jnp.concatenate([cos, cos, zero], axis=1),
            jnp.concatenate([-sin, sin, zero], axis=1))


def _swap_halves(w):
    half = w.shape[-1] // 2
    return jnp.concatenate([w[..., half:], w[..., :half]], axis=-1)


def kernel(x, meta_tokens, w_in, b_gate, lb_logits, hg_norm_g, w_hg_o, q_a_norm_g, w_q_b,
           kv_a_norm_g, w_kv_b, w_mla_o, w_out, mix_pre_g, mix_post_g, ffn_pre_g, ffn_post_g,
           w_ffn_in, w_ffn_out):
    batch, seq, d = x.shape
    assert w_in.shape[0] == 1, "single-layer block"
    assert seq % INPROJ_ROWS == 0 and seq % ATTN_Q == 0 and seq % HG_STEP == 0
    assert INPROJ_ROWS % HG_BLOCK == 0 and HG_STEP % HG_BLOCK == 0
    assert (batch * seq) % OUT_ROWS == 0
    hgw = HG_HEADS * HG_D
    row = lambda a: a.reshape(1, -1).astype(F32)

    wi = w_in[0]
    o = 0
    parts = []
    for sz in (hgw, hgw, hgw, hgw, Q_LORA + KV_LORA, QK_ROPE, 2 * d):
        parts.append(wi[:, o:o + sz])
        o += sz
    whq, whf, whi, whg, wc, wkpe, wgate = parts
    zpad = jnp.zeros((d, V7X_LANES - QK_ROPE), wi.dtype)
    wpe = jnp.concatenate([wkpe, zpad, _swap_halves(wkpe), zpad], axis=1)
    wq = w_q_b[0].reshape(Q_LORA, MLA_HEADS, QK_NOPE + QK_ROPE)
    wqn = wq[:, :, :QK_NOPE].reshape(Q_LORA, MLA_HEADS * QK_NOPE)
    wq_pe = wq[:, :, QK_NOPE:]
    zq = jnp.zeros((Q_LORA, MLA_HEADS, V7X_LANES - QK_ROPE), wq.dtype)
    wqp = jnp.concatenate([wq_pe, zq], axis=2).reshape(Q_LORA, MLA_HEADS * V7X_LANES)
    wqpr = jnp.concatenate([_swap_halves(wq_pe), zq], axis=2).reshape(
        Q_LORA, MLA_HEADS * V7X_LANES)
    wkv = w_kv_b[0].reshape(KV_LORA, MLA_HEADS, QK_NOPE + V_HEAD)
    wkn = wkv[:, :, :QK_NOPE].reshape(KV_LORA, MLA_HEADS * QK_NOPE)
    wv = wkv[:, :, QK_NOPE:].reshape(KV_LORA, MLA_HEADS * V_HEAD).T
    bf = lambda a: a.astype(BF16)
    inproj_params = [row(mix_pre_g[0]), bf(whq), bf(whf), bf(whi), bf(whg), bf(wc), bf(wpe),
                     bf(wgate), row(b_gate[0]), lb_logits.astype(F32), row(q_a_norm_g[0]),
                     row(kv_a_norm_g[0]), bf(wqn), bf(wqp), bf(wqpr), bf(wkn), bf(wv)]
    cos_t, sin_t = _rope_tables(N_META + seq)

    m_out = _inproj(meta_tokens.astype(F32), N_META, 1, inproj_params,
                    cos_t[:N_META], sin_t[:N_META], with_intra=False)
    _, kl_m, _, vh_m, _, _, _, kn_m, kp_m, va_m = m_out

    x2d = x.reshape(batch * seq, d)
    (qi, kl, dec, vh, og, gates, q, kn, kp, va, a_hg) = _inproj(
        x2d, INPROJ_ROWS, seq // INPROJ_ROWS, inproj_params, cos_t[N_META:], sin_t[N_META:],
        with_intra=True)

    o_hg = _hgrn(a_hg, qi, kl, dec, vh, og, kl_m, vh_m, row(hg_norm_g[0]), batch, seq)
    o_at = _attention(q, kn, kp, va, kn_m, kp_m, va_m, batch, seq)

    out_params = [bf(w_hg_o[0]), bf(w_mla_o[0]), bf(w_out[0]), row(mix_post_g[0]),
                  row(ffn_pre_g[0]), bf(w_ffn_in[0]), bf(w_ffn_out[0]), row(ffn_post_g[0])]
    y = _out_block(x2d, o_hg.reshape(batch * seq, hgw), o_at.reshape(batch * seq, -1),
                   gates, out_params)
    return y.reshape(batch, seq, d)
```

```python
import functools
import math

import jax
import jax.numpy as jnp
from jax import lax
from jax.experimental import pallas as pl
from jax.experimental.pallas import tpu as pltpu

N_META = 16
NORM_EPS = 1e-6
HG_HEADS = 8
HG_D = 128
MLA_HEADS = 8
QK_NOPE = 128
QK_ROPE = 64
V_HEAD = 128
Q_LORA = 256
KV_LORA = 256
ROPE_THETA = 10000.0

V7X_LANES = 128
V7X_VMEM_LIMIT_BYTES = 56 * 1024 * 1024

INPROJ_ROWS = 512
INPROJ_SPLIT = 2
HG_BLOCK = 64
HG_STEP = 512
HG_SAFE_EXPONENT = 60.0
ATTN_Q = 256
ATTN_K = 256
ATTN_SKEW = 8
OUT_ROWS = 512
OUT_SPLIT = 2

F32 = jnp.float32
BF16 = jnp.bfloat16


def _dot(a, b):
    return jnp.dot(a, b, preferred_element_type=F32)


def _dot_nt(a, b):
    return lax.dot_general(a, b, (((1,), (1,)), ((), ())), preferred_element_type=F32)


def _dot_tn(a, b):
    return lax.dot_general(a, b, (((0,), (0,)), ((), ())), preferred_element_type=F32)


def _rms(x, g):
    ms = jnp.mean(x * x, axis=-1, keepdims=True)
    return x * lax.rsqrt(ms + NORM_EPS) * g


def _sigmoid(x):
    return 1.0 / (1.0 + jnp.exp(-x))


def _resident(shape):
    return pl.BlockSpec(shape, lambda *_: (0,) * len(shape), pipeline_mode=pl.Buffered(1))


def _cumsum_blocks(x, block):
    pos = lax.broadcasted_iota(jnp.int32, x.shape, 0) % block
    s = 1
    while s < block:
        x = x + jnp.where(pos >= s, pltpu.roll(x, s, 0), 0.0)
        s *= 2
    return x


def _inproj_kernel(x_ref, gpre_ref, whq_ref, whf_ref, whi_ref, whg_ref, wc_ref, wpe_ref,
                   wgate_ref, bgate_ref, lbl_ref, qg_ref, kvg_ref, wqn_ref, wqp_ref,
                   wqpr_ref, wkn_ref, wv_ref, cos_ref, sin_ref,
                   qi_ref, kl_ref, dec_ref, vh_ref, og_ref, gates_ref,
                   q_ref, kn_ref, kp_ref, va_ref, *intra_refs, scale, hg_block, split):
    rows, d = x_ref.shape
    part_rows = rows // split
    parts = [slice(i * part_rows, (i + 1) * part_rows) for i in range(split)]
    nblk = part_rows // hg_block
    mid = hg_block // 2 - 1
    pair_w = 2 * HG_D

    lbl = lbl_ref[...]
    e = jnp.exp(lbl - jnp.max(lbl, axis=0, keepdims=True))
    lb = e[0:1] / jnp.sum(e, axis=0, keepdims=True)

    stage1 = []
    for r in parts:
        u = _rms(x_ref[r, :], gpre_ref[...]).astype(BF16)
        hq = _dot(u, whq_ref[...])
        hf = _dot(u, whf_ref[...])
        vh_ref[r, :] = _dot(u, whi_ref[...]).astype(BF16)
        pe = _dot(u, wpe_ref[...])
        cmla = _dot(u, wc_ref[...])
        stage1.append((u, hq, hf, pe, cmla))

    stage2 = []
    for _, hq, hf, _, _ in stage1:
        q = hq * _sigmoid(hq)
        sg = _sigmoid(hf)
        k = (1.0 - lb) * (1.0 - sg)
        c = _cumsum_blocks(jnp.log(lb + (1.0 - lb) * sg), hg_block)
        stage2.append((q, k, c))

    zero = jnp.zeros((hg_block, HG_D), BF16)
    if intra_refs:
        a_ref, ksc_ref, csc_ref = intra_refs
        tri = (lax.broadcasted_iota(jnp.int32, (hg_block, 2 * hg_block), 0)
               >= lax.broadcasted_iota(jnp.int32, (hg_block, 2 * hg_block), 1) % hg_block)
        worst = jnp.zeros((1, d), F32)
    for idx, r in enumerate(parts):
        u, _, _, pe, cmla = stage1[idx]
        q, k, c = stage2[idx]
        cos = cos_ref[r, :]
        sin = sin_ref[r, :]

        def gate_half(half):
            cols = slice(half * d, (half + 1) * d)
            gates_ref[r, cols] = _sigmoid(
                _dot(u, wgate_ref[:, cols]) + bgate_ref[:, cols]).astype(BF16)

        def out_gate():
            hg = _dot(u, whg_ref[...])
            og_ref[r, :] = (hg * _sigmoid(hg)).astype(BF16)

        def mla_q():
            cq = _rms(cmla[:, :Q_LORA], qg_ref[...]).astype(BF16)
            qn = _dot(cq, wqn_ref[...])
            qpe = _dot(cq, wqp_ref[...])
            qper = _dot(cq, wqpr_ref[...])
            for h in range(MLA_HEADS):
                sl = slice(h * V7X_LANES, (h + 1) * V7X_LANES)
                base = 2 * h * V7X_LANES
                q_ref[r, base:base + V7X_LANES] = (qn[:, sl] * scale).astype(BF16)
                q_ref[r, base + V7X_LANES:base + 2 * V7X_LANES] = (
                    (qpe[:, sl] * cos + qper[:, sl] * sin) * scale).astype(BF16)

        def mla_kv():
            kp_ref[r, :] = (pe[:, :V7X_LANES] * cos + pe[:, V7X_LANES:] * sin).astype(BF16)
            ckv = _rms(cmla[:, Q_LORA:], kvg_ref[...]).astype(BF16)
            kn_ref[r, :] = _dot(ckv, wkn_ref[...]).astype(BF16)
            va_ref[idx] = _dot_nt(wv_ref[...], ckv).astype(BF16)

        fillers = [out_gate, functools.partial(gate_half, 0), functools.partial(gate_half, 1),
                   mla_q, mla_kv]
        for blk in range(nblk):
            rb = slice(blk * hg_block, (blk + 1) * hg_block)
            ro = slice(r.start + rb.start, r.start + rb.stop)
            cb = c[rb]
            cm = cb[mid:mid + 1]
            cl = cb[hg_block - 1:hg_block]
            qi_ref[ro, :] = (q[rb] * jnp.exp(cb)).astype(BF16)
            kl_ref[ro, :] = (k[rb] * jnp.exp(cl - cb)).astype(BF16)
            dec_ref[idx * nblk + blk] = jnp.exp(cl)
            if intra_refs:
                worst = jnp.maximum(worst, jnp.maximum(cb[0:1] - cm, cm - cl))
                qt = (q[rb] * jnp.exp(cb - cm)).astype(BF16)
                kt = (k[rb] * jnp.exp(cm - cb)).astype(BF16)
                for p in range(HG_HEADS // 2):
                    ls = slice(p * pair_w, (p + 1) * pair_w)
                    kb = jnp.concatenate(
                        [jnp.concatenate([kt[:, ls][:, :HG_D], zero], axis=1),
                         jnp.concatenate([zero, kt[:, ls][:, HG_D:]], axis=1)], axis=0)
                    a = jnp.where(tri, _dot_nt(qt[:, ls], kb), 0.0)
                    a_ref[ro, p * 2 * hg_block:(p + 1) * 2 * hg_block] = a.astype(BF16)
            if fillers:
                fillers.pop(0)()
        while fillers:
            fillers.pop(0)()

    if intra_refs:
        @pl.when(jnp.max(worst) > HG_SAFE_EXPONENT)
        def _exact_intra():
            a_w = HG_HEADS * hg_block
            head_of_lane = lax.broadcasted_iota(jnp.int32, (d, HG_HEADS), 0) // HG_D
            head_sum = (head_of_lane
                        == lax.broadcasted_iota(jnp.int32, (d, HG_HEADS), 1)).astype(F32)
            col_base = lax.broadcasted_iota(jnp.int32, (HG_HEADS, a_w), 0) * hg_block
            col_lane = lax.broadcasted_iota(jnp.int32, (HG_HEADS, a_w), 1)
            causal = (lax.broadcasted_iota(jnp.int32, (hg_block, a_w), 0)
                      >= lax.broadcasted_iota(jnp.int32, (hg_block, a_w), 1) % hg_block)
            for r, (q, k, c) in zip(parts, stage2):
                ksc_ref[...] = k
                csc_ref[...] = c
                for blk in range(nblk):
                    rb = slice(blk * hg_block, (blk + 1) * hg_block)
                    first = r.start + rb.start
                    qb = q[rb]
                    cb = c[rb]

                    def key_column(s, acc):
                        row = rb.start + s
                        w = (qb * jnp.exp(jnp.minimum(cb - csc_ref[pl.ds(row, 1), :], 0.0))
                             * ksc_ref[pl.ds(row, 1), :])
                        cols = _dot(w, head_sum)
                        place = (col_lane == col_base + s).astype(F32)
                        return acc + _dot(cols, place)

                    acc = lax.fori_loop(0, hg_block, key_column,
                                        jnp.zeros((hg_block, a_w), F32))
                    a_ref[first:first + hg_block, :] = jnp.where(causal, acc, 0.0).astype(BF16)


def _inproj(x2d, rows, pos_blocks, params, cos_t, sin_t, with_intra):
    t, d = x2d.shape
    n = t // rows
    row = lambda w: pl.BlockSpec((rows, w), lambda i: (i, 0))
    pos = pl.BlockSpec((rows, V7X_LANES), lambda i: (i % pos_blocks, 0))
    w_specs = [_resident(p.shape) for p in params]
    hg_block = min(HG_BLOCK, rows)
    nb = rows // hg_block
    split = INPROJ_SPLIT if rows % (INPROJ_SPLIT * hg_block) == 0 else 1
    vt_w = MLA_HEADS * V_HEAD
    lead = lambda *shape: pl.BlockSpec((None,) + shape, lambda i: (i,) + (0,) * len(shape))
    outs = [(row(d), (t, d), BF16), (row(d), (t, d), BF16),
            (lead(nb, 1, d), (n, nb, 1, d), F32),
            (row(d), (t, d), BF16), (row(d), (t, d), BF16), (row(2 * d), (t, 2 * d), BF16),
            (row(2 * d), (t, 2 * d), BF16), (row(d), (t, d), BF16),
            (row(V7X_LANES), (t, V7X_LANES), BF16),
            (lead(split, vt_w, rows // split), (n, split, vt_w, rows // split), BF16)]
    scratch = []
    if with_intra:
        a_w = HG_HEADS * hg_block
        outs.append((row(a_w), (t, a_w), BF16))
        scratch = [pltpu.VMEM((rows // split, d), F32), pltpu.VMEM((rows // split, d), F32)]
    scale = (QK_NOPE + QK_ROPE) ** -0.5 * math.log2(math.e)
    return pl.pallas_call(
        functools.partial(_inproj_kernel, scale=scale, hg_block=hg_block, split=split),
        grid=(n,),
        in_specs=[row(d)] + w_specs + [pos, pos],
        out_specs=[o[0] for o in outs],
        out_shape=[jax.ShapeDtypeStruct(o[1], o[2]) for o in outs],
        scratch_shapes=scratch,
        compiler_params=pltpu.CompilerParams(
            dimension_semantics=("parallel",), vmem_limit_bytes=V7X_VMEM_LIMIT_BYTES),
        name="inproj",
    )(x2d, *params, cos_t, sin_t)


def _hgrn_kernel(a_ref, qi_ref, kl_ref, dec_ref, v_ref, og_ref, klm_ref, vm_ref, g_ref,
                 o_ref, st_ref):
    n = pl.program_id(1)
    c_blk = qi_ref.shape[0]
    hs = [slice(h * HG_D, (h + 1) * HG_D) for h in range(HG_HEADS)]

    @pl.when(n == 0)
    def _init():
        vm = vm_ref[...]
        klm = klm_ref[...]
        for h, sl in enumerate(hs):
            st_ref[h] = _dot_tn(vm[:, sl], klm[:, sl])

    g = g_ref[...]
    zero = jnp.zeros((HG_BLOCK, HG_D), BF16)
    for blk in range(c_blk // HG_BLOCK):
        r = slice(blk * HG_BLOCK, (blk + 1) * HG_BLOCK)
        a = a_ref[r, :]
        qi = qi_ref[r, :]
        kl = kl_ref[r, :]
        v = v_ref[r, :]
        dec = dec_ref[blk]
        outs = []
        for p in range(HG_HEADS // 2):
            ha, hb = hs[2 * p], hs[2 * p + 1]
            vb = jnp.concatenate([jnp.concatenate([v[:, ha], zero], axis=1),
                                  jnp.concatenate([zero, v[:, hb]], axis=1)], axis=0)
            o_pair = _dot(a[:, p * 2 * HG_BLOCK:(p + 1) * 2 * HG_BLOCK], vb)
            for e, sl in enumerate((ha, hb)):
                h = 2 * p + e
                st = st_ref[h]
                outs.append(o_pair[:, e * HG_D:(e + 1) * HG_D]
                            + _dot_nt(qi[:, sl], st.astype(BF16)))
                st_ref[h] = st * dec[:, sl] + _dot_tn(v[:, sl], kl[:, sl])
        for o, sl in zip(outs, hs):
            o_ref[r, sl] = (_rms(o, g) * og_ref[r, sl].astype(F32)).astype(BF16)


def _hgrn(a, qi, kl, dec, vh, og, klm, vm, g, batch, seq):
    w = qi.shape[-1]
    nstep = seq // HG_STEP
    sub = HG_STEP // HG_BLOCK
    tok = lambda width: pl.BlockSpec((None, HG_STEP, width), lambda b, n: (b, n, 0))
    meta = pl.BlockSpec((N_META, w), lambda b, n: (0, 0))
    r3 = lambda x: x.reshape(batch, seq, x.shape[-1])
    return pl.pallas_call(
        _hgrn_kernel,
        grid=(batch, nstep),
        in_specs=[tok(a.shape[-1]), tok(w), tok(w),
                  pl.BlockSpec((None, sub, 1, w), lambda b, n: (b, n, 0, 0)),
                  tok(w), tok(w), meta, meta,
                  pl.BlockSpec((1, HG_D), lambda b, n: (0, 0))],
        out_specs=tok(w),
        out_shape=jax.ShapeDtypeStruct((batch, seq, w), BF16),
        scratch_shapes=[pltpu.VMEM((HG_HEADS, HG_D, HG_D), F32)],
        compiler_params=pltpu.CompilerParams(
            dimension_semantics=("parallel", "arbitrary"),
            vmem_limit_bytes=V7X_VMEM_LIMIT_BYTES),
        name="hgrn2",
    )(r3(a), r3(qi), r3(kl), dec.reshape(batch, seq // HG_BLOCK, 1, w), r3(vh), r3(og), klm,
      vm, g)


def _attn_kernel(q_ref, kn_ref, kp_ref, vt_ref, knm_ref, kpm_ref, vtm_ref, o_ref,
                 m_ref, l_ref, acc_ref):
    i = pl.program_id(1)
    tq = q_ref.shape[0]
    hs = [slice(h * V7X_LANES, (h + 1) * V7X_LANES) for h in range(MLA_HEADS)]

    def q_of(h):
        return q_ref[:, 2 * h * V7X_LANES:2 * (h + 1) * V7X_LANES]

    def update(carry, s_list, vt_list):
        m, l, acc = carry
        m_new = m
        for s in s_list:
            m_new = jnp.maximum(m_new, jnp.max(s, axis=0, keepdims=True))
        alpha = jnp.exp2(m - m_new)
        l = alpha * l
        acc = alpha * acc
        for s, vt in zip(s_list, vt_list):
            p = jnp.exp2(s - m_new)
            l = l + jnp.sum(p, axis=0, keepdims=True)
            acc = acc + _dot(vt, p.astype(BF16))
        return m_new, l, acc

    def skewed(score_fn, consume_fn):
        scores = []
        for h in range(MLA_HEADS + ATTN_SKEW):
            if h < MLA_HEADS:
                scores.append(score_fn(h))
            g = h - ATTN_SKEW
            if g >= 0:
                consume_fn(g, scores[g])

    for h in range(MLA_HEADS):
        m_ref[h] = jnp.full((1, tq), -jnp.inf, F32)
        l_ref[h] = jnp.zeros((1, tq), F32)
        acc_ref[h] = jnp.zeros((V_HEAD, tq), F32)

    def kv_step(blocks):
        starts = [pl.multiple_of(b * ATTN_K, ATTN_K) for b in blocks]
        kps = [kp_ref[pl.ds(ks, ATTN_K), :] for ks in starts]

        def score(h):
            return [_dot_nt(jnp.concatenate([kn_ref[pl.ds(ks, ATTN_K), hs[h]], kp], axis=1),
                            q_of(h)) for ks, kp in zip(starts, kps)]

        def consume(g, s_list):
            m, l, acc = update((m_ref[g], l_ref[g], acc_ref[g]), s_list,
                               [vt_ref[b, hs[g], :] for b in blocks])
            m_ref[g] = m
            l_ref[g] = l
            acc_ref[g] = acc

        skewed(score, consume)

    def pair_step(j2, _):
        kv_step([2 * j2, 2 * j2 + 1])
        return 0

    lax.fori_loop(0, i // 2, pair_step, 0)

    def last_step(full_blocks):
        blocks = list(full_blocks) + [i]
        starts = [pl.multiple_of(b * ATTN_K, ATTN_K) for b in blocks]
        kps = [kp_ref[pl.ds(ks, ATTN_K), :] for ks in starts]
        kpm = kpm_ref[...]
        keep = (lax.broadcasted_iota(jnp.int32, (ATTN_K, tq), 0)
                <= lax.broadcasted_iota(jnp.int32, (ATTN_K, tq), 1))

        def score(h):
            s = [_dot_nt(jnp.concatenate([kn_ref[pl.ds(ks, ATTN_K), hs[h]], kp], axis=1),
                         q_of(h)) for ks, kp in zip(starts, kps)]
            s[-1] = jnp.where(keep, s[-1], -jnp.inf)
            km = jnp.concatenate([knm_ref[:, hs[h]], kpm], axis=1)
            return s + [_dot_nt(km, q_of(h))]

        def finish(g, s_list):
            _, l, acc = update((m_ref[g], l_ref[g], acc_ref[g]), s_list,
                               [vt_ref[b, hs[g], :] for b in blocks] + [vtm_ref[hs[g], :]])
            o_ref[hs[g], :] = (acc / l).astype(BF16)

        skewed(score, finish)

    @pl.when(i % 2 == 1)
    def _odd_tail():
        last_step([i - 1])

    @pl.when(i % 2 == 0)
    def _even_tail():
        last_step([])


def _attention(q, kn, kp, vt, knm, kpm, vtm, batch, seq):
    assert ATTN_Q == ATTN_K == INPROJ_ROWS // INPROJ_SPLIT
    w = kn.shape[-1]
    nq = seq // ATTN_Q
    r3 = lambda a: a.reshape(batch, seq, a.shape[-1])
    qblk = pl.BlockSpec((None, ATTN_Q, 2 * w), lambda b, i: (b, i, 0))
    oblk = pl.BlockSpec((None, None, w, ATTN_Q), lambda b, i: (b, i, 0, 0))
    kfull = pl.BlockSpec((None, seq, w), lambda b, i: (b, 0, 0))
    kpfull = pl.BlockSpec((None, seq, V7X_LANES), lambda b, i: (b, 0, 0))
    vtfull = pl.BlockSpec((None, nq, w, ATTN_K), lambda b, i: (b, 0, 0, 0))
    mh = pl.BlockSpec((N_META, w), lambda b, i: (0, 0))
    mp = pl.BlockSpec((N_META, V7X_LANES), lambda b, i: (0, 0))
    mvt = pl.BlockSpec((w, N_META), lambda b, i: (0, 0))
    return pl.pallas_call(
        _attn_kernel,
        grid=(batch, nq),
        in_specs=[qblk, kfull, kpfull, vtfull, mh, mp, mvt],
        out_specs=oblk,
        out_shape=jax.ShapeDtypeStruct((batch, nq, w, ATTN_Q), BF16),
        scratch_shapes=[pltpu.VMEM((MLA_HEADS, 1, ATTN_Q), F32),
                        pltpu.VMEM((MLA_HEADS, 1, ATTN_Q), F32),
                        pltpu.VMEM((MLA_HEADS, V_HEAD, ATTN_Q), F32)],
        compiler_params=pltpu.CompilerParams(
            dimension_semantics=("parallel", "arbitrary"),
            vmem_limit_bytes=V7X_VMEM_LIMIT_BYTES),
        name="mla_attn",
    )(r3(q), r3(kn), r3(kp), vt.reshape(batch, nq, w, ATTN_K), knm, kpm,
      vtm.reshape(w, N_META))


def _out_kernel(x_ref, oh_ref, oat_ref, gates_ref, who_ref, wmo_ref, wout_ref, gmix_ref,
                gfpre_ref, wfin_ref, wfout_ref, gfpost_ref, y_ref):
    rows, d = x_ref.shape
    hidden = wfout_ref.shape[0]
    halves = [slice(i * rows // OUT_SPLIT, (i + 1) * rows // OUT_SPLIT) for i in range(OUT_SPLIT)]

    def merge(idx, r):
        ya = _dot(oh_ref[r, :], who_ref[...])
        yb = _dot_tn(oat_ref[idx], wmo_ref[...])
        gates = gates_ref[r, :].astype(F32)
        return (gates[:, :d] * ya + gates[:, d:] * yb).astype(BF16)

    def mix_residual(r, merged):
        h1 = x_ref[r, :] + _rms(_dot(merged, wout_ref[...]), gmix_ref[...])
        return h1, _rms(h1, gfpre_ref[...]).astype(BF16)

    def ffn_act(u):
        gu = _dot(u, wfin_ref[...])
        gt = gu[:, :hidden]
        return (gt * _sigmoid(gt) * gu[:, hidden:]).astype(BF16)

    merged = [merge(idx, r) for idx, r in enumerate(halves)]
    h1_u = [mix_residual(r, m) for r, m in zip(halves, merged)]
    acts = [ffn_act(u) for _, u in h1_u]
    for r, (h1, _), act in zip(halves, h1_u, acts):
        y_ref[r, :] = h1 + _rms(_dot(act, wfout_ref[...]), gfpost_ref[...])


def _out_block(x2d, oh, oa, gates, params):
    t, d = x2d.shape
    n = t // OUT_ROWS
    row = lambda w: pl.BlockSpec((OUT_ROWS, w), lambda i: (i, 0))
    return pl.pallas_call(
        _out_kernel,
        grid=(n,),
        in_specs=[row(d), row(d),
                  pl.BlockSpec((OUT_SPLIT, oa.shape[1], OUT_ROWS // OUT_SPLIT), lambda i: (i, 0, 0)),
                  row(2 * d)] + [_resident(p.shape) for p in params],
        out_specs=row(d),
        out_shape=jax.ShapeDtypeStruct((t, d), F32),
        compiler_params=pltpu.CompilerParams(
            dimension_semantics=("parallel",), vmem_limit_bytes=V7X_VMEM_LIMIT_BYTES),
        name="merge_ffn",
    )(x2d, oh, oa, gates, *params)


def _rope_tables(length):
    pos = jnp.arange(length, dtype=F32)
    inv_freq = 1.0 / (ROPE_THETA ** (jnp.arange(0, QK_ROPE, 2, dtype=F32) / QK_ROPE))
    ang = pos[:, None] * inv_freq[None, :]
    cos, sin = jnp.cos(ang), jnp.sin(ang)
    zero = jnp.zeros((length, V7X_LANES - QK_ROPE), F32)
    return (jnp.concatenate([cos, cos, zero], axis=1),
            jnp.concatenate([-sin, sin, zero], axis=1))


def _swap_halves(w):
    half = w.shape[-1] // 2
    return jnp.concatenate([w[..., half:], w[..., :half]], axis=-1)


def kernel(x, meta_tokens, w_in, b_gate, lb_logits, hg_norm_g, w_hg_o, q_a_norm_g, w_q_b,
           kv_a_norm_g, w_kv_b, w_mla_o, w_out, mix_pre_g, mix_post_g, ffn_pre_g, ffn_post_g,
           w_ffn_in, w_ffn_out):
    batch, seq, d = x.shape
    assert w_in.shape[0] == 1, "single-layer block"
    assert seq % INPROJ_ROWS == 0 and seq % ATTN_Q == 0 and seq % HG_STEP == 0
    assert INPROJ_ROWS % HG_BLOCK == 0 and HG_STEP % HG_BLOCK == 0
    assert (batch * seq) % OUT_ROWS == 0
    hgw = HG_HEADS * HG_D
    row = lambda a: a.reshape(1, -1).astype(F32)

    wi = w_in[0]
    o = 0
    parts = []
    for sz in (hgw, hgw, hgw, hgw, Q_LORA + KV_LORA, QK_ROPE, 2 * d):
        parts.append(wi[:, o:o + sz])
        o += sz
    whq, whf, whi, whg, wc, wkpe, wgate = parts
    zpad = jnp.zeros((d, V7X_LANES - QK_ROPE), wi.dtype)
    wpe = jnp.concatenate([wkpe, zpad, _swap_halves(wkpe), zpad], axis=1)
    wq = w_q_b[0].reshape(Q_LORA, MLA_HEADS, QK_NOPE + QK_ROPE)
    wqn = wq[:, :, :QK_NOPE].reshape(Q_LORA, MLA_HEADS * QK_NOPE)
    wq_pe = wq[:, :, QK_NOPE:]
    zq = jnp.zeros((Q_LORA, MLA_HEADS, V7X_LANES - QK_ROPE), wq.dtype)
    wqp = jnp.concatenate([wq_pe, zq], axis=2).reshape(Q_LORA, MLA_HEADS * V7X_LANES)
    wqpr = jnp.concatenate([_swap_halves(wq_pe), zq], axis=2).reshape(
        Q_LORA, MLA_HEADS * V7X_LANES)
    wkv = w_kv_b[0].reshape(KV_LORA, MLA_HEADS, QK_NOPE + V_HEAD)
    wkn = wkv[:, :, :QK_NOPE].reshape(KV_LORA, MLA_HEADS * QK_NOPE)
    wv = wkv[:, :, QK_NOPE:].reshape(KV_LORA, MLA_HEADS * V_HEAD).T
    bf = lambda a: a.astype(BF16)
    inproj_params = [row(mix_pre_g[0]), bf(whq), bf(whf), bf(whi), bf(whg), bf(wc), bf(wpe),
                     bf(wgate), row(b_gate[0]), lb_logits.astype(F32), row(q_a_norm_g[0]),
                     row(kv_a_norm_g[0]), bf(wqn), bf(wqp), bf(wqpr), bf(wkn), bf(wv)]
    cos_t, sin_t = _rope_tables(N_META + seq)

    m_out = _inproj(meta_tokens.astype(F32), N_META, 1, inproj_params,
                    cos_t[:N_META], sin_t[:N_META], with_intra=False)
    _, kl_m, _, vh_m, _, _, _, kn_m, kp_m, va_m = m_out

    x2d = x.reshape(batch * seq, d)
    (qi, kl, dec, vh, og, gates, q, kn, kp, va, a_hg) = _inproj(
        x2d, INPROJ_ROWS, seq // INPROJ_ROWS, inproj_params, cos_t[N_META:], sin_t[N_META:],
        with_intra=True)

    o_hg = _hgrn(a_hg, qi, kl, dec, vh, og, kl_m, vh_m, row(hg_norm_g[0]), batch, seq)
    o_at = _attention(q, kn, kp, va, kn_m, kp_m, va_m, batch, seq)

    out_params = [bf(w_hg_o[0]), bf(w_mla_o[0]), bf(w_out[0]), row(mix_post_g[0]),
                  row(ffn_pre_g[0]), bf(w_ffn_in[0]), bf(w_ffn_out[0]), row(ffn_post_g[0])]
    assert OUT_ROWS // OUT_SPLIT == ATTN_Q
    y = _out_block(x2d, o_hg.reshape(batch * seq, hgw),
                   o_at.reshape(batch * (seq // ATTN_Q), -1, ATTN_Q), gates, out_params)
    return y.reshape(batch, seq, d)
```

```python
import functools
import math

import jax
import jax.numpy as jnp
from jax import lax
from jax.experimental import pallas as pl
from jax.experimental.pallas import tpu as pltpu

N_META = 16
NORM_EPS = 1e-6
HG_HEADS = 8
HG_D = 128
MLA_HEADS = 8
QK_NOPE = 128
QK_ROPE = 64
V_HEAD = 128
Q_LORA = 256
KV_LORA = 256
ROPE_THETA = 10000.0

V7X_LANES = 128
V7X_VMEM_LIMIT_BYTES = 56 * 1024 * 1024

INPROJ_ROWS = 512
INPROJ_SPLIT = 2
HG_BLOCK = 64
HG_STEP = 512
HG_SAFE_EXPONENT = 60.0
ATTN_Q = 256
ATTN_K = 256
ATTN_SKEW = 8
OUT_ROWS = 512
OUT_SPLIT = 2

F32 = jnp.float32
BF16 = jnp.bfloat16


def _dot(a, b):
    return jnp.dot(a, b, preferred_element_type=F32)


def _dot_nt(a, b):
    return lax.dot_general(a, b, (((1,), (1,)), ((), ())), preferred_element_type=F32)


def _dot_tn(a, b):
    return lax.dot_general(a, b, (((0,), (0,)), ((), ())), preferred_element_type=F32)


def _rms(x, g):
    ms = jnp.mean(x * x, axis=-1, keepdims=True)
    return x * lax.rsqrt(ms + NORM_EPS) * g


def _sigmoid(x):
    return 1.0 / (1.0 + jnp.exp(-x))


def _rope(slab, cos, sin):
    half = slab.shape[1] // 2
    return slab * cos + pltpu.roll(slab, half, 1) * sin


def _resident(shape):
    return pl.BlockSpec(shape, lambda *_: (0,) * len(shape), pipeline_mode=pl.Buffered(1))


def _cumsum_blocks(x, block):
    pos = lax.broadcasted_iota(jnp.int32, x.shape, 0) % block
    s = 1
    while s < block:
        x = x + jnp.where(pos >= s, pltpu.roll(x, s, 0), 0.0)
        s *= 2
    return x


def _inproj_kernel(x_ref, gpre_ref, whq_ref, whf_ref, whi_ref, whg_ref, wc_ref, wpe_ref,
                   wgate_ref, bgate_ref, lbl_ref, qg_ref, kvg_ref, wqn_ref, wqp_ref,
                   wkn_ref, wv_ref, cos_ref, sin_ref,
                   qi_ref, kl_ref, dec_ref, vh_ref, og_ref, gates_ref,
                   q_ref, kn_ref, kp_ref, va_ref, *intra_refs, scale, hg_block, split):
    rows, d = x_ref.shape
    part_rows = rows // split
    parts = [slice(i * part_rows, (i + 1) * part_rows) for i in range(split)]
    nblk = part_rows // hg_block
    mid = hg_block // 2 - 1
    pair_w = 2 * HG_D

    lbl = lbl_ref[...]
    e = jnp.exp(lbl - jnp.max(lbl, axis=0, keepdims=True))
    lb = e[0:1] / jnp.sum(e, axis=0, keepdims=True)

    stage1 = []
    for r in parts:
        u = _rms(x_ref[r, :], gpre_ref[...]).astype(BF16)
        hq = _dot(u, whq_ref[...])
        hf = _dot(u, whf_ref[...])
        vh_ref[r, :] = _dot(u, whi_ref[...]).astype(BF16)
        pe = _dot(u, wpe_ref[...])
        cmla = _dot(u, wc_ref[...])
        stage1.append((u, hq, hf, pe, cmla))

    stage2 = []
    for _, hq, hf, _, _ in stage1:
        q = hq * _sigmoid(hq)
        sg = _sigmoid(hf)
        k = (1.0 - lb) * (1.0 - sg)
        c = _cumsum_blocks(jnp.log(lb + (1.0 - lb) * sg), hg_block)
        stage2.append((q, k, c))

    zero = jnp.zeros((hg_block, HG_D), BF16)
    if intra_refs:
        a_ref, ksc_ref, csc_ref = intra_refs
        tri = (lax.broadcasted_iota(jnp.int32, (hg_block, 2 * hg_block), 0)
               >= lax.broadcasted_iota(jnp.int32, (hg_block, 2 * hg_block), 1) % hg_block)
        worst = jnp.zeros((1, d), F32)
    for idx, r in enumerate(parts):
        u, _, _, pe, cmla = stage1[idx]
        q, k, c = stage2[idx]
        cos = cos_ref[r, :]
        sin = sin_ref[r, :]

        def gate_half(half):
            cols = slice(half * d, (half + 1) * d)
            gates_ref[r, cols] = _sigmoid(
                _dot(u, wgate_ref[:, cols]) + bgate_ref[:, cols]).astype(BF16)

        def out_gate():
            hg = _dot(u, whg_ref[...])
            og_ref[r, :] = (hg * _sigmoid(hg)).astype(BF16)

        def mla_q():
            cq = _rms(cmla[:, :Q_LORA], qg_ref[...]).astype(BF16)
            qn = _dot(cq, wqn_ref[...])
            qpe = _dot(cq, wqp_ref[...])
            for h in range(MLA_HEADS):
                sl = slice(h * V7X_LANES, (h + 1) * V7X_LANES)
                base = 2 * h * V7X_LANES
                q_ref[r, base:base + V7X_LANES] = (qn[:, sl] * scale).astype(BF16)
                q_ref[r, base + V7X_LANES:base + 2 * V7X_LANES] = (_rope(qpe[:, sl], cos, sin)
                                                                    * scale).astype(BF16)

        def mla_kv():
            kp_ref[r, :] = _rope(pe, cos, sin).astype(BF16)
            ckv = _rms(cmla[:, Q_LORA:], kvg_ref[...]).astype(BF16)
            kn_ref[r, :] = _dot(ckv, wkn_ref[...]).astype(BF16)
            va_ref[idx] = _dot_nt(wv_ref[...], ckv).astype(BF16)

        fillers = [out_gate, functools.partial(gate_half, 0), functools.partial(gate_half, 1),
                   mla_q, mla_kv]
        for blk in range(nblk):
            rb = slice(blk * hg_block, (blk + 1) * hg_block)
            ro = slice(r.start + rb.start, r.start + rb.stop)
            cb = c[rb]
            cm = cb[mid:mid + 1]
            cl = cb[hg_block - 1:hg_block]
            qi_ref[ro, :] = (q[rb] * jnp.exp(cb)).astype(BF16)
            kl_ref[ro, :] = (k[rb] * jnp.exp(cl - cb)).astype(BF16)
            dec_ref[idx * nblk + blk] = jnp.exp(cl)
            if intra_refs:
                worst = jnp.maximum(worst, jnp.maximum(cb[0:1] - cm, cm - cl))
                qt = (q[rb] * jnp.exp(cb - cm)).astype(BF16)
                kt = (k[rb] * jnp.exp(cm - cb)).astype(BF16)
                for p in range(HG_HEADS // 2):
                    ls = slice(p * pair_w, (p + 1) * pair_w)
                    kb = jnp.concatenate(
                        [jnp.concatenate([kt[:, ls][:, :HG_D], zero], axis=1),
                         jnp.concatenate([zero, kt[:, ls][:, HG_D:]], axis=1)], axis=0)
                    a = jnp.where(tri, _dot_nt(qt[:, ls], kb), 0.0)
                    a_ref[ro, p * 2 * hg_block:(p + 1) * 2 * hg_block] = a.astype(BF16)
            if fillers:
                fillers.pop(0)()
        while fillers:
            fillers.pop(0)()

    if intra_refs:
        @pl.when(jnp.max(worst) > HG_SAFE_EXPONENT)
        def _exact_intra():
            a_w = HG_HEADS * hg_block
            head_of_lane = lax.broadcasted_iota(jnp.int32, (d, HG_HEADS), 0) // HG_D
            head_sum = (head_of_lane
                        == lax.broadcasted_iota(jnp.int32, (d, HG_HEADS), 1)).astype(F32)
            col_base = lax.broadcasted_iota(jnp.int32, (HG_HEADS, a_w), 0) * hg_block
            col_lane = lax.broadcasted_iota(jnp.int32, (HG_HEADS, a_w), 1)
            causal = (lax.broadcasted_iota(jnp.int32, (hg_block, a_w), 0)
                      >= lax.broadcasted_iota(jnp.int32, (hg_block, a_w), 1) % hg_block)
            for r, (q, k, c) in zip(parts, stage2):
                ksc_ref[...] = k
                csc_ref[...] = c
                for blk in range(nblk):
                    rb = slice(blk * hg_block, (blk + 1) * hg_block)
                    first = r.start + rb.start
                    qb = q[rb]
                    cb = c[rb]

                    def key_column(s, acc):
                        row = rb.start + s
                        w = (qb * jnp.exp(jnp.minimum(cb - csc_ref[pl.ds(row, 1), :], 0.0))
                             * ksc_ref[pl.ds(row, 1), :])
                        cols = _dot(w, head_sum)
                        place = (col_lane == col_base + s).astype(F32)
                        return acc + _dot(cols, place)

                    acc = lax.fori_loop(0, hg_block, key_column,
                                        jnp.zeros((hg_block, a_w), F32))
                    a_ref[first:first + hg_block, :] = jnp.where(causal, acc, 0.0).astype(BF16)


def _inproj(x2d, rows, pos_blocks, params, cos_t, sin_t, with_intra):
    t, d = x2d.shape
    n = t // rows
    row = lambda w: pl.BlockSpec((rows, w), lambda i: (i, 0))
    pos = pl.BlockSpec((rows, V7X_LANES), lambda i: (i % pos_blocks, 0))
    w_specs = [_resident(p.shape) for p in params]
    hg_block = min(HG_BLOCK, rows)
    nb = rows // hg_block
    split = INPROJ_SPLIT if rows % (INPROJ_SPLIT * hg_block) == 0 else 1
    vt_w = MLA_HEADS * V_HEAD
    lead = lambda *shape: pl.BlockSpec((None,) + shape, lambda i: (i,) + (0,) * len(shape))
    outs = [(row(d), (t, d), BF16), (row(d), (t, d), BF16),
            (lead(nb, 1, d), (n, nb, 1, d), F32),
            (row(d), (t, d), BF16), (row(d), (t, d), BF16), (row(2 * d), (t, 2 * d), BF16),
            (row(2 * d), (t, 2 * d), BF16), (row(d), (t, d), BF16),
            (row(V7X_LANES), (t, V7X_LANES), BF16),
            (lead(split, vt_w, rows // split), (n, split, vt_w, rows // split), BF16)]
    scratch = []
    if with_intra:
        a_w = HG_HEADS * hg_block
        outs.append((row(a_w), (t, a_w), BF16))
        scratch = [pltpu.VMEM((rows // split, d), F32), pltpu.VMEM((rows // split, d), F32)]
    scale = (QK_NOPE + QK_ROPE) ** -0.5 * math.log2(math.e)
    return pl.pallas_call(
        functools.partial(_inproj_kernel, scale=scale, hg_block=hg_block, split=split),
        grid=(n,),
        in_specs=[row(d)] + w_specs + [pos, pos],
        out_specs=[o[0] for o in outs],
        out_shape=[jax.ShapeDtypeStruct(o[1], o[2]) for o in outs],
        scratch_shapes=scratch,
        compiler_params=pltpu.CompilerParams(
            dimension_semantics=("parallel",), vmem_limit_bytes=V7X_VMEM_LIMIT_BYTES),
        name="inproj",
    )(x2d, *params, cos_t, sin_t)


def _hgrn_kernel(a_ref, qi_ref, kl_ref, dec_ref, v_ref, og_ref, klm_ref, vm_ref, g_ref,
                 o_ref, st_ref):
    n = pl.program_id(1)
    c_blk = qi_ref.shape[0]
    hs = [slice(h * HG_D, (h + 1) * HG_D) for h in range(HG_HEADS)]

    @pl.when(n == 0)
    def _init():
        vm = vm_ref[...]
        klm = klm_ref[...]
        for h, sl in enumerate(hs):
            st_ref[h] = _dot_tn(vm[:, sl], klm[:, sl])

    g = g_ref[...]
    zero = jnp.zeros((HG_BLOCK, HG_D), BF16)
    for blk in range(c_blk // HG_BLOCK):
        r = slice(blk * HG_BLOCK, (blk + 1) * HG_BLOCK)
        a = a_ref[r, :]
        qi = qi_ref[r, :]
        kl = kl_ref[r, :]
        v = v_ref[r, :]
        dec = dec_ref[blk]
        outs = []
        for p in range(HG_HEADS // 2):
            ha, hb = hs[2 * p], hs[2 * p + 1]
            vb = jnp.concatenate([jnp.concatenate([v[:, ha], zero], axis=1),
                                  jnp.concatenate([zero, v[:, hb]], axis=1)], axis=0)
            o_pair = _dot(a[:, p * 2 * HG_BLOCK:(p + 1) * 2 * HG_BLOCK], vb)
            for e, sl in enumerate((ha, hb)):
                h = 2 * p + e
                st = st_ref[h]
                outs.append(o_pair[:, e * HG_D:(e + 1) * HG_D]
                            + _dot_nt(qi[:, sl], st.astype(BF16)))
                st_ref[h] = st * dec[:, sl] + _dot_tn(v[:, sl], kl[:, sl])
        for o, sl in zip(outs, hs):
            o_ref[r, sl] = (_rms(o, g) * og_ref[r, sl].astype(F32)).astype(BF16)


def _hgrn(a, qi, kl, dec, vh, og, klm, vm, g, batch, seq):
    w = qi.shape[-1]
    nstep = seq // HG_STEP
    sub = HG_STEP // HG_BLOCK
    tok = lambda width: pl.BlockSpec((None, HG_STEP, width), lambda b, n: (b, n, 0))
    meta = pl.BlockSpec((N_META, w), lambda b, n: (0, 0))
    r3 = lambda x: x.reshape(batch, seq, x.shape[-1])
    return pl.pallas_call(
        _hgrn_kernel,
        grid=(batch, nstep),
        in_specs=[tok(a.shape[-1]), tok(w), tok(w),
                  pl.BlockSpec((None, sub, 1, w), lambda b, n: (b, n, 0, 0)),
                  tok(w), tok(w), meta, meta,
                  pl.BlockSpec((1, HG_D), lambda b, n: (0, 0))],
        out_specs=tok(w),
        out_shape=jax.ShapeDtypeStruct((batch, seq, w), BF16),
        scratch_shapes=[pltpu.VMEM((HG_HEADS, HG_D, HG_D), F32)],
        compiler_params=pltpu.CompilerParams(
            dimension_semantics=("parallel", "arbitrary"),
            vmem_limit_bytes=V7X_VMEM_LIMIT_BYTES),
        name="hgrn2",
    )(r3(a), r3(qi), r3(kl), dec.reshape(batch, seq // HG_BLOCK, 1, w), r3(vh), r3(og), klm,
      vm, g)


def _attn_kernel(q_ref, kn_ref, kp_ref, vt_ref, knm_ref, kpm_ref, vtm_ref, o_ref,
                 m_ref, l_ref, acc_ref):
    i = pl.program_id(1)
    tq = q_ref.shape[0]
    hs = [slice(h * V7X_LANES, (h + 1) * V7X_LANES) for h in range(MLA_HEADS)]

    def q_of(h):
        return q_ref[:, 2 * h * V7X_LANES:2 * (h + 1) * V7X_LANES]

    def with_max(s):
        return s, jnp.max(s, axis=0, keepdims=True)

    def update(carry, scored, vt_list):
        m, l, acc = carry
        s_list = [s for s, _ in scored]
        m_new = m
        for _, s_max in scored:
            m_new = jnp.maximum(m_new, s_max)
        alpha = jnp.exp2(m - m_new)
        l = alpha * l
        acc = alpha * acc
        for s, vt in zip(s_list, vt_list):
            p = jnp.exp2(s - m_new)
            l = l + jnp.sum(p, axis=0, keepdims=True)
            acc = acc + _dot(vt, p.astype(BF16))
        return m_new, l, acc

    def skewed(score_fn, consume_fn):
        scores = []
        for h in range(MLA_HEADS + ATTN_SKEW):
            if h < MLA_HEADS:
                scores.append(score_fn(h))
            g = h - ATTN_SKEW
            if g >= 0:
                consume_fn(g, scores[g])

    for h in range(MLA_HEADS):
        m_ref[h] = jnp.full((1, tq), -jnp.inf, F32)
        l_ref[h] = jnp.zeros((1, tq), F32)
        acc_ref[h] = jnp.zeros((V_HEAD, tq), F32)

    def kv_step(blocks):
        starts = [pl.multiple_of(b * ATTN_K, ATTN_K) for b in blocks]
        kps = [kp_ref[pl.ds(ks, ATTN_K), :] for ks in starts]

        def score(h):
            return [with_max(_dot_nt(
                jnp.concatenate([kn_ref[pl.ds(ks, ATTN_K), hs[h]], kp], axis=1), q_of(h)))
                for ks, kp in zip(starts, kps)]

        def consume(g, s_list):
            m, l, acc = update((m_ref[g], l_ref[g], acc_ref[g]), s_list,
                               [vt_ref[b, hs[g], :] for b in blocks])
            m_ref[g] = m
            l_ref[g] = l
            acc_ref[g] = acc

        skewed(score, consume)

    def pair_step(j2, _):
        kv_step([2 * j2, 2 * j2 + 1])
        return 0

    lax.fori_loop(0, i // 2, pair_step, 0)

    def last_step(full_blocks):
        blocks = list(full_blocks) + [i]
        starts = [pl.multiple_of(b * ATTN_K, ATTN_K) for b in blocks]
        kps = [kp_ref[pl.ds(ks, ATTN_K), :] for ks in starts]
        kpm = kpm_ref[...]
        keep = (lax.broadcasted_iota(jnp.int32, (ATTN_K, tq), 0)
                <= lax.broadcasted_iota(jnp.int32, (ATTN_K, tq), 1))

        def score(h):
            s = [_dot_nt(jnp.concatenate([kn_ref[pl.ds(ks, ATTN_K), hs[h]], kp], axis=1),
                         q_of(h)) for ks, kp in zip(starts, kps)]
            s[-1] = jnp.where(keep, s[-1], -jnp.inf)
            km = jnp.concatenate([knm_ref[:, hs[h]], kpm], axis=1)
            return [with_max(t) for t in s + [_dot_nt(km, q_of(h))]]

        def finish(g, s_list):
            _, l, acc = update((m_ref[g], l_ref[g], acc_ref[g]), s_list,
                               [vt_ref[b, hs[g], :] for b in blocks] + [vtm_ref[hs[g], :]])
            o_ref[hs[g], :] = (acc / l).astype(BF16)

        skewed(score, finish)

    @pl.when(i % 2 == 1)
    def _odd_tail():
        last_step([i - 1])

    @pl.when(i % 2 == 0)
    def _even_tail():
        last_step([])


def _attention(q, kn, kp, vt, knm, kpm, vtm, batch, seq):
    assert ATTN_Q == ATTN_K == INPROJ_ROWS // INPROJ_SPLIT
    w = kn.shape[-1]
    nq = seq // ATTN_Q
    r3 = lambda a: a.reshape(batch, seq, a.shape[-1])
    qblk = pl.BlockSpec((None, ATTN_Q, 2 * w), lambda b, i: (b, i, 0))
    oblk = pl.BlockSpec((None, None, w, ATTN_Q), lambda b, i: (b, i, 0, 0))
    kfull = pl.BlockSpec((None, seq, w), lambda b, i: (b, 0, 0))
    kpfull = pl.BlockSpec((None, seq, V7X_LANES), lambda b, i: (b, 0, 0))
    vtfull = pl.BlockSpec((None, nq, w, ATTN_K), lambda b, i: (b, 0, 0, 0))
    mh = pl.BlockSpec((N_META, w), lambda b, i: (0, 0))
    mp = pl.BlockSpec((N_META, V7X_LANES), lambda b, i: (0, 0))
    mvt = pl.BlockSpec((w, N_META), lambda b, i: (0, 0))
    return pl.pallas_call(
        _attn_kernel,
        grid=(batch, nq),
        in_specs=[qblk, kfull, kpfull, vtfull, mh, mp, mvt],
        out_specs=oblk,
        out_shape=jax.ShapeDtypeStruct((batch, nq, w, ATTN_Q), BF16),
        scratch_shapes=[pltpu.VMEM((MLA_HEADS, 1, ATTN_Q), F32),
                        pltpu.VMEM((MLA_HEADS, 1, ATTN_Q), F32),
                        pltpu.VMEM((MLA_HEADS, V_HEAD, ATTN_Q), F32)],
        compiler_params=pltpu.CompilerParams(
            dimension_semantics=("parallel", "arbitrary"),
            vmem_limit_bytes=V7X_VMEM_LIMIT_BYTES),
        name="mla_attn",
    )(r3(q), r3(kn), r3(kp), vt.reshape(batch, nq, w, ATTN_K), knm, kpm,
      vtm.reshape(w, N_META))


def _out_kernel(x_ref, oh_ref, oat_ref, gates_ref, who_ref, wmo_ref, wout_ref, gmix_ref,
                gfpre_ref, wfin_ref, wfout_ref, gfpost_ref, y_ref):
    rows, d = x_ref.shape
    hidden = wfout_ref.shape[0]
    halves = [slice(i * rows // OUT_SPLIT, (i + 1) * rows // OUT_SPLIT) for i in range(OUT_SPLIT)]

    def merge(idx, r):
        ya = _dot(oh_ref[r, :], who_ref[...])
        yb = _dot_tn(oat_ref[idx], wmo_ref[...])
        gates = gates_ref[r, :].astype(F32)
        return (gates[:, :d] * ya + gates[:, d:] * yb).astype(BF16)

    def mix_residual(r, merged):
        h1 = x_ref[r, :] + _rms(_dot(merged, wout_ref[...]), gmix_ref[...])
        return h1, _rms(h1, gfpre_ref[...]).astype(BF16)

    def ffn_act(u):
        gu = _dot(u, wfin_ref[...])
        gt = gu[:, :hidden]
        return (gt * _sigmoid(gt) * gu[:, hidden:]).astype(BF16)

    merged = [merge(idx, r) for idx, r in enumerate(halves)]
    h1_u = [mix_residual(r, m) for r, m in zip(halves, merged)]
    acts = [ffn_act(u) for _, u in h1_u]
    for r, (h1, _), act in zip(halves, h1_u, acts):
        y_ref[r, :] = h1 + _rms(_dot(act, wfout_ref[...]), gfpost_ref[...])


def _out_block(x2d, oh, oa, gates, params):
    t, d = x2d.shape
    n = t // OUT_ROWS
    row = lambda w: pl.BlockSpec((OUT_ROWS, w), lambda i: (i, 0))
    return pl.pallas_call(
        _out_kernel,
        grid=(n,),
        in_specs=[row(d), row(d),
                  pl.BlockSpec((OUT_SPLIT, oa.shape[1], OUT_ROWS // OUT_SPLIT), lambda i: (i, 0, 0)),
                  row(2 * d)] + [_resident(p.shape) for p in params],
        out_specs=row(d),
        out_shape=jax.ShapeDtypeStruct((t, d), F32),
        compiler_params=pltpu.CompilerParams(
            dimension_semantics=("parallel",), vmem_limit_bytes=V7X_VMEM_LIMIT_BYTES),
        name="merge_ffn",
    )(x2d, oh, oa, gates, *params)


def _rope_tables(length):
    pos = jnp.arange(length, dtype=F32)
    inv_freq = 1.0 / (ROPE_THETA ** (jnp.arange(0, QK_ROPE, 2, dtype=F32) / QK_ROPE))
    ang = pos[:, None] * inv_freq[None, :]
    cos, sin = jnp.cos(ang), jnp.sin(ang)
    zero = jnp.zeros((length, V7X_LANES - QK_ROPE), F32)
    return (jnp.concatenate([cos, cos, zero], axis=1),
            jnp.concatenate([-sin, sin, zero], axis=1))


def _swap_halves(w):
    half = w.shape[-1] // 2
    return jnp.concatenate([w[..., half:], w[..., :half]], axis=-1)


def kernel(x, meta_tokens, w_in, b_gate, lb_logits, hg_norm_g, w_hg_o, q_a_norm_g, w_q_b,
           kv_a_norm_g, w_kv_b, w_mla_o, w_out, mix_pre_g, mix_post_g, ffn_pre_g, ffn_post_g,
           w_ffn_in, w_ffn_out):
    batch, seq, d = x.shape
    assert w_in.shape[0] == 1, "single-layer block"
    assert 2 * QK_ROPE == V7X_LANES, "rotary slab [t | rot(t)] must fill one lane tile"
    assert seq % INPROJ_ROWS == 0 and seq % ATTN_Q == 0 and seq % HG_STEP == 0
    assert INPROJ_ROWS % HG_BLOCK == 0 and HG_STEP % HG_BLOCK == 0
    assert (batch * seq) % OUT_ROWS == 0
    hgw = HG_HEADS * HG_D
    row = lambda a: a.reshape(1, -1).astype(F32)

    wi = w_in[0]
    o = 0
    parts = []
    for sz in (hgw, hgw, hgw, hgw, Q_LORA + KV_LORA, QK_ROPE, 2 * d):
        parts.append(wi[:, o:o + sz])
        o += sz
    whq, whf, whi, whg, wc, wkpe, wgate = parts
    wpe = jnp.concatenate([wkpe, _swap_halves(wkpe)], axis=1)
    wq = w_q_b[0].reshape(Q_LORA, MLA_HEADS, QK_NOPE + QK_ROPE)
    wqn = wq[:, :, :QK_NOPE].reshape(Q_LORA, MLA_HEADS * QK_NOPE)
    wq_pe = wq[:, :, QK_NOPE:]
    wqp = jnp.concatenate([wq_pe, _swap_halves(wq_pe)], axis=2).reshape(
        Q_LORA, MLA_HEADS * V7X_LANES)
    wkv = w_kv_b[0].reshape(KV_LORA, MLA_HEADS, QK_NOPE + V_HEAD)
    wkn = wkv[:, :, :QK_NOPE].reshape(KV_LORA, MLA_HEADS * QK_NOPE)
    wv = wkv[:, :, QK_NOPE:].reshape(KV_LORA, MLA_HEADS * V_HEAD).T
    bf = lambda a: a.astype(BF16)
    inproj_params = [row(mix_pre_g[0]), bf(whq), bf(whf), bf(whi), bf(whg), bf(wc), bf(wpe),
                     bf(wgate), row(b_gate[0]), lb_logits.astype(F32), row(q_a_norm_g[0]),
                     row(kv_a_norm_g[0]), bf(wqn), bf(wqp), bf(wkn), bf(wv)]
    cos_t, sin_t = _rope_tables(N_META + seq)

    m_out = _inproj(meta_tokens.astype(F32), N_META, 1, inproj_params,
                    cos_t[:N_META], sin_t[:N_META], with_intra=False)
    _, kl_m, _, vh_m, _, _, _, kn_m, kp_m, va_m = m_out

    x2d = x.reshape(batch * seq, d)
    (qi, kl, dec, vh, og, gates, q, kn, kp, va, a_hg) = _inproj(
        x2d, INPROJ_ROWS, seq // INPROJ_ROWS, inproj_params, cos_t[N_META:], sin_t[N_META:],
        with_intra=True)

    o_hg = _hgrn(a_hg, qi, kl, dec, vh, og, kl_m, vh_m, row(hg_norm_g[0]), batch, seq)
    o_at = _attention(q, kn, kp, va, kn_m, kp_m, va_m, batch, seq)

    out_params = [bf(w_hg_o[0]), bf(w_mla_o[0]), bf(w_out[0]), row(mix_post_g[0]),
                  row(ffn_pre_g[0]), bf(w_ffn_in[0]), bf(w_ffn_out[0]), row(ffn_post_g[0])]
    assert OUT_ROWS // OUT_SPLIT == ATTN_Q
    y = _out_block(x2d, o_hg.reshape(batch * seq, hgw),
                   o_at.reshape(batch * (seq // ATTN_Q), -1, ATTN_Q), gates, out_params)
    return y.reshape(batch, seq, d)
```

```python
import functools
import math

import jax
import jax.numpy as jnp
from jax import lax
from jax.experimental import pallas as pl
from jax.experimental.pallas import tpu as pltpu

N_META = 16
NORM_EPS = 1e-6
HG_HEADS = 8
HG_D = 128
MLA_HEADS = 8
QK_NOPE = 128
QK_ROPE = 64
V_HEAD = 128
Q_LORA = 256
KV_LORA = 256
ROPE_THETA = 10000.0

V7X_LANES = 128
V7X_VMEM_LIMIT_BYTES = 56 * 1024 * 1024

INPROJ_ROWS = 512
INPROJ_SPLIT = 2
HG_BLOCK = 64
HG_STEP = 1024
HG_SAFE_EXPONENT = 60.0
ATTN_Q = 256
ATTN_K = 256
ATTN_SKEW = 8
OUT_ROWS = 512
OUT_SPLIT = 2

F32 = jnp.float32
BF16 = jnp.bfloat16


def _dot(a, b):
    return jnp.dot(a, b, preferred_element_type=F32)


def _dot_nt(a, b):
    return lax.dot_general(a, b, (((1,), (1,)), ((), ())), preferred_element_type=F32)


def _dot_tn(a, b):
    return lax.dot_general(a, b, (((0,), (0,)), ((), ())), preferred_element_type=F32)


def _rms(x, g):
    ms = jnp.mean(x * x, axis=-1, keepdims=True)
    return x * lax.rsqrt(ms + NORM_EPS) * g


def _sigmoid(x):
    return 1.0 / (1.0 + jnp.exp(-x))


def _rope(slab, cos, sin):
    half = slab.shape[1] // 2
    return slab * cos + pltpu.roll(slab, half, 1) * sin


def _resident(shape):
    return pl.BlockSpec(shape, lambda *_: (0,) * len(shape), pipeline_mode=pl.Buffered(1))


def _cumsum_blocks(x, block):
    pos = lax.broadcasted_iota(jnp.int32, x.shape, 0) % block
    s = 1
    while s < block:
        x = x + jnp.where(pos >= s, pltpu.roll(x, s, 0), 0.0)
        s *= 2
    return x


def _inproj_kernel(x_ref, gpre_ref, whq_ref, whf_ref, whi_ref, whg_ref, wc_ref, wpe_ref,
                   wgate_ref, bgate_ref, lbl_ref, qg_ref, kvg_ref, wqn_ref, wqp_ref,
                   wkn_ref, wv_ref, cos_ref, sin_ref,
                   qi_ref, kl_ref, dec_ref, vh_ref, og_ref, gates_ref,
                   q_ref, kn_ref, kp_ref, va_ref, *intra_refs, scale, hg_block, split):
    rows, d = x_ref.shape
    part_rows = rows // split
    parts = [slice(i * part_rows, (i + 1) * part_rows) for i in range(split)]
    nblk = part_rows // hg_block
    mid = hg_block // 2 - 1
    pair_w = 2 * HG_D

    lbl = lbl_ref[...]
    e = jnp.exp(lbl - jnp.max(lbl, axis=0, keepdims=True))
    lb = e[0:1] / jnp.sum(e, axis=0, keepdims=True)

    stage1 = []
    for r in parts:
        u = _rms(x_ref[r, :], gpre_ref[...]).astype(BF16)
        hq = _dot(u, whq_ref[...])
        hf = _dot(u, whf_ref[...])
        vh_ref[r, :] = _dot(u, whi_ref[...]).astype(BF16)
        pe = _dot(u, wpe_ref[...])
        cmla = _dot(u, wc_ref[...])
        stage1.append((u, hq, hf, pe, cmla))

    stage2 = []
    for _, hq, hf, _, _ in stage1:
        q = hq * _sigmoid(hq)
        sg = _sigmoid(hf)
        k = (1.0 - lb) * (1.0 - sg)
        c = _cumsum_blocks(jnp.log(lb + (1.0 - lb) * sg), hg_block)
        stage2.append((q, k, c))

    zero = jnp.zeros((hg_block, HG_D), BF16)
    if intra_refs:
        a_ref, ksc_ref, csc_ref = intra_refs
        tri = (lax.broadcasted_iota(jnp.int32, (hg_block, 2 * hg_block), 0)
               >= lax.broadcasted_iota(jnp.int32, (hg_block, 2 * hg_block), 1) % hg_block)
        worst = jnp.zeros((1, d), F32)
    for idx, r in enumerate(parts):
        u, _, _, pe, cmla = stage1[idx]
        q, k, c = stage2[idx]
        cos = cos_ref[r, :]
        sin = sin_ref[r, :]

        def gate_half(half):
            cols = slice(half * d, (half + 1) * d)
            gates_ref[r, cols] = _sigmoid(
                _dot(u, wgate_ref[:, cols]) + bgate_ref[:, cols]).astype(BF16)

        def out_gate():
            hg = _dot(u, whg_ref[...])
            og_ref[r, :] = (hg * _sigmoid(hg)).astype(BF16)

        def mla_q():
            cq = _rms(cmla[:, :Q_LORA], qg_ref[...]).astype(BF16)
            qn = _dot(cq, wqn_ref[...])
            qpe = _dot(cq, wqp_ref[...])
            for h in range(MLA_HEADS):
                sl = slice(h * V7X_LANES, (h + 1) * V7X_LANES)
                base = 2 * h * V7X_LANES
                q_ref[r, base:base + V7X_LANES] = (qn[:, sl] * scale).astype(BF16)
                q_ref[r, base + V7X_LANES:base + 2 * V7X_LANES] = (_rope(qpe[:, sl], cos, sin)
                                                                    * scale).astype(BF16)

        def mla_kv():
            kp_ref[r, :] = _rope(pe, cos, sin).astype(BF16)
            ckv = _rms(cmla[:, Q_LORA:], kvg_ref[...]).astype(BF16)
            kn_ref[r, :] = _dot(ckv, wkn_ref[...]).astype(BF16)
            va_ref[idx] = _dot_nt(wv_ref[...], ckv).astype(BF16)

        fillers = [out_gate, functools.partial(gate_half, 0), functools.partial(gate_half, 1),
                   mla_q, mla_kv]
        for blk in range(nblk):
            rb = slice(blk * hg_block, (blk + 1) * hg_block)
            ro = slice(r.start + rb.start, r.start + rb.stop)
            cb = c[rb]
            cm = cb[mid:mid + 1]
            cl = cb[hg_block - 1:hg_block]
            qi_ref[ro, :] = (q[rb] * jnp.exp(cb)).astype(BF16)
            kl_ref[ro, :] = (k[rb] * jnp.exp(cl - cb)).astype(BF16)
            dec_ref[idx * nblk + blk] = jnp.exp(cl)
            if intra_refs:
                worst = jnp.maximum(worst, jnp.maximum(cb[0:1] - cm, cm - cl))
                qt = (q[rb] * jnp.exp(cb - cm)).astype(BF16)
                kt = (k[rb] * jnp.exp(cm - cb)).astype(BF16)
                for p in range(HG_HEADS // 2):
                    ls = slice(p * pair_w, (p + 1) * pair_w)
                    kb = jnp.concatenate(
                        [jnp.concatenate([kt[:, ls][:, :HG_D], zero], axis=1),
                         jnp.concatenate([zero, kt[:, ls][:, HG_D:]], axis=1)], axis=0)
                    a = jnp.where(tri, _dot_nt(qt[:, ls], kb), 0.0)
                    a_ref[ro, p * 2 * hg_block:(p + 1) * 2 * hg_block] = a.astype(BF16)
            if fillers:
                fillers.pop(0)()
        while fillers:
            fillers.pop(0)()

    if intra_refs:
        @pl.when(jnp.max(worst) > HG_SAFE_EXPONENT)
        def _exact_intra():
            a_w = HG_HEADS * hg_block
            head_of_lane = lax.broadcasted_iota(jnp.int32, (d, HG_HEADS), 0) // HG_D
            head_sum = (head_of_lane
                        == lax.broadcasted_iota(jnp.int32, (d, HG_HEADS), 1)).astype(F32)
            col_base = lax.broadcasted_iota(jnp.int32, (HG_HEADS, a_w), 0) * hg_block
            col_lane = lax.broadcasted_iota(jnp.int32, (HG_HEADS, a_w), 1)
            causal = (lax.broadcasted_iota(jnp.int32, (hg_block, a_w), 0)
                      >= lax.broadcasted_iota(jnp.int32, (hg_block, a_w), 1) % hg_block)
            for r, (q, k, c) in zip(parts, stage2):
                ksc_ref[...] = k
                csc_ref[...] = c
                for blk in range(nblk):
                    rb = slice(blk * hg_block, (blk + 1) * hg_block)
                    first = r.start + rb.start
                    qb = q[rb]
                    cb = c[rb]

                    def key_column(s, acc):
                        row = rb.start + s
                        w = (qb * jnp.exp(jnp.minimum(cb - csc_ref[pl.ds(row, 1), :], 0.0))
                             * ksc_ref[pl.ds(row, 1), :])
                        cols = _dot(w, head_sum)
                        place = (col_lane == col_base + s).astype(F32)
                        return acc + _dot(cols, place)

                    acc = lax.fori_loop(0, hg_block, key_column,
                                        jnp.zeros((hg_block, a_w), F32))
                    a_ref[first:first + hg_block, :] = jnp.where(causal, acc, 0.0).astype(BF16)


def _inproj(x2d, rows, pos_blocks, params, cos_t, sin_t, with_intra):
    t, d = x2d.shape
    n = t // rows
    row = lambda w: pl.BlockSpec((rows, w), lambda i: (i, 0))
    pos = pl.BlockSpec((rows, V7X_LANES), lambda i: (i % pos_blocks, 0))
    w_specs = [_resident(p.shape) for p in params]
    hg_block = min(HG_BLOCK, rows)
    nb = rows // hg_block
    split = INPROJ_SPLIT if rows % (INPROJ_SPLIT * hg_block) == 0 else 1
    vt_w = MLA_HEADS * V_HEAD
    lead = lambda *shape: pl.BlockSpec((None,) + shape, lambda i: (i,) + (0,) * len(shape))
    outs = [(row(d), (t, d), BF16), (row(d), (t, d), BF16),
            (lead(nb, 1, d), (n, nb, 1, d), F32),
            (row(d), (t, d), BF16), (row(d), (t, d), BF16), (row(2 * d), (t, 2 * d), BF16),
            (row(2 * d), (t, 2 * d), BF16), (row(d), (t, d), BF16),
            (row(V7X_LANES), (t, V7X_LANES), BF16),
            (lead(split, vt_w, rows // split), (n, split, vt_w, rows // split), BF16)]
    scratch = []
    if with_intra:
        a_w = HG_HEADS * hg_block
        outs.append((row(a_w), (t, a_w), BF16))
        scratch = [pltpu.VMEM((rows // split, d), F32), pltpu.VMEM((rows // split, d), F32)]
    scale = (QK_NOPE + QK_ROPE) ** -0.5 * math.log2(math.e)
    return pl.pallas_call(
        functools.partial(_inproj_kernel, scale=scale, hg_block=hg_block, split=split),
        grid=(n,),
        in_specs=[row(d)] + w_specs + [pos, pos],
        out_specs=[o[0] for o in outs],
        out_shape=[jax.ShapeDtypeStruct(o[1], o[2]) for o in outs],
        scratch_shapes=scratch,
        compiler_params=pltpu.CompilerParams(
            dimension_semantics=("parallel",), vmem_limit_bytes=V7X_VMEM_LIMIT_BYTES),
        name="inproj",
    )(x2d, *params, cos_t, sin_t)


def _hgrn_kernel(a_ref, qi_ref, kl_ref, dec_ref, v_ref, og_ref, klm_ref, vm_ref, g_ref,
                 o_ref, st_ref):
    n = pl.program_id(1)
    c_blk = qi_ref.shape[0]
    hs = [slice(h * HG_D, (h + 1) * HG_D) for h in range(HG_HEADS)]

    @pl.when(n == 0)
    def _init():
        vm = vm_ref[...]
        klm = klm_ref[...]
        for h, sl in enumerate(hs):
            st_ref[h] = _dot_tn(vm[:, sl], klm[:, sl])

    g = g_ref[...]
    zero = jnp.zeros((HG_BLOCK, HG_D), BF16)
    for blk in range(c_blk // HG_BLOCK):
        r = slice(blk * HG_BLOCK, (blk + 1) * HG_BLOCK)
        a = a_ref[r, :]
        qi = qi_ref[r, :]
        kl = kl_ref[r, :]
        v = v_ref[r, :]
        dec = dec_ref[blk]
        outs = []
        for p in range(HG_HEADS // 2):
            ha, hb = hs[2 * p], hs[2 * p + 1]
            vb = jnp.concatenate([jnp.concatenate([v[:, ha], zero], axis=1),
                                  jnp.concatenate([zero, v[:, hb]], axis=1)], axis=0)
            o_pair = _dot(a[:, p * 2 * HG_BLOCK:(p + 1) * 2 * HG_BLOCK], vb)
            for e, sl in enumerate((ha, hb)):
                h = 2 * p + e
                st = st_ref[h]
                outs.append(o_pair[:, e * HG_D:(e + 1) * HG_D]
                            + _dot_nt(qi[:, sl], st.astype(BF16)))
                st_ref[h] = st * dec[:, sl] + _dot_tn(v[:, sl], kl[:, sl])
        for o, sl in zip(outs, hs):
            o_ref[r, sl] = (_rms(o, g) * og_ref[r, sl].astype(F32)).astype(BF16)


def _hgrn(a, qi, kl, dec, vh, og, klm, vm, g, batch, seq):
    w = qi.shape[-1]
    nstep = seq // HG_STEP
    sub = HG_STEP // HG_BLOCK
    tok = lambda width: pl.BlockSpec((None, HG_STEP, width), lambda b, n: (b, n, 0))
    meta = pl.BlockSpec((N_META, w), lambda b, n: (0, 0))
    r3 = lambda x: x.reshape(batch, seq, x.shape[-1])
    return pl.pallas_call(
        _hgrn_kernel,
        grid=(batch, nstep),
        in_specs=[tok(a.shape[-1]), tok(w), tok(w),
                  pl.BlockSpec((None, sub, 1, w), lambda b, n: (b, n, 0, 0)),
                  tok(w), tok(w), meta, meta,
                  pl.BlockSpec((1, HG_D), lambda b, n: (0, 0))],
        out_specs=tok(w),
        out_shape=jax.ShapeDtypeStruct((batch, seq, w), BF16),
        scratch_shapes=[pltpu.VMEM((HG_HEADS, HG_D, HG_D), F32)],
        compiler_params=pltpu.CompilerParams(
            dimension_semantics=("parallel", "arbitrary"),
            vmem_limit_bytes=V7X_VMEM_LIMIT_BYTES),
        name="hgrn2",
    )(r3(a), r3(qi), r3(kl), dec.reshape(batch, seq // HG_BLOCK, 1, w), r3(vh), r3(og), klm,
      vm, g)


def _attn_kernel(q_ref, kn_ref, kp_ref, vt_ref, knm_ref, kpm_ref, vtm_ref, o_ref,
                 m_ref, l_ref, acc_ref):
    i = pl.program_id(1)
    tq = q_ref.shape[0]
    hs = [slice(h * V7X_LANES, (h + 1) * V7X_LANES) for h in range(MLA_HEADS)]

    def q_of(h):
        return q_ref[:, 2 * h * V7X_LANES:2 * (h + 1) * V7X_LANES]

    def with_max(s):
        return s, jnp.max(s, axis=0, keepdims=True)

    def update(carry, scored, vt_list):
        m, l, acc = carry
        s_list = [s for s, _ in scored]
        m_new = m
        for _, s_max in scored:
            m_new = jnp.maximum(m_new, s_max)
        alpha = jnp.exp2(m - m_new)
        l = alpha * l
        acc = alpha * acc
        for s, vt in zip(s_list, vt_list):
            p = jnp.exp2(s - m_new)
            l = l + jnp.sum(p, axis=0, keepdims=True)
            acc = acc + _dot(vt, p.astype(BF16))
        return m_new, l, acc

    def skewed(score_fn, consume_fn):
        scores = []
        for h in range(MLA_HEADS + ATTN_SKEW):
            if h < MLA_HEADS:
                scores.append(score_fn(h))
            g = h - ATTN_SKEW
            if g >= 0:
                consume_fn(g, scores[g])

    for h in range(MLA_HEADS):
        m_ref[h] = jnp.full((1, tq), -jnp.inf, F32)
        l_ref[h] = jnp.zeros((1, tq), F32)
        acc_ref[h] = jnp.zeros((V_HEAD, tq), F32)

    def kv_step(blocks):
        starts = [pl.multiple_of(b * ATTN_K, ATTN_K) for b in blocks]
        kps = [kp_ref[pl.ds(ks, ATTN_K), :] for ks in starts]

        def score(h):
            return [with_max(_dot_nt(
                jnp.concatenate([kn_ref[pl.ds(ks, ATTN_K), hs[h]], kp], axis=1), q_of(h)))
                for ks, kp in zip(starts, kps)]

        def consume(g, s_list):
            m, l, acc = update((m_ref[g], l_ref[g], acc_ref[g]), s_list,
                               [vt_ref[b, hs[g], :] for b in blocks])
            m_ref[g] = m
            l_ref[g] = l
            acc_ref[g] = acc

        skewed(score, consume)

    def pair_step(j2, _):
        kv_step([2 * j2, 2 * j2 + 1])
        return 0

    lax.fori_loop(0, i // 2, pair_step, 0)

    def last_step(full_blocks):
        blocks = list(full_blocks) + [i]
        starts = [pl.multiple_of(b * ATTN_K, ATTN_K) for b in blocks]
        kps = [kp_ref[pl.ds(ks, ATTN_K), :] for ks in starts]
        kpm = kpm_ref[...]
        keep = (lax.broadcasted_iota(jnp.int32, (ATTN_K, tq), 0)
                <= lax.broadcasted_iota(jnp.int32, (ATTN_K, tq), 1))

        def score(h):
            s = [_dot_nt(jnp.concatenate([kn_ref[pl.ds(ks, ATTN_K), hs[h]], kp], axis=1),
                         q_of(h)) for ks, kp in zip(starts, kps)]
            s[-1] = jnp.where(keep, s[-1], -jnp.inf)
            km = jnp.concatenate([knm_ref[:, hs[h]], kpm], axis=1)
            return [with_max(t) for t in s + [_dot_nt(km, q_of(h))]]

        def finish(g, s_list):
            _, l, acc = update((m_ref[g], l_ref[g], acc_ref[g]), s_list,
                               [vt_ref[b, hs[g], :] for b in blocks] + [vtm_ref[hs[g], :]])
            o_ref[hs[g], :] = (acc / l).astype(BF16)

        skewed(score, finish)

    @pl.when(i % 2 == 1)
    def _odd_tail():
        last_step([i - 1])

    @pl.when(i % 2 == 0)
    def _even_tail():
        last_step([])


def _attention(q, kn, kp, vt, knm, kpm, vtm, batch, seq):
    assert ATTN_Q == ATTN_K == INPROJ_ROWS // INPROJ_SPLIT
    w = kn.shape[-1]
    nq = seq // ATTN_Q
    r3 = lambda a: a.reshape(batch, seq, a.shape[-1])
    qblk = pl.BlockSpec((None, ATTN_Q, 2 * w), lambda b, i: (b, i, 0))
    oblk = pl.BlockSpec((None, None, w, ATTN_Q), lambda b, i: (b, i, 0, 0))
    kfull = pl.BlockSpec((None, seq, w), lambda b, i: (b, 0, 0))
    kpfull = pl.BlockSpec((None, seq, V7X_LANES), lambda b, i: (b, 0, 0))
    vtfull = pl.BlockSpec((None, nq, w, ATTN_K), lambda b, i: (b, 0, 0, 0))
    mh = pl.BlockSpec((N_META, w), lambda b, i: (0, 0))
    mp = pl.BlockSpec((N_META, V7X_LANES), lambda b, i: (0, 0))
    mvt = pl.BlockSpec((w, N_META), lambda b, i: (0, 0))
    return pl.pallas_call(
        _attn_kernel,
        grid=(batch, nq),
        in_specs=[qblk, kfull, kpfull, vtfull, mh, mp, mvt],
        out_specs=oblk,
        out_shape=jax.ShapeDtypeStruct((batch, nq, w, ATTN_Q), BF16),
        scratch_shapes=[pltpu.VMEM((MLA_HEADS, 1, ATTN_Q), F32),
                        pltpu.VMEM((MLA_HEADS, 1, ATTN_Q), F32),
                        pltpu.VMEM((MLA_HEADS, V_HEAD, ATTN_Q), F32)],
        compiler_params=pltpu.CompilerParams(
            dimension_semantics=("parallel", "arbitrary"),
            vmem_limit_bytes=V7X_VMEM_LIMIT_BYTES),
        name="mla_attn",
    )(r3(q), r3(kn), r3(kp), vt.reshape(batch, nq, w, ATTN_K), knm, kpm,
      vtm.reshape(w, N_META))


def _out_kernel(x_ref, oh_ref, oat_ref, gates_ref, who_ref, wmo_ref, wout_ref, gmix_ref,
                gfpre_ref, wfin_ref, wfout_ref, gfpost_ref, y_ref):
    rows, d = x_ref.shape
    hidden = wfout_ref.shape[0]
    halves = [slice(i * rows // OUT_SPLIT, (i + 1) * rows // OUT_SPLIT) for i in range(OUT_SPLIT)]

    def merge(idx, r):
        ya = _dot(oh_ref[r, :], who_ref[...])
        yb = _dot_tn(oat_ref[idx], wmo_ref[...])
        gates = gates_ref[r, :].astype(F32)
        return (gates[:, :d] * ya + gates[:, d:] * yb).astype(BF16)

    def mix_residual(r, merged):
        h1 = x_ref[r, :] + _rms(_dot(merged, wout_ref[...]), gmix_ref[...])
        return h1, _rms(h1, gfpre_ref[...]).astype(BF16)

    def ffn_act(u):
        gu = _dot(u, wfin_ref[...])
        gt = gu[:, :hidden]
        return (gt * _sigmoid(gt) * gu[:, hidden:]).astype(BF16)

    merged = [merge(idx, r) for idx, r in enumerate(halves)]
    h1_u = [mix_residual(r, m) for r, m in zip(halves, merged)]
    acts = [ffn_act(u) for _, u in h1_u]
    for r, (h1, _), act in zip(halves, h1_u, acts):
        y_ref[r, :] = h1 + _rms(_dot(act, wfout_ref[...]), gfpost_ref[...])


def _out_block(x2d, oh, oa, gates, params):
    t, d = x2d.shape
    n = t // OUT_ROWS
    row = lambda w: pl.BlockSpec((OUT_ROWS, w), lambda i: (i, 0))
    return pl.pallas_call(
        _out_kernel,
        grid=(n,),
        in_specs=[row(d), row(d),
                  pl.BlockSpec((OUT_SPLIT, oa.shape[1], OUT_ROWS // OUT_SPLIT), lambda i: (i, 0, 0)),
                  row(2 * d)] + [_resident(p.shape) for p in params],
        out_specs=row(d),
        out_shape=jax.ShapeDtypeStruct((t, d), F32),
        compiler_params=pltpu.CompilerParams(
            dimension_semantics=("parallel",), vmem_limit_bytes=V7X_VMEM_LIMIT_BYTES),
        name="merge_ffn",
    )(x2d, oh, oa, gates, *params)


def _rope_tables(length):
    pos = jnp.arange(length, dtype=F32)
    inv_freq = 1.0 / (ROPE_THETA ** (jnp.arange(0, QK_ROPE, 2, dtype=F32) / QK_ROPE))
    ang = pos[:, None] * inv_freq[None, :]
    cos, sin = jnp.cos(ang), jnp.sin(ang)
    zero = jnp.zeros((length, V7X_LANES - QK_ROPE), F32)
    return (jnp.concatenate([cos, cos, zero], axis=1),
            jnp.concatenate([-sin, sin, zero], axis=1))


def _swap_halves(w):
    half = w.shape[-1] // 2
    return jnp.concatenate([w[..., half:], w[..., :half]], axis=-1)


def kernel(x, meta_tokens, w_in, b_gate, lb_logits, hg_norm_g, w_hg_o, q_a_norm_g, w_q_b,
           kv_a_norm_g, w_kv_b, w_mla_o, w_out, mix_pre_g, mix_post_g, ffn_pre_g, ffn_post_g,
           w_ffn_in, w_ffn_out):
    batch, seq, d = x.shape
    assert w_in.shape[0] == 1, "single-layer block"
    assert 2 * QK_ROPE == V7X_LANES, "rotary slab [t | rot(t)] must fill one lane tile"
    assert seq % INPROJ_ROWS == 0 and seq % ATTN_Q == 0 and seq % HG_STEP == 0
    assert INPROJ_ROWS % HG_BLOCK == 0 and HG_STEP % HG_BLOCK == 0
    assert (batch * seq) % OUT_ROWS == 0
    hgw = HG_HEADS * HG_D
    row = lambda a: a.reshape(1, -1).astype(F32)

    wi = w_in[0]
    o = 0
    parts = []
    for sz in (hgw, hgw, hgw, hgw, Q_LORA + KV_LORA, QK_ROPE, 2 * d):
        parts.append(wi[:, o:o + sz])
        o += sz
    whq, whf, whi, whg, wc, wkpe, wgate = parts
    wpe = jnp.concatenate([wkpe, _swap_halves(wkpe)], axis=1)
    wq = w_q_b[0].reshape(Q_LORA, MLA_HEADS, QK_NOPE + QK_ROPE)
    wqn = wq[:, :, :QK_NOPE].reshape(Q_LORA, MLA_HEADS * QK_NOPE)
    wq_pe = wq[:, :, QK_NOPE:]
    wqp = jnp.concatenate([wq_pe, _swap_halves(wq_pe)], axis=2).reshape(
        Q_LORA, MLA_HEADS * V7X_LANES)
    wkv = w_kv_b[0].reshape(KV_LORA, MLA_HEADS, QK_NOPE + V_HEAD)
    wkn = wkv[:, :, :QK_NOPE].reshape(KV_LORA, MLA_HEADS * QK_NOPE)
    wv = wkv[:, :, QK_NOPE:].reshape(KV_LORA, MLA_HEADS * V_HEAD).T
    bf = lambda a: a.astype(BF16)
    inproj_params = [row(mix_pre_g[0]), bf(whq), bf(whf), bf(whi), bf(whg), bf(wc), bf(wpe),
                     bf(wgate), row(b_gate[0]), lb_logits.astype(F32), row(q_a_norm_g[0]),
                     row(kv_a_norm_g[0]), bf(wqn), bf(wqp), bf(wkn), bf(wv)]
    cos_t, sin_t = _rope_tables(N_META + seq)

    m_out = _inproj(meta_tokens.astype(F32), N_META, 1, inproj_params,
                    cos_t[:N_META], sin_t[:N_META], with_intra=False)
    _, kl_m, _, vh_m, _, _, _, kn_m, kp_m, va_m = m_out

    x2d = x.reshape(batch * seq, d)
    (qi, kl, dec, vh, og, gates, q, kn, kp, va, a_hg) = _inproj(
        x2d, INPROJ_ROWS, seq // INPROJ_ROWS, inproj_params, cos_t[N_META:], sin_t[N_META:],
        with_intra=True)

    o_hg = _hgrn(a_hg, qi, kl, dec, vh, og, kl_m, vh_m, row(hg_norm_g[0]), batch, seq)
    o_at = _attention(q, kn, kp, va, kn_m, kp_m, va_m, batch, seq)

    out_params = [bf(w_hg_o[0]), bf(w_mla_o[0]), bf(w_out[0]), row(mix_post_g[0]),
                  row(ffn_pre_g[0]), bf(w_ffn_in[0]), bf(w_ffn_out[0]), row(ffn_post_g[0])]
    assert OUT_ROWS // OUT_SPLIT == ATTN_Q
    y = _out_block(x2d, o_hg.reshape(batch * seq, hgw),
                   o_at.reshape(batch * (seq // ATTN_Q), -1, ATTN_Q), gates, out_params)
    return y.reshape(batch, seq, d)
```

```python
import functools
import math

import jax
import jax.numpy as jnp
from jax import lax
from jax.experimental import pallas as pl
from jax.experimental.pallas import tpu as pltpu

N_META = 16
NORM_EPS = 1e-6
HG_HEADS = 8
HG_D = 128
MLA_HEADS = 8
QK_NOPE = 128
QK_ROPE = 64
V_HEAD = 128
Q_LORA = 256
KV_LORA = 256
ROPE_THETA = 10000.0

V7X_LANES = 128
V7X_VMEM_LIMIT_BYTES = 56 * 1024 * 1024

INPROJ_ROWS = 512
INPROJ_SPLIT = 2
HG_BLOCK = 64
HG_SAFE_EXPONENT = 60.0
ATTN_Q = 256
ATTN_K = 256
ATTN_SKEW = 8
OUT_ROWS = 512
OUT_SPLIT = 2

F32 = jnp.float32
BF16 = jnp.bfloat16


def _dot(a, b):
    return jnp.dot(a, b, preferred_element_type=F32)


def _dot_nt(a, b):
    return lax.dot_general(a, b, (((1,), (1,)), ((), ())), preferred_element_type=F32)


def _dot_tn(a, b):
    return lax.dot_general(a, b, (((0,), (0,)), ((), ())), preferred_element_type=F32)


def _rms(x, g):
    ms = jnp.mean(x * x, axis=-1, keepdims=True)
    return x * lax.rsqrt(ms + NORM_EPS) * g


def _sigmoid(x):
    return 1.0 / (1.0 + jnp.exp(-x))


def _rope(slab, cos, sin):
    half = slab.shape[1] // 2
    return slab * cos + pltpu.roll(slab, half, 1) * sin


def _resident(shape):
    return pl.BlockSpec(shape, lambda *_: (0,) * len(shape), pipeline_mode=pl.Buffered(1))


def _cumsum_blocks(x, block):
    pos = lax.broadcasted_iota(jnp.int32, x.shape, 0) % block
    s = 1
    while s < block:
        x = x + jnp.where(pos >= s, pltpu.roll(x, s, 0), 0.0)
        s *= 2
    return x


def _inproj_kernel(x_ref, gpre_ref, whq_ref, whf_ref, whi_ref, whg_ref, wc_ref, wpe_ref,
                   wgate_ref, bgate_ref, lbl_ref, qg_ref, kvg_ref, wqn_ref, wqp_ref,
                   wkn_ref, wv_ref, cos_ref, sin_ref,
                   qi_ref, kl_ref, dec_ref, vh_ref, og_ref, gates_ref,
                   q_ref, kn_ref, kp_ref, va_ref, *intra_refs, scale, hg_block, split):
    rows, d = x_ref.shape
    part_rows = rows // split
    parts = [slice(i * part_rows, (i + 1) * part_rows) for i in range(split)]
    nblk = part_rows // hg_block
    mid = hg_block // 2 - 1
    pair_w = 2 * HG_D

    lbl = lbl_ref[...]
    e = jnp.exp(lbl - jnp.max(lbl, axis=0, keepdims=True))
    lb = e[0:1] / jnp.sum(e, axis=0, keepdims=True)

    stage1 = []
    for r in parts:
        u = _rms(x_ref[r, :], gpre_ref[...]).astype(BF16)
        hq = _dot(u, whq_ref[...])
        hf = _dot(u, whf_ref[...])
        vh_ref[r, :] = _dot(u, whi_ref[...]).astype(BF16)
        pe = _dot(u, wpe_ref[...])
        cmla = _dot(u, wc_ref[...])
        stage1.append((u, hq, hf, pe, cmla))

    stage2 = []
    for _, hq, hf, _, _ in stage1:
        q = hq * _sigmoid(hq)
        sg = _sigmoid(hf)
        k = (1.0 - lb) * (1.0 - sg)
        c = _cumsum_blocks(jnp.log(lb + (1.0 - lb) * sg), hg_block)
        stage2.append((q, k, c))

    zero = jnp.zeros((hg_block, HG_D), BF16)
    if intra_refs:
        a_ref, ksc_ref, csc_ref = intra_refs
        tri = (lax.broadcasted_iota(jnp.int32, (hg_block, 2 * hg_block), 0)
               >= lax.broadcasted_iota(jnp.int32, (hg_block, 2 * hg_block), 1) % hg_block)
        worst = jnp.zeros((1, d), F32)
    for idx, r in enumerate(parts):
        u, _, _, pe, cmla = stage1[idx]
        q, k, c = stage2[idx]
        cos = cos_ref[r, :]
        sin = sin_ref[r, :]

        def gate_half(half):
            cols = slice(half * d, (half + 1) * d)
            gates_ref[r, cols] = _sigmoid(
                _dot(u, wgate_ref[:, cols]) + bgate_ref[:, cols]).astype(BF16)

        def out_gate():
            hg = _dot(u, whg_ref[...])
            og_ref[r, :] = (hg * _sigmoid(hg)).astype(BF16)

        def mla_q():
            cq = _rms(cmla[:, :Q_LORA], qg_ref[...]).astype(BF16)
            qn = _dot(cq, wqn_ref[...])
            qpe = _dot(cq, wqp_ref[...])
            for h in range(MLA_HEADS):
                sl = slice(h * V7X_LANES, (h + 1) * V7X_LANES)
                base = 2 * h * V7X_LANES
                q_ref[r, base:base + V7X_LANES] = (qn[:, sl] * scale).astype(BF16)
                q_ref[r, base + V7X_LANES:base + 2 * V7X_LANES] = (_rope(qpe[:, sl], cos, sin)
                                                                    * scale).astype(BF16)

        def mla_kv():
            kp_ref[r, :] = _rope(pe, cos, sin).astype(BF16)
            ckv = _rms(cmla[:, Q_LORA:], kvg_ref[...]).astype(BF16)
            kn_ref[r, :] = _dot(ckv, wkn_ref[...]).astype(BF16)
            va_ref[idx] = _dot_nt(wv_ref[...], ckv).astype(BF16)

        fillers = [out_gate, functools.partial(gate_half, 0), functools.partial(gate_half, 1),
                   mla_q, mla_kv]
        for blk in range(nblk):
            rb = slice(blk * hg_block, (blk + 1) * hg_block)
            ro = slice(r.start + rb.start, r.start + rb.stop)
            cb = c[rb]
            cm = cb[mid:mid + 1]
            cl = cb[hg_block - 1:hg_block]
            qi_ref[ro, :] = (q[rb] * jnp.exp(cb)).astype(BF16)
            kl_ref[ro, :] = (k[rb] * jnp.exp(cl - cb)).astype(BF16)
            dec_ref[idx * nblk + blk] = jnp.exp(cl)
            if intra_refs:
                worst = jnp.maximum(worst, jnp.maximum(cb[0:1] - cm, cm - cl))
                qt = (q[rb] * jnp.exp(cb - cm)).astype(BF16)
                kt = (k[rb] * jnp.exp(cm - cb)).astype(BF16)
                for p in range(HG_HEADS // 2):
                    ls = slice(p * pair_w, (p + 1) * pair_w)
                    kb = jnp.concatenate(
                        [jnp.concatenate([kt[:, ls][:, :HG_D], zero], axis=1),
                         jnp.concatenate([zero, kt[:, ls][:, HG_D:]], axis=1)], axis=0)
                    a = jnp.where(tri, _dot_nt(qt[:, ls], kb), 0.0)
                    a_ref[ro, p * 2 * hg_block:(p + 1) * 2 * hg_block] = a.astype(BF16)
            if fillers:
                fillers.pop(0)()
        while fillers:
            fillers.pop(0)()

    if intra_refs:
        @pl.when(jnp.max(worst) > HG_SAFE_EXPONENT)
        def _exact_intra():
            a_w = HG_HEADS * hg_block
            head_of_lane = lax.broadcasted_iota(jnp.int32, (d, HG_HEADS), 0) // HG_D
            head_sum = (head_of_lane
                        == lax.broadcasted_iota(jnp.int32, (d, HG_HEADS), 1)).astype(F32)
            col_base = lax.broadcasted_iota(jnp.int32, (HG_HEADS, a_w), 0) * hg_block
            col_lane = lax.broadcasted_iota(jnp.int32, (HG_HEADS, a_w), 1)
            causal = (lax.broadcasted_iota(jnp.int32, (hg_block, a_w), 0)
                      >= lax.broadcasted_iota(jnp.int32, (hg_block, a_w), 1) % hg_block)
            for r, (q, k, c) in zip(parts, stage2):
                ksc_ref[...] = k
                csc_ref[...] = c
                for blk in range(nblk):
                    rb = slice(blk * hg_block, (blk + 1) * hg_block)
                    first = r.start + rb.start
                    qb = q[rb]
                    cb = c[rb]

                    def key_column(s, acc):
                        row = rb.start + s
                        w = (qb * jnp.exp(jnp.minimum(cb - csc_ref[pl.ds(row, 1), :], 0.0))
                             * ksc_ref[pl.ds(row, 1), :])
                        cols = _dot(w, head_sum)
                        place = (col_lane == col_base + s).astype(F32)
                        return acc + _dot(cols, place)

                    acc = lax.fori_loop(0, hg_block, key_column,
                                        jnp.zeros((hg_block, a_w), F32))
                    a_ref[first:first + hg_block, :] = jnp.where(causal, acc, 0.0).astype(BF16)


def _inproj(x2d, rows, pos_blocks, params, cos_t, sin_t, with_intra):
    t, d = x2d.shape
    n = t // rows
    row = lambda w: pl.BlockSpec((rows, w), lambda i: (i, 0))
    pos = pl.BlockSpec((rows, V7X_LANES), lambda i: (i % pos_blocks, 0))
    w_specs = [_resident(p.shape) for p in params]
    hg_block = min(HG_BLOCK, rows)
    nb = rows // hg_block
    split = INPROJ_SPLIT if rows % (INPROJ_SPLIT * hg_block) == 0 else 1
    vt_w = MLA_HEADS * V_HEAD
    lead = lambda *shape: pl.BlockSpec((None,) + shape, lambda i: (i,) + (0,) * len(shape))
    outs = [(row(d), (t, d), BF16), (row(d), (t, d), BF16),
            (lead(nb, 1, d), (n, nb, 1, d), F32),
            (row(d), (t, d), BF16), (row(d), (t, d), BF16), (row(2 * d), (t, 2 * d), BF16),
            (row(2 * d), (t, 2 * d), BF16), (row(d), (t, d), BF16),
            (row(V7X_LANES), (t, V7X_LANES), BF16),
            (lead(split, vt_w, rows // split), (n, split, vt_w, rows // split), BF16)]
    scratch = []
    if with_intra:
        a_w = HG_HEADS * hg_block
        outs.append((row(a_w), (t, a_w), BF16))
        scratch = [pltpu.VMEM((rows // split, d), F32), pltpu.VMEM((rows // split, d), F32)]
    scale = (QK_NOPE + QK_ROPE) ** -0.5 * math.log2(math.e)
    return pl.pallas_call(
        functools.partial(_inproj_kernel, scale=scale, hg_block=hg_block, split=split),
        grid=(n,),
        in_specs=[row(d)] + w_specs + [pos, pos],
        out_specs=[o[0] for o in outs],
        out_shape=[jax.ShapeDtypeStruct(o[1], o[2]) for o in outs],
        scratch_shapes=scratch,
        compiler_params=pltpu.CompilerParams(
            dimension_semantics=("parallel",), vmem_limit_bytes=V7X_VMEM_LIMIT_BYTES),
        name="inproj",
    )(x2d, *params, cos_t, sin_t)


def _attn_kernel(q_ref, kn_ref, kp_ref, vt_ref, knm_ref, kpm_ref, vtm_ref,
                 a_ref, qi_ref, kl_ref, dec_ref, vh_ref, og_ref, klm_ref, vhm_ref, ghg_ref,
                 o_ref, ohg_ref, m_ref, l_ref, acc_ref, st_ref):
    i = pl.program_id(1)
    tq = q_ref.shape[0]
    hs = [slice(h * V7X_LANES, (h + 1) * V7X_LANES) for h in range(MLA_HEADS)]
    gs = [slice(h * HG_D, (h + 1) * HG_D) for h in range(HG_HEADS)]

    @pl.when(i == 0)
    def _init_state():
        vm = vhm_ref[...]
        klm = klm_ref[...]
        for h, sl in enumerate(gs):
            st_ref[h] = _dot_tn(vm[:, sl], klm[:, sl])

    def recurrence_block(blk):
        r = slice(blk * HG_BLOCK, (blk + 1) * HG_BLOCK)
        a = a_ref[r, :]
        qi = qi_ref[r, :]
        kl = kl_ref[r, :]
        v = vh_ref[r, :]
        dec = dec_ref[blk]
        g = ghg_ref[...]
        zero = jnp.zeros((HG_BLOCK, HG_D), BF16)
        outs = []
        for p in range(HG_HEADS // 2):
            ha, hb = gs[2 * p], gs[2 * p + 1]
            vb = jnp.concatenate([jnp.concatenate([v[:, ha], zero], axis=1),
                                  jnp.concatenate([zero, v[:, hb]], axis=1)], axis=0)
            o_pair = _dot(a[:, p * 2 * HG_BLOCK:(p + 1) * 2 * HG_BLOCK], vb)
            for e, sl in enumerate((ha, hb)):
                h = 2 * p + e
                st = st_ref[h]
                outs.append(o_pair[:, e * HG_D:(e + 1) * HG_D]
                            + _dot_nt(qi[:, sl], st.astype(BF16)))
                st_ref[h] = st * dec[:, sl] + _dot_tn(v[:, sl], kl[:, sl])
        for o, sl in zip(outs, gs):
            ohg_ref[r, sl] = (_rms(o, g) * og_ref[r, sl].astype(F32)).astype(BF16)

    def q_of(h):
        return q_ref[:, 2 * h * V7X_LANES:2 * (h + 1) * V7X_LANES]

    def with_max(s):
        return s, jnp.max(s, axis=0, keepdims=True)

    def update(carry, scored, vt_list):
        m, l, acc = carry
        s_list = [s for s, _ in scored]
        m_new = m
        for _, s_max in scored:
            m_new = jnp.maximum(m_new, s_max)
        alpha = jnp.exp2(m - m_new)
        l = alpha * l
        acc = alpha * acc
        for s, vt in zip(s_list, vt_list):
            p = jnp.exp2(s - m_new)
            l = l + jnp.sum(p, axis=0, keepdims=True)
            acc = acc + _dot(vt, p.astype(BF16))
        return m_new, l, acc

    def skewed(score_fn, consume_fn, fillers=()):
        fillers = list(fillers)
        scores = []
        for h in range(MLA_HEADS + ATTN_SKEW):
            if h < MLA_HEADS:
                scores.append(score_fn(h))
                if fillers and h % 2 == 1:
                    fillers.pop(0)()
            g = h - ATTN_SKEW
            if g >= 0:
                consume_fn(g, scores[g])
        while fillers:
            fillers.pop(0)()

    for h in range(MLA_HEADS):
        m_ref[h] = jnp.full((1, tq), -jnp.inf, F32)
        l_ref[h] = jnp.zeros((1, tq), F32)
        acc_ref[h] = jnp.zeros((V_HEAD, tq), F32)

    def kv_step(blocks):
        starts = [pl.multiple_of(b * ATTN_K, ATTN_K) for b in blocks]
        kps = [kp_ref[pl.ds(ks, ATTN_K), :] for ks in starts]

        def score(h):
            return [with_max(_dot_nt(
                jnp.concatenate([kn_ref[pl.ds(ks, ATTN_K), hs[h]], kp], axis=1), q_of(h)))
                for ks, kp in zip(starts, kps)]

        def consume(g, s_list):
            m, l, acc = update((m_ref[g], l_ref[g], acc_ref[g]), s_list,
                               [vt_ref[b, hs[g], :] for b in blocks])
            m_ref[g] = m
            l_ref[g] = l
            acc_ref[g] = acc

        skewed(score, consume)

    def pair_step(j2, _):
        kv_step([2 * j2, 2 * j2 + 1])
        return 0

    lax.fori_loop(0, i // 2, pair_step, 0)

    def last_step(full_blocks):
        blocks = list(full_blocks) + [i]
        starts = [pl.multiple_of(b * ATTN_K, ATTN_K) for b in blocks]
        kps = [kp_ref[pl.ds(ks, ATTN_K), :] for ks in starts]
        kpm = kpm_ref[...]
        keep = (lax.broadcasted_iota(jnp.int32, (ATTN_K, tq), 0)
                <= lax.broadcasted_iota(jnp.int32, (ATTN_K, tq), 1))

        def score(h):
            s = [_dot_nt(jnp.concatenate([kn_ref[pl.ds(ks, ATTN_K), hs[h]], kp], axis=1),
                         q_of(h)) for ks, kp in zip(starts, kps)]
            s[-1] = jnp.where(keep, s[-1], -jnp.inf)
            km = jnp.concatenate([knm_ref[:, hs[h]], kpm], axis=1)
            return [with_max(t) for t in s + [_dot_nt(km, q_of(h))]]

        def finish(g, s_list):
            _, l, acc = update((m_ref[g], l_ref[g], acc_ref[g]), s_list,
                               [vt_ref[b, hs[g], :] for b in blocks] + [vtm_ref[hs[g], :]])
            o_ref[hs[g], :] = (acc / l).astype(BF16)

        skewed(score, finish, [functools.partial(recurrence_block, b)
                               for b in range(tq // HG_BLOCK)])

    @pl.when(i % 2 == 1)
    def _odd_tail():
        last_step([i - 1])

    @pl.when(i % 2 == 0)
    def _even_tail():
        last_step([])


def _attention(q, kn, kp, vt, knm, kpm, vtm, a, qi, kl, dec, vh, og, klm, vhm, ghg, batch, seq):
    assert ATTN_Q == ATTN_K == INPROJ_ROWS // INPROJ_SPLIT and ATTN_Q % HG_BLOCK == 0
    w = kn.shape[-1]
    nq = seq // ATTN_Q
    r3 = lambda a: a.reshape(batch, seq, a.shape[-1])
    qblk = pl.BlockSpec((None, ATTN_Q, 2 * w), lambda b, i: (b, i, 0))
    oblk = pl.BlockSpec((None, None, w, ATTN_Q), lambda b, i: (b, i, 0, 0))
    kfull = pl.BlockSpec((None, seq, w), lambda b, i: (b, 0, 0))
    kpfull = pl.BlockSpec((None, seq, V7X_LANES), lambda b, i: (b, 0, 0))
    vtfull = pl.BlockSpec((None, nq, w, ATTN_K), lambda b, i: (b, 0, 0, 0))
    mh = pl.BlockSpec((N_META, w), lambda b, i: (0, 0))
    mp = pl.BlockSpec((N_META, V7X_LANES), lambda b, i: (0, 0))
    mvt = pl.BlockSpec((w, N_META), lambda b, i: (0, 0))
    tok = lambda width: pl.BlockSpec((None, ATTN_Q, width), lambda b, i: (b, i, 0))
    sub = ATTN_Q // HG_BLOCK
    decblk = pl.BlockSpec((None, sub, 1, w), lambda b, i: (b, i, 0, 0))
    return pl.pallas_call(
        _attn_kernel,
        grid=(batch, nq),
        in_specs=[qblk, kfull, kpfull, vtfull, mh, mp, mvt,
                  tok(a.shape[-1]), tok(w), tok(w), decblk, tok(w), tok(w), mh, mh,
                  pl.BlockSpec((1, HG_D), lambda b, i: (0, 0))],
        out_specs=[oblk, tok(w)],
        out_shape=[jax.ShapeDtypeStruct((batch, nq, w, ATTN_Q), BF16),
                   jax.ShapeDtypeStruct((batch, seq, w), BF16)],
        scratch_shapes=[pltpu.VMEM((MLA_HEADS, 1, ATTN_Q), F32),
                        pltpu.VMEM((MLA_HEADS, 1, ATTN_Q), F32),
                        pltpu.VMEM((MLA_HEADS, V_HEAD, ATTN_Q), F32),
                        pltpu.VMEM((HG_HEADS, HG_D, HG_D), F32)],
        compiler_params=pltpu.CompilerParams(
            dimension_semantics=("parallel", "arbitrary"),
            vmem_limit_bytes=V7X_VMEM_LIMIT_BYTES),
        name="mla_attn",
    )(r3(q), r3(kn), r3(kp), vt.reshape(batch, nq, w, ATTN_K), knm, kpm,
      vtm.reshape(w, N_META), r3(a), r3(qi), r3(kl),
      dec.reshape(batch, seq // HG_BLOCK, 1, w), r3(vh), r3(og), klm, vhm, ghg)


def _out_kernel(x_ref, oh_ref, oat_ref, gates_ref, who_ref, wmo_ref, wout_ref, gmix_ref,
                gfpre_ref, wfin_ref, wfout_ref, gfpost_ref, y_ref):
    rows, d = x_ref.shape
    hidden = wfout_ref.shape[0]
    halves = [slice(i * rows // OUT_SPLIT, (i + 1) * rows // OUT_SPLIT) for i in range(OUT_SPLIT)]

    def merge(idx, r):
        ya = _dot(oh_ref[r, :], who_ref[...])
        yb = _dot_tn(oat_ref[idx], wmo_ref[...])
        gates = gates_ref[r, :].astype(F32)
        return (gates[:, :d] * ya + gates[:, d:] * yb).astype(BF16)

    def mix_residual(r, merged):
        h1 = x_ref[r, :] + _rms(_dot(merged, wout_ref[...]), gmix_ref[...])
        return h1, _rms(h1, gfpre_ref[...]).astype(BF16)

    def ffn_act(u):
        gu = _dot(u, wfin_ref[...])
        gt = gu[:, :hidden]
        return (gt * _sigmoid(gt) * gu[:, hidden:]).astype(BF16)

    merged = [merge(idx, r) for idx, r in enumerate(halves)]
    h1_u = [mix_residual(r, m) for r, m in zip(halves, merged)]
    acts = [ffn_act(u) for _, u in h1_u]
    for r, (h1, _), act in zip(halves, h1_u, acts):
        y_ref[r, :] = h1 + _rms(_dot(act, wfout_ref[...]), gfpost_ref[...])


def _out_block(x2d, oh, oa, gates, params):
    t, d = x2d.shape
    n = t // OUT_ROWS
    row = lambda w: pl.BlockSpec((OUT_ROWS, w), lambda i: (i, 0))
    return pl.pallas_call(
        _out_kernel,
        grid=(n,),
        in_specs=[row(d), row(d),
                  pl.BlockSpec((OUT_SPLIT, oa.shape[1], OUT_ROWS // OUT_SPLIT), lambda i: (i, 0, 0)),
                  row(2 * d)] + [_resident(p.shape) for p in params],
        out_specs=row(d),
        out_shape=jax.ShapeDtypeStruct((t, d), F32),
        compiler_params=pltpu.CompilerParams(
            dimension_semantics=("parallel",), vmem_limit_bytes=V7X_VMEM_LIMIT_BYTES),
        name="merge_ffn",
    )(x2d, oh, oa, gates, *params)


def _rope_tables(length):
    pos = jnp.arange(length, dtype=F32)
    inv_freq = 1.0 / (ROPE_THETA ** (jnp.arange(0, QK_ROPE, 2, dtype=F32) / QK_ROPE))
    ang = pos[:, None] * inv_freq[None, :]
    cos, sin = jnp.cos(ang), jnp.sin(ang)
    zero = jnp.zeros((length, V7X_LANES - QK_ROPE), F32)
    return (jnp.concatenate([cos, cos, zero], axis=1),
            jnp.concatenate([-sin, sin, zero], axis=1))


def _swap_halves(w):
    half = w.shape[-1] // 2
    return jnp.concatenate([w[..., half:], w[..., :half]], axis=-1)


def kernel(x, meta_tokens, w_in, b_gate, lb_logits, hg_norm_g, w_hg_o, q_a_norm_g, w_q_b,
           kv_a_norm_g, w_kv_b, w_mla_o, w_out, mix_pre_g, mix_post_g, ffn_pre_g, ffn_post_g,
           w_ffn_in, w_ffn_out):
    batch, seq, d = x.shape
    assert w_in.shape[0] == 1, "single-layer block"
    assert 2 * QK_ROPE == V7X_LANES, "rotary slab [t | rot(t)] must fill one lane tile"
    assert seq % INPROJ_ROWS == 0 and seq % ATTN_Q == 0 and INPROJ_ROWS % HG_BLOCK == 0
    assert (batch * seq) % OUT_ROWS == 0
    hgw = HG_HEADS * HG_D
    row = lambda a: a.reshape(1, -1).astype(F32)

    wi = w_in[0]
    o = 0
    parts = []
    for sz in (hgw, hgw, hgw, hgw, Q_LORA + KV_LORA, QK_ROPE, 2 * d):
        parts.append(wi[:, o:o + sz])
        o += sz
    whq, whf, whi, whg, wc, wkpe, wgate = parts
    wpe = jnp.concatenate([wkpe, _swap_halves(wkpe)], axis=1)
    wq = w_q_b[0].reshape(Q_LORA, MLA_HEADS, QK_NOPE + QK_ROPE)
    wqn = wq[:, :, :QK_NOPE].reshape(Q_LORA, MLA_HEADS * QK_NOPE)
    wq_pe = wq[:, :, QK_NOPE:]
    wqp = jnp.concatenate([wq_pe, _swap_halves(wq_pe)], axis=2).reshape(
        Q_LORA, MLA_HEADS * V7X_LANES)
    wkv = w_kv_b[0].reshape(KV_LORA, MLA_HEADS, QK_NOPE + V_HEAD)
    wkn = wkv[:, :, :QK_NOPE].reshape(KV_LORA, MLA_HEADS * QK_NOPE)
    wv = wkv[:, :, QK_NOPE:].reshape(KV_LORA, MLA_HEADS * V_HEAD).T
    bf = lambda a: a.astype(BF16)
    inproj_params = [row(mix_pre_g[0]), bf(whq), bf(whf), bf(whi), bf(whg), bf(wc), bf(wpe),
                     bf(wgate), row(b_gate[0]), lb_logits.astype(F32), row(q_a_norm_g[0]),
                     row(kv_a_norm_g[0]), bf(wqn), bf(wqp), bf(wkn), bf(wv)]
    cos_t, sin_t = _rope_tables(N_META + seq)

    m_out = _inproj(meta_tokens.astype(F32), N_META, 1, inproj_params,
                    cos_t[:N_META], sin_t[:N_META], with_intra=False)
    _, kl_m, _, vh_m, _, _, _, kn_m, kp_m, va_m = m_out

    x2d = x.reshape(batch * seq, d)
    (qi, kl, dec, vh, og, gates, q, kn, kp, va, a_hg) = _inproj(
        x2d, INPROJ_ROWS, seq // INPROJ_ROWS, inproj_params, cos_t[N_META:], sin_t[N_META:],
        with_intra=True)

    o_at, o_hg = _attention(q, kn, kp, va, kn_m, kp_m, va_m, a_hg, qi, kl, dec, vh, og,
                            kl_m, vh_m, row(hg_norm_g[0]), batch, seq)

    out_params = [bf(w_hg_o[0]), bf(w_mla_o[0]), bf(w_out[0]), row(mix_post_g[0]),
                  row(ffn_pre_g[0]), bf(w_ffn_in[0]), bf(w_ffn_out[0]), row(ffn_post_g[0])]
    assert OUT_ROWS // OUT_SPLIT == ATTN_Q
    y = _out_block(x2d, o_hg.reshape(batch * seq, hgw),
                   o_at.reshape(batch * (seq // ATTN_Q), -1, ATTN_Q), gates, out_params)
    return y.reshape(batch, seq, d)
```

```python
import functools
import math

import jax
import jax.numpy as jnp
from jax import lax
from jax.experimental import pallas as pl
from jax.experimental.pallas import tpu as pltpu

N_META = 16
NORM_EPS = 1e-6
HG_HEADS = 8
HG_D = 128
MLA_HEADS = 8
QK_NOPE = 128
QK_ROPE = 64
V_HEAD = 128
Q_LORA = 256
KV_LORA = 256
ROPE_THETA = 10000.0

V7X_LANES = 128
V7X_VMEM_LIMIT_BYTES = 56 * 1024 * 1024

INPROJ_ROWS = 512
INPROJ_SPLIT = 2
HG_BLOCK = 64
HG_STEP = 1024
HG_SAFE_EXPONENT = 60.0
ATTN_Q = 256
ATTN_K = 256
ATTN_GROUP = 2
ATTN_SKEW = 8
OUT_ROWS = 512
OUT_SPLIT = 2

F32 = jnp.float32
BF16 = jnp.bfloat16


def _dot(a, b):
    return jnp.dot(a, b, preferred_element_type=F32)


def _dot_nt(a, b):
    return lax.dot_general(a, b, (((1,), (1,)), ((), ())), preferred_element_type=F32)


def _dot_tn(a, b):
    return lax.dot_general(a, b, (((0,), (0,)), ((), ())), preferred_element_type=F32)


def _rms(x, g):
    ms = jnp.mean(x * x, axis=-1, keepdims=True)
    return x * lax.rsqrt(ms + NORM_EPS) * g


def _sigmoid(x):
    return 1.0 / (1.0 + jnp.exp(-x))


def _rope(slab, cos, sin):
    half = slab.shape[1] // 2
    return slab * cos + pltpu.roll(slab, half, 1) * sin


def _resident(shape):
    return pl.BlockSpec(shape, lambda *_: (0,) * len(shape), pipeline_mode=pl.Buffered(1))


def _cumsum_blocks(x, block):
    pos = lax.broadcasted_iota(jnp.int32, x.shape, 0) % block
    s = 1
    while s < block:
        x = x + jnp.where(pos >= s, pltpu.roll(x, s, 0), 0.0)
        s *= 2
    return x


def _inproj_kernel(x_ref, gpre_ref, whq_ref, whf_ref, whi_ref, whg_ref, wc_ref, wpe_ref,
                   wgate_ref, bgate_ref, lbl_ref, qg_ref, kvg_ref, wqn_ref, wqp_ref,
                   wkn_ref, wv_ref, cos_ref, sin_ref,
                   qi_ref, kl_ref, dec_ref, vh_ref, og_ref, gates_ref,
                   q_ref, kn_ref, kp_ref, va_ref, *intra_refs, scale, hg_block, split):
    rows, d = x_ref.shape
    part_rows = rows // split
    parts = [slice(i * part_rows, (i + 1) * part_rows) for i in range(split)]
    nblk = part_rows // hg_block
    mid = hg_block // 2 - 1
    pair_w = 2 * HG_D

    lbl = lbl_ref[...]
    e = jnp.exp(lbl - jnp.max(lbl, axis=0, keepdims=True))
    lb = e[0:1] / jnp.sum(e, axis=0, keepdims=True)

    stage1 = []
    for r in parts:
        u = _rms(x_ref[r, :], gpre_ref[...]).astype(BF16)
        hq = _dot(u, whq_ref[...])
        hf = _dot(u, whf_ref[...])
        vh_ref[r, :] = _dot(u, whi_ref[...]).astype(BF16)
        pe = _dot(u, wpe_ref[...])
        cmla = _dot(u, wc_ref[...])
        stage1.append((u, hq, hf, pe, cmla))

    stage2 = []
    for _, hq, hf, _, _ in stage1:
        q = hq * _sigmoid(hq)
        sg = _sigmoid(hf)
        k = (1.0 - lb) * (1.0 - sg)
        c = _cumsum_blocks(jnp.log(lb + (1.0 - lb) * sg), hg_block)
        stage2.append((q, k, c))

    zero = jnp.zeros((hg_block, HG_D), BF16)
    if intra_refs:
        a_ref, ksc_ref, csc_ref = intra_refs
        tri = (lax.broadcasted_iota(jnp.int32, (hg_block, 2 * hg_block), 0)
               >= lax.broadcasted_iota(jnp.int32, (hg_block, 2 * hg_block), 1) % hg_block)
        worst = jnp.zeros((1, d), F32)
    for idx, r in enumerate(parts):
        u, _, _, pe, cmla = stage1[idx]
        q, k, c = stage2[idx]
        cos = cos_ref[r, :]
        sin = sin_ref[r, :]

        def gate_half(half):
            cols = slice(half * d, (half + 1) * d)
            gates_ref[r, cols] = _sigmoid(
                _dot(u, wgate_ref[:, cols]) + bgate_ref[:, cols]).astype(BF16)

        def out_gate():
            hg = _dot(u, whg_ref[...])
            og_ref[r, :] = (hg * _sigmoid(hg)).astype(BF16)

        def mla_q():
            cq = _rms(cmla[:, :Q_LORA], qg_ref[...]).astype(BF16)
            qn = _dot(cq, wqn_ref[...])
            qpe = _dot(cq, wqp_ref[...])
            for h in range(MLA_HEADS):
                sl = slice(h * V7X_LANES, (h + 1) * V7X_LANES)
                base = 2 * h * V7X_LANES
                q_ref[r, base:base + V7X_LANES] = (qn[:, sl] * scale).astype(BF16)
                q_ref[r, base + V7X_LANES:base + 2 * V7X_LANES] = (_rope(qpe[:, sl], cos, sin)
                                                                    * scale).astype(BF16)

        def mla_kv():
            kp_ref[r, :] = _rope(pe, cos, sin).astype(BF16)
            ckv = _rms(cmla[:, Q_LORA:], kvg_ref[...]).astype(BF16)
            kn_ref[r, :] = _dot(ckv, wkn_ref[...]).astype(BF16)
            va_ref[idx] = _dot_nt(wv_ref[...], ckv).astype(BF16)

        fillers = [out_gate, functools.partial(gate_half, 0), functools.partial(gate_half, 1),
                   mla_q, mla_kv]
        for blk in range(nblk):
            rb = slice(blk * hg_block, (blk + 1) * hg_block)
            ro = slice(r.start + rb.start, r.start + rb.stop)
            cb = c[rb]
            cm = cb[mid:mid + 1]
            cl = cb[hg_block - 1:hg_block]
            qi_ref[ro, :] = (q[rb] * jnp.exp(cb)).astype(BF16)
            kl_ref[ro, :] = (k[rb] * jnp.exp(cl - cb)).astype(BF16)
            dec_ref[idx * nblk + blk] = jnp.exp(cl)
            if intra_refs:
                worst = jnp.maximum(worst, jnp.maximum(cb[0:1] - cm, cm - cl))
                qt = (q[rb] * jnp.exp(cb - cm)).astype(BF16)
                kt = (k[rb] * jnp.exp(cm - cb)).astype(BF16)
                for p in range(HG_HEADS // 2):
                    ls = slice(p * pair_w, (p + 1) * pair_w)
                    kb = jnp.concatenate(
                        [jnp.concatenate([kt[:, ls][:, :HG_D], zero], axis=1),
                         jnp.concatenate([zero, kt[:, ls][:, HG_D:]], axis=1)], axis=0)
                    a = jnp.where(tri, _dot_nt(qt[:, ls], kb), 0.0)
                    a_ref[ro, p * 2 * hg_block:(p + 1) * 2 * hg_block] = a.astype(BF16)
            if fillers:
                fillers.pop(0)()
        while fillers:
            fillers.pop(0)()

    if intra_refs:
        @pl.when(jnp.max(worst) > HG_SAFE_EXPONENT)
        def _exact_intra():
            a_w = HG_HEADS * hg_block
            head_of_lane = lax.broadcasted_iota(jnp.int32, (d, HG_HEADS), 0) // HG_D
            head_sum = (head_of_lane
                        == lax.broadcasted_iota(jnp.int32, (d, HG_HEADS), 1)).astype(F32)
            col_base = lax.broadcasted_iota(jnp.int32, (HG_HEADS, a_w), 0) * hg_block
            col_lane = lax.broadcasted_iota(jnp.int32, (HG_HEADS, a_w), 1)
            causal = (lax.broadcasted_iota(jnp.int32, (hg_block, a_w), 0)
                      >= lax.broadcasted_iota(jnp.int32, (hg_block, a_w), 1) % hg_block)
            for r, (q, k, c) in zip(parts, stage2):
                ksc_ref[...] = k
                csc_ref[...] = c
                for blk in range(nblk):
                    rb = slice(blk * hg_block, (blk + 1) * hg_block)
                    first = r.start + rb.start
                    qb = q[rb]
                    cb = c[rb]

                    def key_column(s, acc):
                        row = rb.start + s
                        w = (qb * jnp.exp(jnp.minimum(cb - csc_ref[pl.ds(row, 1), :], 0.0))
                             * ksc_ref[pl.ds(row, 1), :])
                        cols = _dot(w, head_sum)
                        place = (col_lane == col_base + s).astype(F32)
                        return acc + _dot(cols, place)

                    acc = lax.fori_loop(0, hg_block, key_column,
                                        jnp.zeros((hg_block, a_w), F32))
                    a_ref[first:first + hg_block, :] = jnp.where(causal, acc, 0.0).astype(BF16)


def _inproj(x2d, rows, pos_blocks, params, cos_t, sin_t, with_intra):
    t, d = x2d.shape
    n = t // rows
    row = lambda w: pl.BlockSpec((rows, w), lambda i: (i, 0))
    pos = pl.BlockSpec((rows, V7X_LANES), lambda i: (i % pos_blocks, 0))
    w_specs = [_resident(p.shape) for p in params]
    hg_block = min(HG_BLOCK, rows)
    nb = rows // hg_block
    split = INPROJ_SPLIT if rows % (INPROJ_SPLIT * hg_block) == 0 else 1
    vt_w = MLA_HEADS * V_HEAD
    lead = lambda *shape: pl.BlockSpec((None,) + shape, lambda i: (i,) + (0,) * len(shape))
    outs = [(row(d), (t, d), BF16), (row(d), (t, d), BF16),
            (lead(nb, 1, d), (n, nb, 1, d), F32),
            (row(d), (t, d), BF16), (row(d), (t, d), BF16), (row(2 * d), (t, 2 * d), BF16),
            (row(2 * d), (t, 2 * d), BF16), (row(d), (t, d), BF16),
            (row(V7X_LANES), (t, V7X_LANES), BF16),
            (lead(split, vt_w, rows // split), (n, split, vt_w, rows // split), BF16)]
    scratch = []
    if with_intra:
        a_w = HG_HEADS * hg_block
        outs.append((row(a_w), (t, a_w), BF16))
        scratch = [pltpu.VMEM((rows // split, d), F32), pltpu.VMEM((rows // split, d), F32)]
    scale = (QK_NOPE + QK_ROPE) ** -0.5 * math.log2(math.e)
    return pl.pallas_call(
        functools.partial(_inproj_kernel, scale=scale, hg_block=hg_block, split=split),
        grid=(n,),
        in_specs=[row(d)] + w_specs + [pos, pos],
        out_specs=[o[0] for o in outs],
        out_shape=[jax.ShapeDtypeStruct(o[1], o[2]) for o in outs],
        scratch_shapes=scratch,
        compiler_params=pltpu.CompilerParams(
            dimension_semantics=("parallel",), vmem_limit_bytes=V7X_VMEM_LIMIT_BYTES),
        name="inproj",
    )(x2d, *params, cos_t, sin_t)


def _hgrn_kernel(a_ref, qi_ref, kl_ref, dec_ref, v_ref, og_ref, klm_ref, vm_ref, g_ref,
                 o_ref, st_ref):
    n = pl.program_id(1)
    c_blk = qi_ref.shape[0]
    hs = [slice(h * HG_D, (h + 1) * HG_D) for h in range(HG_HEADS)]

    @pl.when(n == 0)
    def _init():
        vm = vm_ref[...]
        klm = klm_ref[...]
        for h, sl in enumerate(hs):
            st_ref[h] = _dot_tn(vm[:, sl], klm[:, sl])

    g = g_ref[...]
    zero = jnp.zeros((HG_BLOCK, HG_D), BF16)
    for blk in range(c_blk // HG_BLOCK):
        r = slice(blk * HG_BLOCK, (blk + 1) * HG_BLOCK)
        a = a_ref[r, :]
        qi = qi_ref[r, :]
        kl = kl_ref[r, :]
        v = v_ref[r, :]
        dec = dec_ref[blk]
        outs = []
        for p in range(HG_HEADS // 2):
            ha, hb = hs[2 * p], hs[2 * p + 1]
            vb = jnp.concatenate([jnp.concatenate([v[:, ha], zero], axis=1),
                                  jnp.concatenate([zero, v[:, hb]], axis=1)], axis=0)
            o_pair = _dot(a[:, p * 2 * HG_BLOCK:(p + 1) * 2 * HG_BLOCK], vb)
            for e, sl in enumerate((ha, hb)):
                h = 2 * p + e
                st = st_ref[h]
                outs.append(o_pair[:, e * HG_D:(e + 1) * HG_D]
                            + _dot_nt(qi[:, sl], st.astype(BF16)))
                st_ref[h] = st * dec[:, sl] + _dot_tn(v[:, sl], kl[:, sl])
        for o, sl in zip(outs, hs):
            o_ref[r, sl] = (_rms(o, g) * og_ref[r, sl].astype(F32)).astype(BF16)


def _hgrn(a, qi, kl, dec, vh, og, klm, vm, g, batch, seq):
    w = qi.shape[-1]
    nstep = seq // HG_STEP
    sub = HG_STEP // HG_BLOCK
    tok = lambda width: pl.BlockSpec((None, HG_STEP, width), lambda b, n: (b, n, 0))
    meta = pl.BlockSpec((N_META, w), lambda b, n: (0, 0))
    r3 = lambda x: x.reshape(batch, seq, x.shape[-1])
    return pl.pallas_call(
        _hgrn_kernel,
        grid=(batch, nstep),
        in_specs=[tok(a.shape[-1]), tok(w), tok(w),
                  pl.BlockSpec((None, sub, 1, w), lambda b, n: (b, n, 0, 0)),
                  tok(w), tok(w), meta, meta,
                  pl.BlockSpec((1, HG_D), lambda b, n: (0, 0))],
        out_specs=tok(w),
        out_shape=jax.ShapeDtypeStruct((batch, seq, w), BF16),
        scratch_shapes=[pltpu.VMEM((HG_HEADS, HG_D, HG_D), F32)],
        compiler_params=pltpu.CompilerParams(
            dimension_semantics=("parallel", "arbitrary"),
            vmem_limit_bytes=V7X_VMEM_LIMIT_BYTES),
        name="hgrn2",
    )(r3(a), r3(qi), r3(kl), dec.reshape(batch, seq // HG_BLOCK, 1, w), r3(vh), r3(og), klm,
      vm, g)


def _attn_kernel(q_ref, kn_ref, kp_ref, vt_ref, knm_ref, kpm_ref, vtm_ref, o_ref,
                 m_ref, l_ref, acc_ref):
    for sub in range(ATTN_GROUP):
        _attn_query_block(ATTN_GROUP * pl.program_id(1) + sub, sub % 2 == 1,
                          q_ref.at[pl.ds(sub * ATTN_Q, ATTN_Q), :], kn_ref, kp_ref, vt_ref,
                          knm_ref, kpm_ref, vtm_ref, o_ref.at[sub], m_ref, l_ref, acc_ref)


def _attn_query_block(i, i_is_odd, q_ref, kn_ref, kp_ref, vt_ref, knm_ref, kpm_ref, vtm_ref,
                      o_ref, m_ref, l_ref, acc_ref):
    tq = q_ref.shape[0]
    hs = [slice(h * V7X_LANES, (h + 1) * V7X_LANES) for h in range(MLA_HEADS)]

    def q_of(h):
        return q_ref[:, 2 * h * V7X_LANES:2 * (h + 1) * V7X_LANES]

    def with_max(s):
        return s, jnp.max(s, axis=0, keepdims=True)

    def update(carry, scored, vt_list):
        m, l, acc = carry
        s_list = [s for s, _ in scored]
        m_new = m
        for _, s_max in scored:
            m_new = jnp.maximum(m_new, s_max)
        alpha = jnp.exp2(m - m_new)
        l = alpha * l
        acc = alpha * acc
        for s, vt in zip(s_list, vt_list):
            p = jnp.exp2(s - m_new)
            l = l + jnp.sum(p, axis=0, keepdims=True)
            acc = acc + _dot(vt, p.astype(BF16))
        return m_new, l, acc

    def skewed(score_fn, consume_fn):
        scores = []
        for h in range(MLA_HEADS + ATTN_SKEW):
            if h < MLA_HEADS:
                scores.append(score_fn(h))
            g = h - ATTN_SKEW
            if g >= 0:
                consume_fn(g, scores[g])

    for h in range(MLA_HEADS):
        m_ref[h] = jnp.full((1, tq), -jnp.inf, F32)
        l_ref[h] = jnp.zeros((1, tq), F32)
        acc_ref[h] = jnp.zeros((V_HEAD, tq), F32)

    def kv_step(blocks):
        starts = [pl.multiple_of(b * ATTN_K, ATTN_K) for b in blocks]
        kps = [kp_ref[pl.ds(ks, ATTN_K), :] for ks in starts]

        def score(h):
            return [with_max(_dot_nt(
                jnp.concatenate([kn_ref[pl.ds(ks, ATTN_K), hs[h]], kp], axis=1), q_of(h)))
                for ks, kp in zip(starts, kps)]

        def consume(g, s_list):
            m, l, acc = update((m_ref[g], l_ref[g], acc_ref[g]), s_list,
                               [vt_ref[b, hs[g], :] for b in blocks])
            m_ref[g] = m
            l_ref[g] = l
            acc_ref[g] = acc

        skewed(score, consume)

    def pair_step(j2, _):
        kv_step([2 * j2, 2 * j2 + 1])
        return 0

    lax.fori_loop(0, i // 2, pair_step, 0)

    def last_step(full_blocks):
        blocks = list(full_blocks) + [i]
        starts = [pl.multiple_of(b * ATTN_K, ATTN_K) for b in blocks]
        kps = [kp_ref[pl.ds(ks, ATTN_K), :] for ks in starts]
        kpm = kpm_ref[...]
        keep = (lax.broadcasted_iota(jnp.int32, (ATTN_K, tq), 0)
                <= lax.broadcasted_iota(jnp.int32, (ATTN_K, tq), 1))

        def score(h):
            s = [_dot_nt(jnp.concatenate([kn_ref[pl.ds(ks, ATTN_K), hs[h]], kp], axis=1),
                         q_of(h)) for ks, kp in zip(starts, kps)]
            s[-1] = jnp.where(keep, s[-1], -jnp.inf)
            km = jnp.concatenate([knm_ref[:, hs[h]], kpm], axis=1)
            return [with_max(t) for t in s + [_dot_nt(km, q_of(h))]]

        def finish(g, s_list):
            _, l, acc = update((m_ref[g], l_ref[g], acc_ref[g]), s_list,
                               [vt_ref[b, hs[g], :] for b in blocks] + [vtm_ref[hs[g], :]])
            o_ref[hs[g], :] = (acc / l).astype(BF16)

        skewed(score, finish)

    last_step([i - 1] if i_is_odd else [])


def _attention(q, kn, kp, vt, knm, kpm, vtm, batch, seq):
    assert ATTN_Q == ATTN_K == INPROJ_ROWS // INPROJ_SPLIT and ATTN_GROUP % 2 == 0
    w = kn.shape[-1]
    nq = seq // ATTN_Q
    r3 = lambda a: a.reshape(batch, seq, a.shape[-1])
    qblk = pl.BlockSpec((None, ATTN_GROUP * ATTN_Q, 2 * w), lambda b, i: (b, i, 0))
    oblk = pl.BlockSpec((None, ATTN_GROUP, w, ATTN_Q), lambda b, i: (b, i, 0, 0))
    kfull = pl.BlockSpec((None, seq, w), lambda b, i: (b, 0, 0))
    kpfull = pl.BlockSpec((None, seq, V7X_LANES), lambda b, i: (b, 0, 0))
    vtfull = pl.BlockSpec((None, nq, w, ATTN_K), lambda b, i: (b, 0, 0, 0))
    mh = pl.BlockSpec((N_META, w), lambda b, i: (0, 0))
    mp = pl.BlockSpec((N_META, V7X_LANES), lambda b, i: (0, 0))
    mvt = pl.BlockSpec((w, N_META), lambda b, i: (0, 0))
    return pl.pallas_call(
        _attn_kernel,
        grid=(batch, nq // ATTN_GROUP),
        in_specs=[qblk, kfull, kpfull, vtfull, mh, mp, mvt],
        out_specs=oblk,
        out_shape=jax.ShapeDtypeStruct((batch, nq, w, ATTN_Q), BF16),
        scratch_shapes=[pltpu.VMEM((MLA_HEADS, 1, ATTN_Q), F32),
                        pltpu.VMEM((MLA_HEADS, 1, ATTN_Q), F32),
                        pltpu.VMEM((MLA_HEADS, V_HEAD, ATTN_Q), F32)],
        compiler_params=pltpu.CompilerParams(
            dimension_semantics=("parallel", "arbitrary"),
            vmem_limit_bytes=V7X_VMEM_LIMIT_BYTES),
        name="mla_attn",
    )(r3(q), r3(kn), r3(kp), vt.reshape(batch, nq, w, ATTN_K), knm, kpm,
      vtm.reshape(w, N_META))


def _out_kernel(x_ref, oh_ref, oat_ref, gates_ref, who_ref, wmo_ref, wout_ref, gmix_ref,
                gfpre_ref, wfin_ref, wfout_ref, gfpost_ref, y_ref):
    rows, d = x_ref.shape
    hidden = wfout_ref.shape[0]
    halves = [slice(i * rows // OUT_SPLIT, (i + 1) * rows // OUT_SPLIT) for i in range(OUT_SPLIT)]

    def merge(idx, r):
        ya = _dot(oh_ref[r, :], who_ref[...])
        yb = _dot_tn(oat_ref[idx], wmo_ref[...])
        gates = gates_ref[r, :].astype(F32)
        return (gates[:, :d] * ya + gates[:, d:] * yb).astype(BF16)

    def mix_residual(r, merged):
        h1 = x_ref[r, :] + _rms(_dot(merged, wout_ref[...]), gmix_ref[...])
        return h1, _rms(h1, gfpre_ref[...]).astype(BF16)

    def ffn_act(u):
        gu = _dot(u, wfin_ref[...])
        gt = gu[:, :hidden]
        return (gt * _sigmoid(gt) * gu[:, hidden:]).astype(BF16)

    merged = [merge(idx, r) for idx, r in enumerate(halves)]
    h1_u = [mix_residual(r, m) for r, m in zip(halves, merged)]
    acts = [ffn_act(u) for _, u in h1_u]
    for r, (h1, _), act in zip(halves, h1_u, acts):
        y_ref[r, :] = h1 + _rms(_dot(act, wfout_ref[...]), gfpost_ref[...])


def _out_block(x2d, oh, oa, gates, params):
    t, d = x2d.shape
    n = t // OUT_ROWS
    row = lambda w: pl.BlockSpec((OUT_ROWS, w), lambda i: (i, 0))
    return pl.pallas_call(
        _out_kernel,
        grid=(n,),
        in_specs=[row(d), row(d),
                  pl.BlockSpec((OUT_SPLIT, oa.shape[1], OUT_ROWS // OUT_SPLIT), lambda i: (i, 0, 0)),
                  row(2 * d)] + [_resident(p.shape) for p in params],
        out_specs=row(d),
        out_shape=jax.ShapeDtypeStruct((t, d), F32),
        compiler_params=pltpu.CompilerParams(
            dimension_semantics=("parallel",), vmem_limit_bytes=V7X_VMEM_LIMIT_BYTES),
        name="merge_ffn",
    )(x2d, oh, oa, gates, *params)


def _rope_tables(length):
    pos = jnp.arange(length, dtype=F32)
    inv_freq = 1.0 / (ROPE_THETA ** (jnp.arange(0, QK_ROPE, 2, dtype=F32) / QK_ROPE))
    ang = pos[:, None] * inv_freq[None, :]
    cos, sin = jnp.cos(ang), jnp.sin(ang)
    zero = jnp.zeros((length, V7X_LANES - QK_ROPE), F32)
    return (jnp.concatenate([cos, cos, zero], axis=1),
            jnp.concatenate([-sin, sin, zero], axis=1))


def _swap_halves(w):
    half = w.shape[-1] // 2
    return jnp.concatenate([w[..., half:], w[..., :half]], axis=-1)


def kernel(x, meta_tokens, w_in, b_gate, lb_logits, hg_norm_g, w_hg_o, q_a_norm_g, w_q_b,
           kv_a_norm_g, w_kv_b, w_mla_o, w_out, mix_pre_g, mix_post_g, ffn_pre_g, ffn_post_g,
           w_ffn_in, w_ffn_out):
    batch, seq, d = x.shape
    assert w_in.shape[0] == 1, "single-layer block"
    assert 2 * QK_ROPE == V7X_LANES, "rotary slab [t | rot(t)] must fill one lane tile"
    assert seq % INPROJ_ROWS == 0 and seq % ATTN_Q == 0 and seq % HG_STEP == 0
    assert INPROJ_ROWS % HG_BLOCK == 0 and HG_STEP % HG_BLOCK == 0
    assert (batch * seq) % OUT_ROWS == 0
    hgw = HG_HEADS * HG_D
    row = lambda a: a.reshape(1, -1).astype(F32)

    wi = w_in[0]
    o = 0
    parts = []
    for sz in (hgw, hgw, hgw, hgw, Q_LORA + KV_LORA, QK_ROPE, 2 * d):
        parts.append(wi[:, o:o + sz])
        o += sz
    whq, whf, whi, whg, wc, wkpe, wgate = parts
    wpe = jnp.concatenate([wkpe, _swap_halves(wkpe)], axis=1)
    wq = w_q_b[0].reshape(Q_LORA, MLA_HEADS, QK_NOPE + QK_ROPE)
    wqn = wq[:, :, :QK_NOPE].reshape(Q_LORA, MLA_HEADS * QK_NOPE)
    wq_pe = wq[:, :, QK_NOPE:]
    wqp = jnp.concatenate([wq_pe, _swap_halves(wq_pe)], axis=2).reshape(
        Q_LORA, MLA_HEADS * V7X_LANES)
    wkv = w_kv_b[0].reshape(KV_LORA, MLA_HEADS, QK_NOPE + V_HEAD)
    wkn = wkv[:, :, :QK_NOPE].reshape(KV_LORA, MLA_HEADS * QK_NOPE)
    wv = wkv[:, :, QK_NOPE:].reshape(KV_LORA, MLA_HEADS * V_HEAD).T
    bf = lambda a: a.astype(BF16)
    inproj_params = [row(mix_pre_g[0]), bf(whq), bf(whf), bf(whi), bf(whg), bf(wc), bf(wpe),
                     bf(wgate), row(b_gate[0]), lb_logits.astype(F32), row(q_a_norm_g[0]),
                     row(kv_a_norm_g[0]), bf(wqn), bf(wqp), bf(wkn), bf(wv)]
    cos_t, sin_t = _rope_tables(N_META + seq)

    m_out = _inproj(meta_tokens.astype(F32), N_META, 1, inproj_params,
                    cos_t[:N_META], sin_t[:N_META], with_intra=False)
    _, kl_m, _, vh_m, _, _, _, kn_m, kp_m, va_m = m_out

    x2d = x.reshape(batch * seq, d)
    (qi, kl, dec, vh, og, gates, q, kn, kp, va, a_hg) = _inproj(
        x2d, INPROJ_ROWS, seq // INPROJ_ROWS, inproj_params, cos_t[N_META:], sin_t[N_META:],
        with_intra=True)

    o_hg = _hgrn(a_hg, qi, kl, dec, vh, og, kl_m, vh_m, row(hg_norm_g[0]), batch, seq)
    o_at = _attention(q, kn, kp, va, kn_m, kp_m, va_m, batch, seq)

    out_params = [bf(w_hg_o[0]), bf(w_mla_o[0]), bf(w_out[0]), row(mix_post_g[0]),
                  row(ffn_pre_g[0]), bf(w_ffn_in[0]), bf(w_ffn_out[0]), row(ffn_post_g[0])]
    assert OUT_ROWS // OUT_SPLIT == ATTN_Q
    y = _out_block(x2d, o_hg.reshape(batch * seq, hgw),
                   o_at.reshape(batch * (seq // ATTN_Q), -1, ATTN_Q), gates, out_params)
    return y.reshape(batch, seq, d)
```

```python
import functools
import math

import jax
import jax.numpy as jnp
from jax import lax
from jax.experimental import pallas as pl
from jax.experimental.pallas import tpu as pltpu

N_META = 16
NORM_EPS = 1e-6
HG_HEADS = 8
HG_D = 128
MLA_HEADS = 8
QK_NOPE = 128
QK_ROPE = 64
V_HEAD = 128
Q_LORA = 256
KV_LORA = 256
ROPE_THETA = 10000.0

V7X_LANES = 128
V7X_VMEM_LIMIT_BYTES = 56 * 1024 * 1024

INPROJ_ROWS = 512
INPROJ_SPLIT = 2
HG_BLOCK = 64
HG_STEP = 1024
HG_SAFE_EXPONENT = 60.0
ATTN_Q = 256
ATTN_K = 256
ATTN_GROUP = 4
ATTN_SKEW = 8
OUT_ROWS = 512
OUT_SPLIT = 2

F32 = jnp.float32
BF16 = jnp.bfloat16


def _dot(a, b):
    return jnp.dot(a, b, preferred_element_type=F32)


def _dot_nt(a, b):
    return lax.dot_general(a, b, (((1,), (1,)), ((), ())), preferred_element_type=F32)


def _dot_tn(a, b):
    return lax.dot_general(a, b, (((0,), (0,)), ((), ())), preferred_element_type=F32)


def _rms(x, g):
    ms = jnp.mean(x * x, axis=-1, keepdims=True)
    return x * lax.rsqrt(ms + NORM_EPS) * g


def _sigmoid(x):
    return 1.0 / (1.0 + jnp.exp(-x))


def _rope(slab, cos, sin):
    half = slab.shape[1] // 2
    return slab * cos + pltpu.roll(slab, half, 1) * sin


def _resident(shape):
    return pl.BlockSpec(shape, lambda *_: (0,) * len(shape), pipeline_mode=pl.Buffered(1))


def _cumsum_blocks(x, block):
    pos = lax.broadcasted_iota(jnp.int32, x.shape, 0) % block
    s = 1
    while s < block:
        x = x + jnp.where(pos >= s, pltpu.roll(x, s, 0), 0.0)
        s *= 2
    return x


def _inproj_kernel(x_ref, gpre_ref, whq_ref, whf_ref, whi_ref, whg_ref, wc_ref, wpe_ref,
                   wgate_ref, bgate_ref, lbl_ref, qg_ref, kvg_ref, wqn_ref, wqp_ref,
                   wkn_ref, wv_ref, cos_ref, sin_ref,
                   qi_ref, kl_ref, dec_ref, vh_ref, og_ref, gates_ref,
                   q_ref, kn_ref, kp_ref, va_ref, *intra_refs, scale, hg_block, split):
    rows, d = x_ref.shape
    part_rows = rows // split
    parts = [slice(i * part_rows, (i + 1) * part_rows) for i in range(split)]
    nblk = part_rows // hg_block
    mid = hg_block // 2 - 1
    pair_w = 2 * HG_D

    lbl = lbl_ref[...]
    e = jnp.exp(lbl - jnp.max(lbl, axis=0, keepdims=True))
    lb = e[0:1] / jnp.sum(e, axis=0, keepdims=True)

    stage1 = []
    for r in parts:
        u = _rms(x_ref[r, :], gpre_ref[...]).astype(BF16)
        hq = _dot(u, whq_ref[...])
        hf = _dot(u, whf_ref[...])
        vh_ref[r, :] = _dot(u, whi_ref[...]).astype(BF16)
        pe = _dot(u, wpe_ref[...])
        cmla = _dot(u, wc_ref[...])
        stage1.append((u, hq, hf, pe, cmla))

    stage2 = []
    for _, hq, hf, _, _ in stage1:
        q = hq * _sigmoid(hq)
        sg = _sigmoid(hf)
        k = (1.0 - lb) * (1.0 - sg)
        c = _cumsum_blocks(jnp.log(lb + (1.0 - lb) * sg), hg_block)
        stage2.append((q, k, c))

    zero = jnp.zeros((hg_block, HG_D), BF16)
    if intra_refs:
        a_ref, ksc_ref, csc_ref = intra_refs
        tri = (lax.broadcasted_iota(jnp.int32, (hg_block, 2 * hg_block), 0)
               >= lax.broadcasted_iota(jnp.int32, (hg_block, 2 * hg_block), 1) % hg_block)
        worst = jnp.zeros((1, d), F32)
    for idx, r in enumerate(parts):
        u, _, _, pe, cmla = stage1[idx]
        q, k, c = stage2[idx]
        cos = cos_ref[r, :]
        sin = sin_ref[r, :]

        def gate_half(half):
            cols = slice(half * d, (half + 1) * d)
            gates_ref[r, cols] = _sigmoid(
                _dot(u, wgate_ref[:, cols]) + bgate_ref[:, cols]).astype(BF16)

        def out_gate():
            hg = _dot(u, whg_ref[...])
            og_ref[r, :] = (hg * _sigmoid(hg)).astype(BF16)

        def mla_q():
            cq = _rms(cmla[:, :Q_LORA], qg_ref[...]).astype(BF16)
            qn = _dot(cq, wqn_ref[...])
            qpe = _dot(cq, wqp_ref[...])
            for h in range(MLA_HEADS):
                sl = slice(h * V7X_LANES, (h + 1) * V7X_LANES)
                base = 2 * h * V7X_LANES
                q_ref[r, base:base + V7X_LANES] = (qn[:, sl] * scale).astype(BF16)
                q_ref[r, base + V7X_LANES:base + 2 * V7X_LANES] = (_rope(qpe[:, sl], cos, sin)
                                                                    * scale).astype(BF16)

        def mla_kv():
            kp_ref[r, :] = _rope(pe, cos, sin).astype(BF16)
            ckv = _rms(cmla[:, Q_LORA:], kvg_ref[...]).astype(BF16)
            kn_ref[r, :] = _dot(ckv, wkn_ref[...]).astype(BF16)
            va_ref[idx] = _dot_nt(wv_ref[...], ckv).astype(BF16)

        fillers = [out_gate, functools.partial(gate_half, 0), functools.partial(gate_half, 1),
                   mla_q, mla_kv]
        for blk in range(nblk):
            rb = slice(blk * hg_block, (blk + 1) * hg_block)
            ro = slice(r.start + rb.start, r.start + rb.stop)
            cb = c[rb]
            cm = cb[mid:mid + 1]
            cl = cb[hg_block - 1:hg_block]
            qi_ref[ro, :] = (q[rb] * jnp.exp(cb)).astype(BF16)
            kl_ref[ro, :] = (k[rb] * jnp.exp(cl - cb)).astype(BF16)
            dec_ref[idx * nblk + blk] = jnp.exp(cl)
            if intra_refs:
                worst = jnp.maximum(worst, jnp.maximum(cb[0:1] - cm, cm - cl))
                qt = (q[rb] * jnp.exp(cb - cm)).astype(BF16)
                kt = (k[rb] * jnp.exp(cm - cb)).astype(BF16)
                for p in range(HG_HEADS // 2):
                    ls = slice(p * pair_w, (p + 1) * pair_w)
                    kb = jnp.concatenate(
                        [jnp.concatenate([kt[:, ls][:, :HG_D], zero], axis=1),
                         jnp.concatenate([zero, kt[:, ls][:, HG_D:]], axis=1)], axis=0)
                    a = jnp.where(tri, _dot_nt(qt[:, ls], kb), 0.0)
                    a_ref[ro, p * 2 * hg_block:(p + 1) * 2 * hg_block] = a.astype(BF16)
            if fillers:
                fillers.pop(0)()
        while fillers:
            fillers.pop(0)()

    if intra_refs:
        @pl.when(jnp.max(worst) > HG_SAFE_EXPONENT)
        def _exact_intra():
            a_w = HG_HEADS * hg_block
            head_of_lane = lax.broadcasted_iota(jnp.int32, (d, HG_HEADS), 0) // HG_D
            head_sum = (head_of_lane
                        == lax.broadcasted_iota(jnp.int32, (d, HG_HEADS), 1)).astype(F32)
            col_base = lax.broadcasted_iota(jnp.int32, (HG_HEADS, a_w), 0) * hg_block
            col_lane = lax.broadcasted_iota(jnp.int32, (HG_HEADS, a_w), 1)
            causal = (lax.broadcasted_iota(jnp.int32, (hg_block, a_w), 0)
                      >= lax.broadcasted_iota(jnp.int32, (hg_block, a_w), 1) % hg_block)
            for r, (q, k, c) in zip(parts, stage2):
                ksc_ref[...] = k
                csc_ref[...] = c
                for blk in range(nblk):
                    rb = slice(blk * hg_block, (blk + 1) * hg_block)
                    first = r.start + rb.start
                    qb = q[rb]
                    cb = c[rb]

                    def key_column(s, acc):
                        row = rb.start + s
                        w = (qb * jnp.exp(jnp.minimum(cb - csc_ref[pl.ds(row, 1), :], 0.0))
                             * ksc_ref[pl.ds(row, 1), :])
                        cols = _dot(w, head_sum)
                        place = (col_lane == col_base + s).astype(F32)
                        return acc + _dot(cols, place)

                    acc = lax.fori_loop(0, hg_block, key_column,
                                        jnp.zeros((hg_block, a_w), F32))
                    a_ref[first:first + hg_block, :] = jnp.where(causal, acc, 0.0).astype(BF16)


def _inproj(x2d, rows, pos_blocks, params, cos_t, sin_t, with_intra):
    t, d = x2d.shape
    n = t // rows
    row = lambda w: pl.BlockSpec((rows, w), lambda i: (i, 0))
    pos = pl.BlockSpec((rows, V7X_LANES), lambda i: (i % pos_blocks, 0))
    w_specs = [_resident(p.shape) for p in params]
    hg_block = min(HG_BLOCK, rows)
    nb = rows // hg_block
    split = INPROJ_SPLIT if rows % (INPROJ_SPLIT * hg_block) == 0 else 1
    vt_w = MLA_HEADS * V_HEAD
    lead = lambda *shape: pl.BlockSpec((None,) + shape, lambda i: (i,) + (0,) * len(shape))
    outs = [(row(d), (t, d), BF16), (row(d), (t, d), BF16),
            (lead(nb, 1, d), (n, nb, 1, d), F32),
            (row(d), (t, d), BF16), (row(d), (t, d), BF16), (row(2 * d), (t, 2 * d), BF16),
            (row(2 * d), (t, 2 * d), BF16), (row(d), (t, d), BF16),
            (row(V7X_LANES), (t, V7X_LANES), BF16),
            (lead(split, vt_w, rows // split), (n, split, vt_w, rows // split), BF16)]
    scratch = []
    if with_intra:
        a_w = HG_HEADS * hg_block
        outs.append((row(a_w), (t, a_w), BF16))
        scratch = [pltpu.VMEM((rows // split, d), F32), pltpu.VMEM((rows // split, d), F32)]
    scale = (QK_NOPE + QK_ROPE) ** -0.5 * math.log2(math.e)
    return pl.pallas_call(
        functools.partial(_inproj_kernel, scale=scale, hg_block=hg_block, split=split),
        grid=(n,),
        in_specs=[row(d)] + w_specs + [pos, pos],
        out_specs=[o[0] for o in outs],
        out_shape=[jax.ShapeDtypeStruct(o[1], o[2]) for o in outs],
        scratch_shapes=scratch,
        compiler_params=pltpu.CompilerParams(
            dimension_semantics=("parallel",), vmem_limit_bytes=V7X_VMEM_LIMIT_BYTES),
        name="inproj",
    )(x2d, *params, cos_t, sin_t)


def _hgrn_kernel(a_ref, qi_ref, kl_ref, dec_ref, v_ref, og_ref, klm_ref, vm_ref, g_ref,
                 o_ref, st_ref):
    n = pl.program_id(1)
    c_blk = qi_ref.shape[0]
    hs = [slice(h * HG_D, (h + 1) * HG_D) for h in range(HG_HEADS)]

    @pl.when(n == 0)
    def _init():
        vm = vm_ref[...]
        klm = klm_ref[...]
        for h, sl in enumerate(hs):
            st_ref[h] = _dot_tn(vm[:, sl], klm[:, sl])

    g = g_ref[...]
    zero = jnp.zeros((HG_BLOCK, HG_D), BF16)
    for blk in range(c_blk // HG_BLOCK):
        r = slice(blk * HG_BLOCK, (blk + 1) * HG_BLOCK)
        a = a_ref[r, :]
        qi = qi_ref[r, :]
        kl = kl_ref[r, :]
        v = v_ref[r, :]
        dec = dec_ref[blk]
        outs = []
        for p in range(HG_HEADS // 2):
            ha, hb = hs[2 * p], hs[2 * p + 1]
            vb = jnp.concatenate([jnp.concatenate([v[:, ha], zero], axis=1),
                                  jnp.concatenate([zero, v[:, hb]], axis=1)], axis=0)
            o_pair = _dot(a[:, p * 2 * HG_BLOCK:(p + 1) * 2 * HG_BLOCK], vb)
            for e, sl in enumerate((ha, hb)):
                h = 2 * p + e
                st = st_ref[h]
                outs.append(o_pair[:, e * HG_D:(e + 1) * HG_D]
                            + _dot_nt(qi[:, sl], st.astype(BF16)))
                st_ref[h] = st * dec[:, sl] + _dot_tn(v[:, sl], kl[:, sl])
        for o, sl in zip(outs, hs):
            o_ref[r, sl] = (_rms(o, g) * og_ref[r, sl].astype(F32)).astype(BF16)


def _hgrn(a, qi, kl, dec, vh, og, klm, vm, g, batch, seq):
    w = qi.shape[-1]
    nstep = seq // HG_STEP
    sub = HG_STEP // HG_BLOCK
    tok = lambda width: pl.BlockSpec((None, HG_STEP, width), lambda b, n: (b, n, 0))
    meta = pl.BlockSpec((N_META, w), lambda b, n: (0, 0))
    r3 = lambda x: x.reshape(batch, seq, x.shape[-1])
    return pl.pallas_call(
        _hgrn_kernel,
        grid=(batch, nstep),
        in_specs=[tok(a.shape[-1]), tok(w), tok(w),
                  pl.BlockSpec((None, sub, 1, w), lambda b, n: (b, n, 0, 0)),
                  tok(w), tok(w), meta, meta,
                  pl.BlockSpec((1, HG_D), lambda b, n: (0, 0))],
        out_specs=tok(w),
        out_shape=jax.ShapeDtypeStruct((batch, seq, w), BF16),
        scratch_shapes=[pltpu.VMEM((HG_HEADS, HG_D, HG_D), F32)],
        compiler_params=pltpu.CompilerParams(
            dimension_semantics=("parallel", "arbitrary"),
            vmem_limit_bytes=V7X_VMEM_LIMIT_BYTES),
        name="hgrn2",
    )(r3(a), r3(qi), r3(kl), dec.reshape(batch, seq // HG_BLOCK, 1, w), r3(vh), r3(og), klm,
      vm, g)


def _attn_kernel(q_ref, kn_ref, kp_ref, vt_ref, knm_ref, kpm_ref, vtm_ref, o_ref,
                 m_ref, l_ref, acc_ref):
    for sub in range(ATTN_GROUP):
        _attn_query_block(ATTN_GROUP * pl.program_id(1) + sub, sub % 2 == 1,
                          q_ref.at[pl.ds(sub * ATTN_Q, ATTN_Q), :], kn_ref, kp_ref, vt_ref,
                          knm_ref, kpm_ref, vtm_ref, o_ref.at[sub], m_ref, l_ref, acc_ref)


def _attn_query_block(i, i_is_odd, q_ref, kn_ref, kp_ref, vt_ref, knm_ref, kpm_ref, vtm_ref,
                      o_ref, m_ref, l_ref, acc_ref):
    tq = q_ref.shape[0]
    hs = [slice(h * V7X_LANES, (h + 1) * V7X_LANES) for h in range(MLA_HEADS)]

    def q_of(h):
        return q_ref[:, 2 * h * V7X_LANES:2 * (h + 1) * V7X_LANES]

    def with_max(s):
        return s, jnp.max(s, axis=0, keepdims=True)

    def update(carry, scored, vt_list):
        m, l, acc = carry
        s_list = [s for s, _ in scored]
        m_new = m
        for _, s_max in scored:
            m_new = jnp.maximum(m_new, s_max)
        alpha = jnp.exp2(m - m_new)
        l = alpha * l
        acc = alpha * acc
        for s, vt in zip(s_list, vt_list):
            p = jnp.exp2(s - m_new)
            l = l + jnp.sum(p, axis=0, keepdims=True)
            acc = acc + _dot(vt, p.astype(BF16))
        return m_new, l, acc

    def skewed(score_fn, consume_fn):
        scores = []
        for h in range(MLA_HEADS + ATTN_SKEW):
            if h < MLA_HEADS:
                scores.append(score_fn(h))
            g = h - ATTN_SKEW
            if g >= 0:
                consume_fn(g, scores[g])

    for h in range(MLA_HEADS):
        m_ref[h] = jnp.full((1, tq), -jnp.inf, F32)
        l_ref[h] = jnp.zeros((1, tq), F32)
        acc_ref[h] = jnp.zeros((V_HEAD, tq), F32)

    def kv_step(blocks):
        starts = [pl.multiple_of(b * ATTN_K, ATTN_K) for b in blocks]
        kps = [kp_ref[pl.ds(ks, ATTN_K), :] for ks in starts]

        def score(h):
            return [with_max(_dot_nt(
                jnp.concatenate([kn_ref[pl.ds(ks, ATTN_K), hs[h]], kp], axis=1), q_of(h)))
                for ks, kp in zip(starts, kps)]

        def consume(g, s_list):
            m, l, acc = update((m_ref[g], l_ref[g], acc_ref[g]), s_list,
                               [vt_ref[b, hs[g], :] for b in blocks])
            m_ref[g] = m
            l_ref[g] = l
            acc_ref[g] = acc

        skewed(score, consume)

    def pair_step(j2, _):
        kv_step([2 * j2, 2 * j2 + 1])
        return 0

    lax.fori_loop(0, i // 2, pair_step, 0)

    def last_step(full_blocks):
        blocks = list(full_blocks) + [i]
        starts = [pl.multiple_of(b * ATTN_K, ATTN_K) for b in blocks]
        kps = [kp_ref[pl.ds(ks, ATTN_K), :] for ks in starts]
        kpm = kpm_ref[...]
        keep = (lax.broadcasted_iota(jnp.int32, (ATTN_K, tq), 0)
                <= lax.broadcasted_iota(jnp.int32, (ATTN_K, tq), 1))

        def score(h):
            s = [_dot_nt(jnp.concatenate([kn_ref[pl.ds(ks, ATTN_K), hs[h]], kp], axis=1),
                         q_of(h)) for ks, kp in zip(starts, kps)]
            s[-1] = jnp.where(keep, s[-1], -jnp.inf)
            km = jnp.concatenate([knm_ref[:, hs[h]], kpm], axis=1)
            return [with_max(t) for t in s + [_dot_nt(km, q_of(h))]]

        def finish(g, s_list):
            _, l, acc = update((m_ref[g], l_ref[g], acc_ref[g]), s_list,
                               [vt_ref[b, hs[g], :] for b in blocks] + [vtm_ref[hs[g], :]])
            o_ref[hs[g], :] = (acc / l).astype(BF16)

        skewed(score, finish)

    last_step([i - 1] if i_is_odd else [])


def _attention(q, kn, kp, vt, knm, kpm, vtm, batch, seq):
    assert ATTN_Q == ATTN_K == INPROJ_ROWS // INPROJ_SPLIT and ATTN_GROUP % 2 == 0
    w = kn.shape[-1]
    nq = seq // ATTN_Q
    r3 = lambda a: a.reshape(batch, seq, a.shape[-1])
    qblk = pl.BlockSpec((None, ATTN_GROUP * ATTN_Q, 2 * w), lambda b, i: (b, i, 0))
    oblk = pl.BlockSpec((None, ATTN_GROUP, w, ATTN_Q), lambda b, i: (b, i, 0, 0))
    kfull = pl.BlockSpec((None, seq, w), lambda b, i: (b, 0, 0))
    kpfull = pl.BlockSpec((None, seq, V7X_LANES), lambda b, i: (b, 0, 0))
    vtfull = pl.BlockSpec((None, nq, w, ATTN_K), lambda b, i: (b, 0, 0, 0))
    mh = pl.BlockSpec((N_META, w), lambda b, i: (0, 0))
    mp = pl.BlockSpec((N_META, V7X_LANES), lambda b, i: (0, 0))
    mvt = pl.BlockSpec((w, N_META), lambda b, i: (0, 0))
    return pl.pallas_call(
        _attn_kernel,
        grid=(batch, nq // ATTN_GROUP),
        in_specs=[qblk, kfull, kpfull, vtfull, mh, mp, mvt],
        out_specs=oblk,
        out_shape=jax.ShapeDtypeStruct((batch, nq, w, ATTN_Q), BF16),
        scratch_shapes=[pltpu.VMEM((MLA_HEADS, 1, ATTN_Q), F32),
                        pltpu.VMEM((MLA_HEADS, 1, ATTN_Q), F32),
                        pltpu.VMEM((MLA_HEADS, V_HEAD, ATTN_Q), F32)],
        compiler_params=pltpu.CompilerParams(
            dimension_semantics=("parallel", "arbitrary"),
            vmem_limit_bytes=V7X_VMEM_LIMIT_BYTES),
        name="mla_attn",
    )(r3(q), r3(kn), r3(kp), vt.reshape(batch, nq, w, ATTN_K), knm, kpm,
      vtm.reshape(w, N_META))


def _out_kernel(x_ref, oh_ref, oat_ref, gates_ref, who_ref, wmo_ref, wout_ref, gmix_ref,
                gfpre_ref, wfin_ref, wfout_ref, gfpost_ref, y_ref):
    rows, d = x_ref.shape
    hidden = wfout_ref.shape[0]
    halves = [slice(i * rows // OUT_SPLIT, (i + 1) * rows // OUT_SPLIT) for i in range(OUT_SPLIT)]

    def merge(idx, r):
        ya = _dot(oh_ref[r, :], who_ref[...])
        yb = _dot_tn(oat_ref[idx], wmo_ref[...])
        gates = gates_ref[r, :].astype(F32)
        return (gates[:, :d] * ya + gates[:, d:] * yb).astype(BF16)

    def mix_residual(r, merged):
        h1 = x_ref[r, :] + _rms(_dot(merged, wout_ref[...]), gmix_ref[...])
        return h1, _rms(h1, gfpre_ref[...]).astype(BF16)

    def ffn_act(u):
        gu = _dot(u, wfin_ref[...])
        gt = gu[:, :hidden]
        return (gt * _sigmoid(gt) * gu[:, hidden:]).astype(BF16)

    merged = [merge(idx, r) for idx, r in enumerate(halves)]
    h1_u = [mix_residual(r, m) for r, m in zip(halves, merged)]
    acts = [ffn_act(u) for _, u in h1_u]
    for r, (h1, _), act in zip(halves, h1_u, acts):
        y_ref[r, :] = h1 + _rms(_dot(act, wfout_ref[...]), gfpost_ref[...])


def _out_block(x2d, oh, oa, gates, params):
    t, d = x2d.shape
    n = t // OUT_ROWS
    row = lambda w: pl.BlockSpec((OUT_ROWS, w), lambda i: (i, 0))
    return pl.pallas_call(
        _out_kernel,
        grid=(n,),
        in_specs=[row(d), row(d),
                  pl.BlockSpec((OUT_SPLIT, oa.shape[1], OUT_ROWS // OUT_SPLIT), lambda i: (i, 0, 0)),
                  row(2 * d)] + [_resident(p.shape) for p in params],
        out_specs=row(d),
        out_shape=jax.ShapeDtypeStruct((t, d), F32),
        compiler_params=pltpu.CompilerParams(
            dimension_semantics=("parallel",), vmem_limit_bytes=V7X_VMEM_LIMIT_BYTES),
        name="merge_ffn",
    )(x2d, oh, oa, gates, *params)


def _rope_tables(length):
    pos = jnp.arange(length, dtype=F32)
    inv_freq = 1.0 / (ROPE_THETA ** (jnp.arange(0, QK_ROPE, 2, dtype=F32) / QK_ROPE))
    ang = pos[:, None] * inv_freq[None, :]
    cos, sin = jnp.cos(ang), jnp.sin(ang)
    zero = jnp.zeros((length, V7X_LANES - QK_ROPE), F32)
    return (jnp.concatenate([cos, cos, zero], axis=1),
            jnp.concatenate([-sin, sin, zero], axis=1))


def _swap_halves(w):
    half = w.shape[-1] // 2
    return jnp.concatenate([w[..., half:], w[..., :half]], axis=-1)


def kernel(x, meta_tokens, w_in, b_gate, lb_logits, hg_norm_g, w_hg_o, q_a_norm_g, w_q_b,
           kv_a_norm_g, w_kv_b, w_mla_o, w_out, mix_pre_g, mix_post_g, ffn_pre_g, ffn_post_g,
           w_ffn_in, w_ffn_out):
    batch, seq, d = x.shape
    assert w_in.shape[0] == 1, "single-layer block"
    assert 2 * QK_ROPE == V7X_LANES, "rotary slab [t | rot(t)] must fill one lane tile"
    assert seq % INPROJ_ROWS == 0 and seq % ATTN_Q == 0 and seq % HG_STEP == 0
    assert INPROJ_ROWS % HG_BLOCK == 0 and HG_STEP % HG_BLOCK == 0
    assert (batch * seq) % OUT_ROWS == 0
    hgw = HG_HEADS * HG_D
    row = lambda a: a.reshape(1, -1).astype(F32)

    wi = w_in[0]
    o = 0
    parts = []
    for sz in (hgw, hgw, hgw, hgw, Q_LORA + KV_LORA, QK_ROPE, 2 * d):
        parts.append(wi[:, o:o + sz])
        o += sz
    whq, whf, whi, whg, wc, wkpe, wgate = parts
    wpe = jnp.concatenate([wkpe, _swap_halves(wkpe)], axis=1)
    wq = w_q_b[0].reshape(Q_LORA, MLA_HEADS, QK_NOPE + QK_ROPE)
    wqn = wq[:, :, :QK_NOPE].reshape(Q_LORA, MLA_HEADS * QK_NOPE)
    wq_pe = wq[:, :, QK_NOPE:]
    wqp = jnp.concatenate([wq_pe, _swap_halves(wq_pe)], axis=2).reshape(
        Q_LORA, MLA_HEADS * V7X_LANES)
    wkv = w_kv_b[0].reshape(KV_LORA, MLA_HEADS, QK_NOPE + V_HEAD)
    wkn = wkv[:, :, :QK_NOPE].reshape(KV_LORA, MLA_HEADS * QK_NOPE)
    wv = wkv[:, :, QK_NOPE:].reshape(KV_LORA, MLA_HEADS * V_HEAD).T
    bf = lambda a: a.astype(BF16)
    inproj_params = [row(mix_pre_g[0]), bf(whq), bf(whf), bf(whi), bf(whg), bf(wc), bf(wpe),
                     bf(wgate), row(b_gate[0]), lb_logits.astype(F32), row(q_a_norm_g[0]),
                     row(kv_a_norm_g[0]), bf(wqn), bf(wqp), bf(wkn), bf(wv)]
    cos_t, sin_t = _rope_tables(N_META + seq)

    m_out = _inproj(meta_tokens.astype(F32), N_META, 1, inproj_params,
                    cos_t[:N_META], sin_t[:N_META], with_intra=False)
    _, kl_m, _, vh_m, _, _, _, kn_m, kp_m, va_m = m_out

    x2d = x.reshape(batch * seq, d)
    (qi, kl, dec, vh, og, gates, q, kn, kp, va, a_hg) = _inproj(
        x2d, INPROJ_ROWS, seq // INPROJ_ROWS, inproj_params, cos_t[N_META:], sin_t[N_META:],
        with_intra=True)

    o_hg = _hgrn(a_hg, qi, kl, dec, vh, og, kl_m, vh_m, row(hg_norm_g[0]), batch, seq)
    o_at = _attention(q, kn, kp, va, kn_m, kp_m, va_m, batch, seq)

    out_params = [bf(w_hg_o[0]), bf(w_mla_o[0]), bf(w_out[0]), row(mix_post_g[0]),
                  row(ffn_pre_g[0]), bf(w_ffn_in[0]), bf(w_ffn_out[0]), row(ffn_post_g[0])]
    assert OUT_ROWS // OUT_SPLIT == ATTN_Q
    y = _out_block(x2d, o_hg.reshape(batch * seq, hgw),
                   o_at.reshape(batch * (seq // ATTN_Q), -1, ATTN_Q), gates, out_params)
    return y.reshape(batch, seq, d)
```

```python
import functools
import math

import jax
import jax.numpy as jnp
from jax import lax
from jax.experimental import pallas as pl
from jax.experimental.pallas import tpu as pltpu

N_META = 16
NORM_EPS = 1e-6
HG_HEADS = 8
HG_D = 128
MLA_HEADS = 8
QK_NOPE = 128
QK_ROPE = 64
V_HEAD = 128
Q_LORA = 256
KV_LORA = 256
ROPE_THETA = 10000.0

V7X_LANES = 128
V7X_VMEM_LIMIT_BYTES = 56 * 1024 * 1024

INPROJ_ROWS = 512
INPROJ_SPLIT = 2
HG_BLOCK = 64
HG_STEP = 1024
HG_SAFE_EXPONENT = 60.0
ATTN_Q = 256
ATTN_K = 256
ATTN_GROUP = 2
ATTN_SKEW = 8
OUT_ROWS = 512
OUT_SPLIT = 2

F32 = jnp.float32
BF16 = jnp.bfloat16


def _dot(a, b):
    return jnp.dot(a, b, preferred_element_type=F32)


def _dot_nt(a, b):
    return lax.dot_general(a, b, (((1,), (1,)), ((), ())), preferred_element_type=F32)


def _dot_tn(a, b):
    return lax.dot_general(a, b, (((0,), (0,)), ((), ())), preferred_element_type=F32)


def _rms(x, g):
    ms = jnp.mean(x * x, axis=-1, keepdims=True)
    return x * lax.rsqrt(ms + NORM_EPS) * g


def _sigmoid(x):
    return 1.0 / (1.0 + jnp.exp(-x))


def _rope(slab, cos, sin):
    half = slab.shape[1] // 2
    return slab * cos + pltpu.roll(slab, half, 1) * sin


def _resident(shape):
    return pl.BlockSpec(shape, lambda *_: (0,) * len(shape), pipeline_mode=pl.Buffered(1))


def _cumsum_blocks(x, block):
    pos = lax.broadcasted_iota(jnp.int32, x.shape, 0) % block
    s = 1
    while s < block:
        x = x + jnp.where(pos >= s, pltpu.roll(x, s, 0), 0.0)
        s *= 2
    return x


def _inproj_kernel(x_ref, gpre_ref, wmain_ref, wpe_ref,
                   wgate_ref, bgate_ref, lbl_ref, qg_ref, kvg_ref, wqn_ref, wqp_ref,
                   wkn_ref, wv_ref, cos_ref, sin_ref,
                   qi_ref, kl_ref, dec_ref, vh_ref, og_ref, gates_ref,
                   q_ref, kn_ref, kp_ref, va_ref, *intra_refs, scale, hg_block, split):
    rows, d = x_ref.shape
    whq_ref, whf_ref, whi_ref, whg_ref = (wmain_ref.at[:, j * d:(j + 1) * d] for j in range(4))
    wc_ref = wmain_ref.at[:, 4 * d:4 * d + Q_LORA + KV_LORA]
    part_rows = rows // split
    parts = [slice(i * part_rows, (i + 1) * part_rows) for i in range(split)]
    nblk = part_rows // hg_block
    mid = hg_block // 2 - 1
    pair_w = 2 * HG_D

    lbl = lbl_ref[...]
    e = jnp.exp(lbl - jnp.max(lbl, axis=0, keepdims=True))
    lb = e[0:1] / jnp.sum(e, axis=0, keepdims=True)

    stage1 = []
    for r in parts:
        u = _rms(x_ref[r, :], gpre_ref[...]).astype(BF16)
        hq = _dot(u, whq_ref[...])
        hf = _dot(u, whf_ref[...])
        vh_ref[r, :] = _dot(u, whi_ref[...]).astype(BF16)
        pe = _dot(u, wpe_ref[...])
        cmla = _dot(u, wc_ref[...])
        stage1.append((u, hq, hf, pe, cmla))

    stage2 = []
    for _, hq, hf, _, _ in stage1:
        q = hq * _sigmoid(hq)
        sg = _sigmoid(hf)
        k = (1.0 - lb) * (1.0 - sg)
        c = _cumsum_blocks(jnp.log(lb + (1.0 - lb) * sg), hg_block)
        stage2.append((q, k, c))

    zero = jnp.zeros((hg_block, HG_D), BF16)
    if intra_refs:
        a_ref, ksc_ref, csc_ref = intra_refs
        tri = (lax.broadcasted_iota(jnp.int32, (hg_block, 2 * hg_block), 0)
               >= lax.broadcasted_iota(jnp.int32, (hg_block, 2 * hg_block), 1) % hg_block)
        worst = jnp.zeros((1, d), F32)
    for idx, r in enumerate(parts):
        u, _, _, pe, cmla = stage1[idx]
        q, k, c = stage2[idx]
        cos = cos_ref[r, :]
        sin = sin_ref[r, :]

        def gate_half(half):
            cols = slice(half * d, (half + 1) * d)
            gates_ref[r, cols] = _sigmoid(
                _dot(u, wgate_ref[:, cols]) + bgate_ref[:, cols]).astype(BF16)

        def out_gate():
            hg = _dot(u, whg_ref[...])
            og_ref[r, :] = (hg * _sigmoid(hg)).astype(BF16)

        def mla_q():
            cq = _rms(cmla[:, :Q_LORA], qg_ref[...]).astype(BF16)
            qn = _dot(cq, wqn_ref[...])
            qpe = _dot(cq, wqp_ref[...])
            for h in range(MLA_HEADS):
                sl = slice(h * V7X_LANES, (h + 1) * V7X_LANES)
                base = 2 * h * V7X_LANES
                q_ref[r, base:base + V7X_LANES] = (qn[:, sl] * scale).astype(BF16)
                q_ref[r, base + V7X_LANES:base + 2 * V7X_LANES] = (_rope(qpe[:, sl], cos, sin)
                                                                    * scale).astype(BF16)

        def mla_kv():
            kp_ref[r, :] = _rope(pe, cos, sin).astype(BF16)
            ckv = _rms(cmla[:, Q_LORA:], kvg_ref[...]).astype(BF16)
            kn_ref[r, :] = _dot(ckv, wkn_ref[...]).astype(BF16)
            va_ref[idx] = _dot_nt(wv_ref[...], ckv).astype(BF16)

        fillers = [out_gate, functools.partial(gate_half, 0), functools.partial(gate_half, 1),
                   mla_q, mla_kv]
        for blk in range(nblk):
            rb = slice(blk * hg_block, (blk + 1) * hg_block)
            ro = slice(r.start + rb.start, r.start + rb.stop)
            cb = c[rb]
            cm = cb[mid:mid + 1]
            cl = cb[hg_block - 1:hg_block]
            qi_ref[ro, :] = (q[rb] * jnp.exp(cb)).astype(BF16)
            kl_ref[ro, :] = (k[rb] * jnp.exp(cl - cb)).astype(BF16)
            dec_ref[idx * nblk + blk] = jnp.exp(cl)
            if intra_refs:
                worst = jnp.maximum(worst, jnp.maximum(cb[0:1] - cm, cm - cl))
                qt = (q[rb] * jnp.exp(cb - cm)).astype(BF16)
                kt = (k[rb] * jnp.exp(cm - cb)).astype(BF16)
                for p in range(HG_HEADS // 2):
                    ls = slice(p * pair_w, (p + 1) * pair_w)
                    kb = jnp.concatenate(
                        [jnp.concatenate([kt[:, ls][:, :HG_D], zero], axis=1),
                         jnp.concatenate([zero, kt[:, ls][:, HG_D:]], axis=1)], axis=0)
                    a = jnp.where(tri, _dot_nt(qt[:, ls], kb), 0.0)
                    a_ref[ro, p * 2 * hg_block:(p + 1) * 2 * hg_block] = a.astype(BF16)
            if fillers:
                fillers.pop(0)()
        while fillers:
            fillers.pop(0)()

    if intra_refs:
        @pl.when(jnp.max(worst) > HG_SAFE_EXPONENT)
        def _exact_intra():
            a_w = HG_HEADS * hg_block
            head_of_lane = lax.broadcasted_iota(jnp.int32, (d, HG_HEADS), 0) // HG_D
            head_sum = (head_of_lane
                        == lax.broadcasted_iota(jnp.int32, (d, HG_HEADS), 1)).astype(F32)
            col_base = lax.broadcasted_iota(jnp.int32, (HG_HEADS, a_w), 0) * hg_block
            col_lane = lax.broadcasted_iota(jnp.int32, (HG_HEADS, a_w), 1)
            causal = (lax.broadcasted_iota(jnp.int32, (hg_block, a_w), 0)
                      >= lax.broadcasted_iota(jnp.int32, (hg_block, a_w), 1) % hg_block)
            for r, (q, k, c) in zip(parts, stage2):
                ksc_ref[...] = k
                csc_ref[...] = c
                for blk in range(nblk):
                    rb = slice(blk * hg_block, (blk + 1) * hg_block)
                    first = r.start + rb.start
                    qb = q[rb]
                    cb = c[rb]

                    def key_column(s, acc):
                        row = rb.start + s
                        w = (qb * jnp.exp(jnp.minimum(cb - csc_ref[pl.ds(row, 1), :], 0.0))
                             * ksc_ref[pl.ds(row, 1), :])
                        cols = _dot(w, head_sum)
                        place = (col_lane == col_base + s).astype(F32)
                        return acc + _dot(cols, place)

                    acc = lax.fori_loop(0, hg_block, key_column,
                                        jnp.zeros((hg_block, a_w), F32))
                    a_ref[first:first + hg_block, :] = jnp.where(causal, acc, 0.0).astype(BF16)


def _inproj(x2d, rows, pos_blocks, params, cos_t, sin_t, with_intra):
    t, d = x2d.shape
    n = t // rows
    row = lambda w: pl.BlockSpec((rows, w), lambda i: (i, 0))
    pos = pl.BlockSpec((rows, V7X_LANES), lambda i: (i % pos_blocks, 0))
    w_specs = [_resident(p.shape) for p in params]
    main_cols = 4 * d + Q_LORA + KV_LORA
    w_specs[1] = pl.BlockSpec((d, main_cols), lambda i: (0, 0), pipeline_mode=pl.Buffered(1))
    hg_block = min(HG_BLOCK, rows)
    nb = rows // hg_block
    split = INPROJ_SPLIT if rows % (INPROJ_SPLIT * hg_block) == 0 else 1
    vt_w = MLA_HEADS * V_HEAD
    lead = lambda *shape: pl.BlockSpec((None,) + shape, lambda i: (i,) + (0,) * len(shape))
    outs = [(row(d), (t, d), BF16), (row(d), (t, d), BF16),
            (lead(nb, 1, d), (n, nb, 1, d), F32),
            (row(d), (t, d), BF16), (row(d), (t, d), BF16), (row(2 * d), (t, 2 * d), BF16),
            (row(2 * d), (t, 2 * d), BF16), (row(d), (t, d), BF16),
            (row(V7X_LANES), (t, V7X_LANES), BF16),
            (lead(split, vt_w, rows // split), (n, split, vt_w, rows // split), BF16)]
    scratch = []
    if with_intra:
        a_w = HG_HEADS * hg_block
        outs.append((row(a_w), (t, a_w), BF16))
        scratch = [pltpu.VMEM((rows // split, d), F32), pltpu.VMEM((rows // split, d), F32)]
    scale = (QK_NOPE + QK_ROPE) ** -0.5 * math.log2(math.e)
    return pl.pallas_call(
        functools.partial(_inproj_kernel, scale=scale, hg_block=hg_block, split=split),
        grid=(n,),
        in_specs=[row(d)] + w_specs + [pos, pos],
        out_specs=[o[0] for o in outs],
        out_shape=[jax.ShapeDtypeStruct(o[1], o[2]) for o in outs],
        scratch_shapes=scratch,
        compiler_params=pltpu.CompilerParams(
            dimension_semantics=("parallel",), vmem_limit_bytes=V7X_VMEM_LIMIT_BYTES),
        name="inproj",
    )(x2d, *params, cos_t, sin_t)


def _hgrn_kernel(a_ref, qi_ref, kl_ref, dec_ref, v_ref, og_ref, klm_ref, vm_ref, g_ref,
                 o_ref, st_ref):
    n = pl.program_id(1)
    c_blk = qi_ref.shape[0]
    hs = [slice(h * HG_D, (h + 1) * HG_D) for h in range(HG_HEADS)]

    @pl.when(n == 0)
    def _init():
        vm = vm_ref[...]
        klm = klm_ref[...]
        for h, sl in enumerate(hs):
            st_ref[h] = _dot_tn(vm[:, sl], klm[:, sl])

    g = g_ref[...]
    zero = jnp.zeros((HG_BLOCK, HG_D), BF16)
    for blk in range(c_blk // HG_BLOCK):
        r = slice(blk * HG_BLOCK, (blk + 1) * HG_BLOCK)
        a = a_ref[r, :]
        qi = qi_ref[r, :]
        kl = kl_ref[r, :]
        v = v_ref[r, :]
        dec = dec_ref[blk]
        outs = []
        for p in range(HG_HEADS // 2):
            ha, hb = hs[2 * p], hs[2 * p + 1]
            vb = jnp.concatenate([jnp.concatenate([v[:, ha], zero], axis=1),
                                  jnp.concatenate([zero, v[:, hb]], axis=1)], axis=0)
            o_pair = _dot(a[:, p * 2 * HG_BLOCK:(p + 1) * 2 * HG_BLOCK], vb)
            for e, sl in enumerate((ha, hb)):
                h = 2 * p + e
                st = st_ref[h]
                outs.append(o_pair[:, e * HG_D:(e + 1) * HG_D]
                            + _dot_nt(qi[:, sl], st.astype(BF16)))
                st_ref[h] = st * dec[:, sl] + _dot_tn(v[:, sl], kl[:, sl])
        for o, sl in zip(outs, hs):
            o_ref[r, sl] = (_rms(o, g) * og_ref[r, sl].astype(F32)).astype(BF16)


def _hgrn(a, qi, kl, dec, vh, og, klm, vm, g, batch, seq):
    w = qi.shape[-1]
    nstep = seq // HG_STEP
    sub = HG_STEP // HG_BLOCK
    tok = lambda width: pl.BlockSpec((None, HG_STEP, width), lambda b, n: (b, n, 0))
    meta = pl.BlockSpec((N_META, w), lambda b, n: (0, 0))
    r3 = lambda x: x.reshape(batch, seq, x.shape[-1])
    return pl.pallas_call(
        _hgrn_kernel,
        grid=(batch, nstep),
        in_specs=[tok(a.shape[-1]), tok(w), tok(w),
                  pl.BlockSpec((None, sub, 1, w), lambda b, n: (b, n, 0, 0)),
                  tok(w), tok(w), meta, meta,
                  pl.BlockSpec((1, HG_D), lambda b, n: (0, 0))],
        out_specs=tok(w),
        out_shape=jax.ShapeDtypeStruct((batch, seq, w), BF16),
        scratch_shapes=[pltpu.VMEM((HG_HEADS, HG_D, HG_D), F32)],
        compiler_params=pltpu.CompilerParams(
            dimension_semantics=("parallel", "arbitrary"),
            vmem_limit_bytes=V7X_VMEM_LIMIT_BYTES),
        name="hgrn2",
    )(r3(a), r3(qi), r3(kl), dec.reshape(batch, seq // HG_BLOCK, 1, w), r3(vh), r3(og), klm,
      vm, g)


def _attn_kernel(q_ref, kn_ref, kp_ref, vt_ref, knm_ref, kpm_ref, vtm_ref, o_ref,
                 m_ref, l_ref, acc_ref):
    for sub in range(ATTN_GROUP):
        _attn_query_block(ATTN_GROUP * pl.program_id(1) + sub, sub % 2 == 1,
                          q_ref.at[pl.ds(sub * ATTN_Q, ATTN_Q), :], kn_ref, kp_ref, vt_ref,
                          knm_ref, kpm_ref, vtm_ref, o_ref.at[sub], m_ref, l_ref, acc_ref)


def _attn_query_block(i, i_is_odd, q_ref, kn_ref, kp_ref, vt_ref, knm_ref, kpm_ref, vtm_ref,
                      o_ref, m_ref, l_ref, acc_ref):
    tq = q_ref.shape[0]
    hs = [slice(h * V7X_LANES, (h + 1) * V7X_LANES) for h in range(MLA_HEADS)]

    def q_of(h):
        return q_ref[:, 2 * h * V7X_LANES:2 * (h + 1) * V7X_LANES]

    def with_max(s):
        return s, jnp.max(s, axis=0, keepdims=True)

    def update(carry, scored, vt_list):
        m, l, acc = carry
        s_list = [s for s, _ in scored]
        m_new = m
        for _, s_max in scored:
            m_new = jnp.maximum(m_new, s_max)
        alpha = jnp.exp2(m - m_new)
        l = alpha * l
        acc = alpha * acc
        for s, vt in zip(s_list, vt_list):
            p = jnp.exp2(s - m_new)
            l = l + jnp.sum(p, axis=0, keepdims=True)
            acc = acc + _dot(vt, p.astype(BF16))
        return m_new, l, acc

    def skewed(score_fn, consume_fn):
        scores = []
        for h in range(MLA_HEADS + ATTN_SKEW):
            if h < MLA_HEADS:
                scores.append(score_fn(h))
            g = h - ATTN_SKEW
            if g >= 0:
                consume_fn(g, scores[g])

    for h in range(MLA_HEADS):
        m_ref[h] = jnp.full((1, tq), -jnp.inf, F32)
        l_ref[h] = jnp.zeros((1, tq), F32)
        acc_ref[h] = jnp.zeros((V_HEAD, tq), F32)

    def kv_step(blocks):
        starts = [pl.multiple_of(b * ATTN_K, ATTN_K) for b in blocks]
        kps = [kp_ref[pl.ds(ks, ATTN_K), :] for ks in starts]

        def score(h):
            return [with_max(_dot_nt(
                jnp.concatenate([kn_ref[pl.ds(ks, ATTN_K), hs[h]], kp], axis=1), q_of(h)))
                for ks, kp in zip(starts, kps)]

        def consume(g, s_list):
            m, l, acc = update((m_ref[g], l_ref[g], acc_ref[g]), s_list,
                               [vt_ref[b, hs[g], :] for b in blocks])
            m_ref[g] = m
            l_ref[g] = l
            acc_ref[g] = acc

        skewed(score, consume)

    def pair_step(j2, _):
        kv_step([2 * j2, 2 * j2 + 1])
        return 0

    lax.fori_loop(0, i // 2, pair_step, 0)

    def last_step(full_blocks):
        blocks = list(full_blocks) + [i]
        starts = [pl.multiple_of(b * ATTN_K, ATTN_K) for b in blocks]
        kps = [kp_ref[pl.ds(ks, ATTN_K), :] for ks in starts]
        kpm = kpm_ref[...]
        keep = (lax.broadcasted_iota(jnp.int32, (ATTN_K, tq), 0)
                <= lax.broadcasted_iota(jnp.int32, (ATTN_K, tq), 1))

        def score(h):
            s = [_dot_nt(jnp.concatenate([kn_ref[pl.ds(ks, ATTN_K), hs[h]], kp], axis=1),
                         q_of(h)) for ks, kp in zip(starts, kps)]
            s[-1] = jnp.where(keep, s[-1], -jnp.inf)
            km = jnp.concatenate([knm_ref[:, hs[h]], kpm], axis=1)
            return [with_max(t) for t in s + [_dot_nt(km, q_of(h))]]

        def finish(g, s_list):
            _, l, acc = update((m_ref[g], l_ref[g], acc_ref[g]), s_list,
                               [vt_ref[b, hs[g], :] for b in blocks] + [vtm_ref[hs[g], :]])
            o_ref[hs[g], :] = (acc / l).astype(BF16)

        skewed(score, finish)

    last_step([i - 1] if i_is_odd else [])


def _attention(q, kn, kp, vt, knm, kpm, vtm, batch, seq):
    assert ATTN_Q == ATTN_K == INPROJ_ROWS // INPROJ_SPLIT and ATTN_GROUP % 2 == 0
    w = kn.shape[-1]
    nq = seq // ATTN_Q
    r3 = lambda a: a.reshape(batch, seq, a.shape[-1])
    qblk = pl.BlockSpec((None, ATTN_GROUP * ATTN_Q, 2 * w), lambda b, i: (b, i, 0))
    oblk = pl.BlockSpec((None, ATTN_GROUP, w, ATTN_Q), lambda b, i: (b, i, 0, 0))
    kfull = pl.BlockSpec((None, seq, w), lambda b, i: (b, 0, 0))
    kpfull = pl.BlockSpec((None, seq, V7X_LANES), lambda b, i: (b, 0, 0))
    vtfull = pl.BlockSpec((None, nq, w, ATTN_K), lambda b, i: (b, 0, 0, 0))
    mh = pl.BlockSpec((N_META, w), lambda b, i: (0, 0))
    mp = pl.BlockSpec((N_META, V7X_LANES), lambda b, i: (0, 0))
    mvt = pl.BlockSpec((w, N_META), lambda b, i: (0, 0))
    return pl.pallas_call(
        _attn_kernel,
        grid=(batch, nq // ATTN_GROUP),
        in_specs=[qblk, kfull, kpfull, vtfull, mh, mp, mvt],
        out_specs=oblk,
        out_shape=jax.ShapeDtypeStruct((batch, nq, w, ATTN_Q), BF16),
        scratch_shapes=[pltpu.VMEM((MLA_HEADS, 1, ATTN_Q), F32),
                        pltpu.VMEM((MLA_HEADS, 1, ATTN_Q), F32),
                        pltpu.VMEM((MLA_HEADS, V_HEAD, ATTN_Q), F32)],
        compiler_params=pltpu.CompilerParams(
            dimension_semantics=("parallel", "arbitrary"),
            vmem_limit_bytes=V7X_VMEM_LIMIT_BYTES),
        name="mla_attn",
    )(r3(q), r3(kn), r3(kp), vt.reshape(batch, nq, w, ATTN_K), knm, kpm,
      vtm.reshape(w, N_META))


def _out_kernel(x_ref, oh_ref, oat_ref, gates_ref, who_ref, wmo_ref, wout_ref, gmix_ref,
                gfpre_ref, wfin_ref, wfout_ref, gfpost_ref, y_ref):
    rows, d = x_ref.shape
    hidden = wfout_ref.shape[0]
    halves = [slice(i * rows // OUT_SPLIT, (i + 1) * rows // OUT_SPLIT) for i in range(OUT_SPLIT)]

    def merge(idx, r):
        ya = _dot(oh_ref[r, :], who_ref[...])
        yb = _dot_tn(oat_ref[idx], wmo_ref[...])
        gates = gates_ref[r, :].astype(F32)
        return (gates[:, :d] * ya + gates[:, d:] * yb).astype(BF16)

    def mix_residual(r, merged):
        h1 = x_ref[r, :] + _rms(_dot(merged, wout_ref[...]), gmix_ref[...])
        return h1, _rms(h1, gfpre_ref[...]).astype(BF16)

    def ffn_act(u):
        gu = _dot(u, wfin_ref[...])
        gt = gu[:, :hidden]
        return (gt * _sigmoid(gt) * gu[:, hidden:]).astype(BF16)

    merged = [merge(idx, r) for idx, r in enumerate(halves)]
    h1_u = [mix_residual(r, m) for r, m in zip(halves, merged)]
    acts = [ffn_act(u) for _, u in h1_u]
    for r, (h1, _), act in zip(halves, h1_u, acts):
        y_ref[r, :] = h1 + _rms(_dot(act, wfout_ref[...]), gfpost_ref[...])


def _out_block(x2d, oh, oa, gates, params):
    t, d = x2d.shape
    n = t // OUT_ROWS
    row = lambda w: pl.BlockSpec((OUT_ROWS, w), lambda i: (i, 0))
    return pl.pallas_call(
        _out_kernel,
        grid=(n,),
        in_specs=[row(d), row(d),
                  pl.BlockSpec((OUT_SPLIT, oa.shape[1], OUT_ROWS // OUT_SPLIT), lambda i: (i, 0, 0)),
                  row(2 * d)] + [_resident(p.shape) for p in params],
        out_specs=row(d),
        out_shape=jax.ShapeDtypeStruct((t, d), F32),
        compiler_params=pltpu.CompilerParams(
            dimension_semantics=("parallel",), vmem_limit_bytes=V7X_VMEM_LIMIT_BYTES),
        name="merge_ffn",
    )(x2d, oh, oa, gates, *params)


def _rope_tables(length):
    pos = jnp.arange(length, dtype=F32)
    inv_freq = 1.0 / (ROPE_THETA ** (jnp.arange(0, QK_ROPE, 2, dtype=F32) / QK_ROPE))
    ang = pos[:, None] * inv_freq[None, :]
    cos, sin = jnp.cos(ang), jnp.sin(ang)
    zero = jnp.zeros((length, V7X_LANES - QK_ROPE), F32)
    return (jnp.concatenate([cos, cos, zero], axis=1),
            jnp.concatenate([-sin, sin, zero], axis=1))


def _swap_halves(w):
    half = w.shape[-1] // 2
    return jnp.concatenate([w[..., half:], w[..., :half]], axis=-1)


def kernel(x, meta_tokens, w_in, b_gate, lb_logits, hg_norm_g, w_hg_o, q_a_norm_g, w_q_b,
           kv_a_norm_g, w_kv_b, w_mla_o, w_out, mix_pre_g, mix_post_g, ffn_pre_g, ffn_post_g,
           w_ffn_in, w_ffn_out):
    batch, seq, d = x.shape
    assert w_in.shape[0] == 1, "single-layer block"
    assert 2 * QK_ROPE == V7X_LANES, "rotary slab [t | rot(t)] must fill one lane tile"
    assert seq % INPROJ_ROWS == 0 and seq % ATTN_Q == 0 and seq % HG_STEP == 0
    assert INPROJ_ROWS % HG_BLOCK == 0 and HG_STEP % HG_BLOCK == 0
    assert (batch * seq) % OUT_ROWS == 0
    hgw = HG_HEADS * HG_D
    row = lambda a: a.reshape(1, -1).astype(F32)

    bf = lambda a: a.astype(BF16)
    assert hgw == d, "in-projection column layout assumes HG_WIDTH == D_MODEL"
    wi = bf(w_in[0])
    o = 4 * hgw + Q_LORA + KV_LORA
    wkpe = wi[:, o:o + QK_ROPE]
    wgate = wi[:, o + QK_ROPE:]
    wpe = jnp.concatenate([wkpe, _swap_halves(wkpe)], axis=1)
    wq = w_q_b[0].reshape(Q_LORA, MLA_HEADS, QK_NOPE + QK_ROPE)
    wqn = wq[:, :, :QK_NOPE].reshape(Q_LORA, MLA_HEADS * QK_NOPE)
    wq_pe = wq[:, :, QK_NOPE:]
    wqp = jnp.concatenate([wq_pe, _swap_halves(wq_pe)], axis=2).reshape(
        Q_LORA, MLA_HEADS * V7X_LANES)
    wkv = w_kv_b[0].reshape(KV_LORA, MLA_HEADS, QK_NOPE + V_HEAD)
    wkn = wkv[:, :, :QK_NOPE].reshape(KV_LORA, MLA_HEADS * QK_NOPE)
    wv = wkv[:, :, QK_NOPE:].reshape(KV_LORA, MLA_HEADS * V_HEAD).T
    inproj_params = [row(mix_pre_g[0]), wi, wpe,
                     wgate, row(b_gate[0]), lb_logits.astype(F32), row(q_a_norm_g[0]),
                     row(kv_a_norm_g[0]), bf(wqn), bf(wqp), bf(wkn), bf(wv)]
    cos_t, sin_t = _rope_tables(N_META + seq)

    m_out = _inproj(meta_tokens.astype(F32), N_META, 1, inproj_params,
                    cos_t[:N_META], sin_t[:N_META], with_intra=False)
    _, kl_m, _, vh_m, _, _, _, kn_m, kp_m, va_m = m_out

    x2d = x.reshape(batch * seq, d)
    (qi, kl, dec, vh, og, gates, q, kn, kp, va, a_hg) = _inproj(
        x2d, INPROJ_ROWS, seq // INPROJ_ROWS, inproj_params, cos_t[N_META:], sin_t[N_META:],
        with_intra=True)

    o_hg = _hgrn(a_hg, qi, kl, dec, vh, og, kl_m, vh_m, row(hg_norm_g[0]), batch, seq)
    o_at = _attention(q, kn, kp, va, kn_m, kp_m, va_m, batch, seq)

    out_params = [bf(w_hg_o[0]), bf(w_mla_o[0]), bf(w_out[0]), row(mix_post_g[0]),
                  row(ffn_pre_g[0]), bf(w_ffn_in[0]), bf(w_ffn_out[0]), row(ffn_post_g[0])]
    assert OUT_ROWS // OUT_SPLIT == ATTN_Q
    y = _out_block(x2d, o_hg.reshape(batch * seq, hgw),
                   o_at.reshape(batch * (seq // ATTN_Q), -1, ATTN_Q), gates, out_params)
    return y.reshape(batch, seq, d)
```

```python
import functools
import math

import jax
import jax.numpy as jnp
from jax import lax
from jax.experimental import pallas as pl
from jax.experimental.pallas import tpu as pltpu

N_META = 16
NORM_EPS = 1e-6
HG_HEADS = 8
HG_D = 128
MLA_HEADS = 8
QK_NOPE = 128
QK_ROPE = 64
V_HEAD = 128
Q_LORA = 256
KV_LORA = 256
ROPE_THETA = 10000.0

V7X_LANES = 128
V7X_VMEM_LIMIT_BYTES = 56 * 1024 * 1024

INPROJ_ROWS = 512
INPROJ_SPLIT = 2
HG_BLOCK = 64
HG_STEP = 1024
HG_SAFE_EXPONENT = 60.0
ATTN_Q = 256
ATTN_K = 256
ATTN_GROUP = 2
ATTN_SKEW = 8
OUT_ROWS = 512
OUT_SPLIT = 2

F32 = jnp.float32
BF16 = jnp.bfloat16


def _dot(a, b):
    return jnp.dot(a, b, preferred_element_type=F32)


def _dot_nt(a, b):
    return lax.dot_general(a, b, (((1,), (1,)), ((), ())), preferred_element_type=F32)


def _dot_tn(a, b):
    return lax.dot_general(a, b, (((0,), (0,)), ((), ())), preferred_element_type=F32)


def _rms(x, g):
    ms = jnp.mean(x * x, axis=-1, keepdims=True)
    return x * lax.rsqrt(ms + NORM_EPS) * g


def _sigmoid(x):
    return 1.0 / (1.0 + jnp.exp(-x))


def _rope(slab, cos, sin):
    half = slab.shape[1] // 2
    return slab * cos + pltpu.roll(slab, half, 1) * sin


def _cast_kernel(x_ref, o_ref):
    o_ref[...] = x_ref[...].astype(o_ref.dtype)


def _to_bf16(x, rows=128):
    r, c = x.shape
    spec = pl.BlockSpec((rows, c), lambda i: (i, 0))
    return pl.pallas_call(_cast_kernel, grid=(r // rows,), in_specs=[spec], out_specs=spec,
                          out_shape=jax.ShapeDtypeStruct((r, c), BF16), name="cast_bf16")(x)


def _resident(shape):
    return pl.BlockSpec(shape, lambda *_: (0,) * len(shape), pipeline_mode=pl.Buffered(1))


def _cumsum_blocks(x, block):
    pos = lax.broadcasted_iota(jnp.int32, x.shape, 0) % block
    s = 1
    while s < block:
        x = x + jnp.where(pos >= s, pltpu.roll(x, s, 0), 0.0)
        s *= 2
    return x


def _inproj_kernel(x_ref, gpre_ref, wmain_ref, wpe_ref,
                   wgate_ref, bgate_ref, lbl_ref, qg_ref, kvg_ref, wqn_ref, wqp_ref,
                   wkn_ref, wv_ref, cos_ref, sin_ref,
                   qi_ref, kl_ref, dec_ref, vh_ref, og_ref, gates_ref,
                   q_ref, kn_ref, kp_ref, va_ref, *intra_refs, scale, hg_block, split):
    rows, d = x_ref.shape
    whq_ref, whf_ref, whi_ref, whg_ref = (wmain_ref.at[:, j * d:(j + 1) * d] for j in range(4))
    wc_ref = wmain_ref.at[:, 4 * d:4 * d + Q_LORA + KV_LORA]
    part_rows = rows // split
    parts = [slice(i * part_rows, (i + 1) * part_rows) for i in range(split)]
    nblk = part_rows // hg_block
    mid = hg_block // 2 - 1
    pair_w = 2 * HG_D

    lbl = lbl_ref[...]
    e = jnp.exp(lbl - jnp.max(lbl, axis=0, keepdims=True))
    lb = e[0:1] / jnp.sum(e, axis=0, keepdims=True)

    stage1 = []
    for r in parts:
        u = _rms(x_ref[r, :], gpre_ref[...]).astype(BF16)
        hq = _dot(u, whq_ref[...])
        hf = _dot(u, whf_ref[...])
        vh_ref[r, :] = _dot(u, whi_ref[...]).astype(BF16)
        pe = _dot(u, wpe_ref[...])
        cmla = _dot(u, wc_ref[...])
        stage1.append((u, hq, hf, pe, cmla))

    stage2 = []
    for _, hq, hf, _, _ in stage1:
        q = hq * _sigmoid(hq)
        sg = _sigmoid(hf)
        k = (1.0 - lb) * (1.0 - sg)
        c = _cumsum_blocks(jnp.log(lb + (1.0 - lb) * sg), hg_block)
        stage2.append((q, k, c))

    zero = jnp.zeros((hg_block, HG_D), BF16)
    if intra_refs:
        a_ref, ksc_ref, csc_ref = intra_refs
        tri = (lax.broadcasted_iota(jnp.int32, (hg_block, 2 * hg_block), 0)
               >= lax.broadcasted_iota(jnp.int32, (hg_block, 2 * hg_block), 1) % hg_block)
        worst = jnp.zeros((1, d), F32)
    for idx, r in enumerate(parts):
        u, _, _, pe, cmla = stage1[idx]
        q, k, c = stage2[idx]
        cos = cos_ref[r, :]
        sin = sin_ref[r, :]

        def gate_half(half):
            cols = slice(half * d, (half + 1) * d)
            gates_ref[r, cols] = _sigmoid(
                _dot(u, wgate_ref[:, cols]) + bgate_ref[:, cols]).astype(BF16)

        def out_gate():
            hg = _dot(u, whg_ref[...])
            og_ref[r, :] = (hg * _sigmoid(hg)).astype(BF16)

        def mla_q():
            cq = _rms(cmla[:, :Q_LORA], qg_ref[...]).astype(BF16)
            qn = _dot(cq, wqn_ref[...])
            qpe = _dot(cq, wqp_ref[...])
            for h in range(MLA_HEADS):
                sl = slice(h * V7X_LANES, (h + 1) * V7X_LANES)
                base = 2 * h * V7X_LANES
                q_ref[r, base:base + V7X_LANES] = (qn[:, sl] * scale).astype(BF16)
                q_ref[r, base + V7X_LANES:base + 2 * V7X_LANES] = (_rope(qpe[:, sl], cos, sin)
                                                                    * scale).astype(BF16)

        def mla_kv():
            kp_ref[r, :] = _rope(pe, cos, sin).astype(BF16)
            ckv = _rms(cmla[:, Q_LORA:], kvg_ref[...]).astype(BF16)
            kn_ref[r, :] = _dot(ckv, wkn_ref[...]).astype(BF16)
            va_ref[idx] = _dot_nt(wv_ref[...], ckv).astype(BF16)

        fillers = [out_gate, functools.partial(gate_half, 0), functools.partial(gate_half, 1),
                   mla_q, mla_kv]
        for blk in range(nblk):
            rb = slice(blk * hg_block, (blk + 1) * hg_block)
            ro = slice(r.start + rb.start, r.start + rb.stop)
            cb = c[rb]
            cm = cb[mid:mid + 1]
            cl = cb[hg_block - 1:hg_block]
            qi_ref[ro, :] = (q[rb] * jnp.exp(cb)).astype(BF16)
            kl_ref[ro, :] = (k[rb] * jnp.exp(cl - cb)).astype(BF16)
            dec_ref[idx * nblk + blk] = jnp.exp(cl)
            if intra_refs:
                worst = jnp.maximum(worst, jnp.maximum(cb[0:1] - cm, cm - cl))
                qt = (q[rb] * jnp.exp(cb - cm)).astype(BF16)
                kt = (k[rb] * jnp.exp(cm - cb)).astype(BF16)
                for p in range(HG_HEADS // 2):
                    ls = slice(p * pair_w, (p + 1) * pair_w)
                    kb = jnp.concatenate(
                        [jnp.concatenate([kt[:, ls][:, :HG_D], zero], axis=1),
                         jnp.concatenate([zero, kt[:, ls][:, HG_D:]], axis=1)], axis=0)
                    a = jnp.where(tri, _dot_nt(qt[:, ls], kb), 0.0)
                    a_ref[ro, p * 2 * hg_block:(p + 1) * 2 * hg_block] = a.astype(BF16)
            if fillers:
                fillers.pop(0)()
        while fillers:
            fillers.pop(0)()

    if intra_refs:
        @pl.when(jnp.max(worst) > HG_SAFE_EXPONENT)
        def _exact_intra():
            a_w = HG_HEADS * hg_block
            head_of_lane = lax.broadcasted_iota(jnp.int32, (d, HG_HEADS), 0) // HG_D
            head_sum = (head_of_lane
                        == lax.broadcasted_iota(jnp.int32, (d, HG_HEADS), 1)).astype(F32)
            col_base = lax.broadcasted_iota(jnp.int32, (HG_HEADS, a_w), 0) * hg_block
            col_lane = lax.broadcasted_iota(jnp.int32, (HG_HEADS, a_w), 1)
            causal = (lax.broadcasted_iota(jnp.int32, (hg_block, a_w), 0)
                      >= lax.broadcasted_iota(jnp.int32, (hg_block, a_w), 1) % hg_block)
            for r, (q, k, c) in zip(parts, stage2):
                ksc_ref[...] = k
                csc_ref[...] = c
                for blk in range(nblk):
                    rb = slice(blk * hg_block, (blk + 1) * hg_block)
                    first = r.start + rb.start
                    qb = q[rb]
                    cb = c[rb]

                    def key_column(s, acc):
                        row = rb.start + s
                        w = (qb * jnp.exp(jnp.minimum(cb - csc_ref[pl.ds(row, 1), :], 0.0))
                             * ksc_ref[pl.ds(row, 1), :])
                        cols = _dot(w, head_sum)
                        place = (col_lane == col_base + s).astype(F32)
                        return acc + _dot(cols, place)

                    acc = lax.fori_loop(0, hg_block, key_column,
                                        jnp.zeros((hg_block, a_w), F32))
                    a_ref[first:first + hg_block, :] = jnp.where(causal, acc, 0.0).astype(BF16)


def _inproj(x2d, rows, pos_blocks, params, cos_t, sin_t, with_intra):
    t, d = x2d.shape
    n = t // rows
    row = lambda w: pl.BlockSpec((rows, w), lambda i: (i, 0))
    pos = pl.BlockSpec((rows, V7X_LANES), lambda i: (i % pos_blocks, 0))
    w_specs = [_resident(p.shape) for p in params]
    main_cols = 4 * d + Q_LORA + KV_LORA
    w_specs[1] = pl.BlockSpec((d, main_cols), lambda i: (0, 0), pipeline_mode=pl.Buffered(1))
    hg_block = min(HG_BLOCK, rows)
    nb = rows // hg_block
    split = INPROJ_SPLIT if rows % (INPROJ_SPLIT * hg_block) == 0 else 1
    vt_w = MLA_HEADS * V_HEAD
    lead = lambda *shape: pl.BlockSpec((None,) + shape, lambda i: (i,) + (0,) * len(shape))
    outs = [(row(d), (t, d), BF16), (row(d), (t, d), BF16),
            (lead(nb, 1, d), (n, nb, 1, d), F32),
            (row(d), (t, d), BF16), (row(d), (t, d), BF16), (row(2 * d), (t, 2 * d), BF16),
            (row(2 * d), (t, 2 * d), BF16), (row(d), (t, d), BF16),
            (row(V7X_LANES), (t, V7X_LANES), BF16),
            (lead(split, vt_w, rows // split), (n, split, vt_w, rows // split), BF16)]
    scratch = []
    if with_intra:
        a_w = HG_HEADS * hg_block
        outs.append((row(a_w), (t, a_w), BF16))
        scratch = [pltpu.VMEM((rows // split, d), F32), pltpu.VMEM((rows // split, d), F32)]
    scale = (QK_NOPE + QK_ROPE) ** -0.5 * math.log2(math.e)
    return pl.pallas_call(
        functools.partial(_inproj_kernel, scale=scale, hg_block=hg_block, split=split),
        grid=(n,),
        in_specs=[row(d)] + w_specs + [pos, pos],
        out_specs=[o[0] for o in outs],
        out_shape=[jax.ShapeDtypeStruct(o[1], o[2]) for o in outs],
        scratch_shapes=scratch,
        compiler_params=pltpu.CompilerParams(
            dimension_semantics=("parallel",), vmem_limit_bytes=V7X_VMEM_LIMIT_BYTES),
        name="inproj",
    )(x2d, *params, cos_t, sin_t)


def _hgrn_kernel(a_ref, qi_ref, kl_ref, dec_ref, v_ref, og_ref, klm_ref, vm_ref, g_ref,
                 o_ref, st_ref):
    n = pl.program_id(1)
    c_blk = qi_ref.shape[0]
    hs = [slice(h * HG_D, (h + 1) * HG_D) for h in range(HG_HEADS)]

    @pl.when(n == 0)
    def _init():
        vm = vm_ref[...]
        klm = klm_ref[...]
        for h, sl in enumerate(hs):
            st_ref[h] = _dot_tn(vm[:, sl], klm[:, sl])

    g = g_ref[...]
    zero = jnp.zeros((HG_BLOCK, HG_D), BF16)
    for blk in range(c_blk // HG_BLOCK):
        r = slice(blk * HG_BLOCK, (blk + 1) * HG_BLOCK)
        a = a_ref[r, :]
        qi = qi_ref[r, :]
        kl = kl_ref[r, :]
        v = v_ref[r, :]
        dec = dec_ref[blk]
        outs = []
        for p in range(HG_HEADS // 2):
            ha, hb = hs[2 * p], hs[2 * p + 1]
            vb = jnp.concatenate([jnp.concatenate([v[:, ha], zero], axis=1),
                                  jnp.concatenate([zero, v[:, hb]], axis=1)], axis=0)
            o_pair = _dot(a[:, p * 2 * HG_BLOCK:(p + 1) * 2 * HG_BLOCK], vb)
            for e, sl in enumerate((ha, hb)):
                h = 2 * p + e
                st = st_ref[h]
                outs.append(o_pair[:, e * HG_D:(e + 1) * HG_D]
                            + _dot_nt(qi[:, sl], st.astype(BF16)))
                st_ref[h] = st * dec[:, sl] + _dot_tn(v[:, sl], kl[:, sl])
        for o, sl in zip(outs, hs):
            o_ref[r, sl] = (_rms(o, g) * og_ref[r, sl].astype(F32)).astype(BF16)


def _hgrn(a, qi, kl, dec, vh, og, klm, vm, g, batch, seq):
    w = qi.shape[-1]
    nstep = seq // HG_STEP
    sub = HG_STEP // HG_BLOCK
    tok = lambda width: pl.BlockSpec((None, HG_STEP, width), lambda b, n: (b, n, 0))
    meta = pl.BlockSpec((N_META, w), lambda b, n: (0, 0))
    r3 = lambda x: x.reshape(batch, seq, x.shape[-1])
    return pl.pallas_call(
        _hgrn_kernel,
        grid=(batch, nstep),
        in_specs=[tok(a.shape[-1]), tok(w), tok(w),
                  pl.BlockSpec((None, sub, 1, w), lambda b, n: (b, n, 0, 0)),
                  tok(w), tok(w), meta, meta,
                  pl.BlockSpec((1, HG_D), lambda b, n: (0, 0))],
        out_specs=tok(w),
        out_shape=jax.ShapeDtypeStruct((batch, seq, w), BF16),
        scratch_shapes=[pltpu.VMEM((HG_HEADS, HG_D, HG_D), F32)],
        compiler_params=pltpu.CompilerParams(
            dimension_semantics=("parallel", "arbitrary"),
            vmem_limit_bytes=V7X_VMEM_LIMIT_BYTES),
        name="hgrn2",
    )(r3(a), r3(qi), r3(kl), dec.reshape(batch, seq // HG_BLOCK, 1, w), r3(vh), r3(og), klm,
      vm, g)


def _attn_kernel(q_ref, kn_ref, kp_ref, vt_ref, knm_ref, kpm_ref, vtm_ref, o_ref,
                 m_ref, l_ref, acc_ref):
    for sub in range(ATTN_GROUP):
        _attn_query_block(ATTN_GROUP * pl.program_id(1) + sub, sub % 2 == 1,
                          q_ref.at[pl.ds(sub * ATTN_Q, ATTN_Q), :], kn_ref, kp_ref, vt_ref,
                          knm_ref, kpm_ref, vtm_ref, o_ref.at[sub], m_ref, l_ref, acc_ref)


def _attn_query_block(i, i_is_odd, q_ref, kn_ref, kp_ref, vt_ref, knm_ref, kpm_ref, vtm_ref,
                      o_ref, m_ref, l_ref, acc_ref):
    tq = q_ref.shape[0]
    hs = [slice(h * V7X_LANES, (h + 1) * V7X_LANES) for h in range(MLA_HEADS)]

    def q_of(h):
        return q_ref[:, 2 * h * V7X_LANES:2 * (h + 1) * V7X_LANES]

    def with_max(s):
        return s, jnp.max(s, axis=0, keepdims=True)

    def update(carry, scored, vt_list):
        m, l, acc = carry
        s_list = [s for s, _ in scored]
        m_new = m
        for _, s_max in scored:
            m_new = jnp.maximum(m_new, s_max)
        alpha = jnp.exp2(m - m_new)
        l = alpha * l
        acc = alpha * acc
        for s, vt in zip(s_list, vt_list):
            p = jnp.exp2(s - m_new)
            l = l + jnp.sum(p, axis=0, keepdims=True)
            acc = acc + _dot(vt, p.astype(BF16))
        return m_new, l, acc

    def skewed(score_fn, consume_fn):
        scores = []
        for h in range(MLA_HEADS + ATTN_SKEW):
            if h < MLA_HEADS:
                scores.append(score_fn(h))
            g = h - ATTN_SKEW
            if g >= 0:
                consume_fn(g, scores[g])

    for h in range(MLA_HEADS):
        m_ref[h] = jnp.full((1, tq), -jnp.inf, F32)
        l_ref[h] = jnp.zeros((1, tq), F32)
        acc_ref[h] = jnp.zeros((V_HEAD, tq), F32)

    def kv_step(blocks):
        starts = [pl.multiple_of(b * ATTN_K, ATTN_K) for b in blocks]
        kps = [kp_ref[pl.ds(ks, ATTN_K), :] for ks in starts]

        def score(h):
            return [with_max(_dot_nt(
                jnp.concatenate([kn_ref[pl.ds(ks, ATTN_K), hs[h]], kp], axis=1), q_of(h)))
                for ks, kp in zip(starts, kps)]

        def consume(g, s_list):
            m, l, acc = update((m_ref[g], l_ref[g], acc_ref[g]), s_list,
                               [vt_ref[b, hs[g], :] for b in blocks])
            m_ref[g] = m
            l_ref[g] = l
            acc_ref[g] = acc

        skewed(score, consume)

    def pair_step(j2, _):
        kv_step([2 * j2, 2 * j2 + 1])
        return 0

    lax.fori_loop(0, i // 2, pair_step, 0)

    def last_step(full_blocks):
        blocks = list(full_blocks) + [i]
        starts = [pl.multiple_of(b * ATTN_K, ATTN_K) for b in blocks]
        kps = [kp_ref[pl.ds(ks, ATTN_K), :] for ks in starts]
        kpm = kpm_ref[...]
        keep = (lax.broadcasted_iota(jnp.int32, (ATTN_K, tq), 0)
                <= lax.broadcasted_iota(jnp.int32, (ATTN_K, tq), 1))

        def score(h):
            s = [_dot_nt(jnp.concatenate([kn_ref[pl.ds(ks, ATTN_K), hs[h]], kp], axis=1),
                         q_of(h)) for ks, kp in zip(starts, kps)]
            s[-1] = jnp.where(keep, s[-1], -jnp.inf)
            km = jnp.concatenate([knm_ref[:, hs[h]], kpm], axis=1)
            return [with_max(t) for t in s + [_dot_nt(km, q_of(h))]]

        def finish(g, s_list):
            _, l, acc = update((m_ref[g], l_ref[g], acc_ref[g]), s_list,
                               [vt_ref[b, hs[g], :] for b in blocks] + [vtm_ref[hs[g], :]])
            o_ref[hs[g], :] = (acc / l).astype(BF16)

        skewed(score, finish)

    last_step([i - 1] if i_is_odd else [])


def _attention(q, kn, kp, vt, knm, kpm, vtm, batch, seq):
    assert ATTN_Q == ATTN_K == INPROJ_ROWS // INPROJ_SPLIT and ATTN_GROUP % 2 == 0
    w = kn.shape[-1]
    nq = seq // ATTN_Q
    r3 = lambda a: a.reshape(batch, seq, a.shape[-1])
    qblk = pl.BlockSpec((None, ATTN_GROUP * ATTN_Q, 2 * w), lambda b, i: (b, i, 0))
    oblk = pl.BlockSpec((None, ATTN_GROUP, w, ATTN_Q), lambda b, i: (b, i, 0, 0))
    kfull = pl.BlockSpec((None, seq, w), lambda b, i: (b, 0, 0))
    kpfull = pl.BlockSpec((None, seq, V7X_LANES), lambda b, i: (b, 0, 0))
    vtfull = pl.BlockSpec((None, nq, w, ATTN_K), lambda b, i: (b, 0, 0, 0))
    mh = pl.BlockSpec((N_META, w), lambda b, i: (0, 0))
    mp = pl.BlockSpec((N_META, V7X_LANES), lambda b, i: (0, 0))
    mvt = pl.BlockSpec((w, N_META), lambda b, i: (0, 0))
    return pl.pallas_call(
        _attn_kernel,
        grid=(batch, nq // ATTN_GROUP),
        in_specs=[qblk, kfull, kpfull, vtfull, mh, mp, mvt],
        out_specs=oblk,
        out_shape=jax.ShapeDtypeStruct((batch, nq, w, ATTN_Q), BF16),
        scratch_shapes=[pltpu.VMEM((MLA_HEADS, 1, ATTN_Q), F32),
                        pltpu.VMEM((MLA_HEADS, 1, ATTN_Q), F32),
                        pltpu.VMEM((MLA_HEADS, V_HEAD, ATTN_Q), F32)],
        compiler_params=pltpu.CompilerParams(
            dimension_semantics=("parallel", "arbitrary"),
            vmem_limit_bytes=V7X_VMEM_LIMIT_BYTES),
        name="mla_attn",
    )(r3(q), r3(kn), r3(kp), vt.reshape(batch, nq, w, ATTN_K), knm, kpm,
      vtm.reshape(w, N_META))


def _out_kernel(x_ref, oh_ref, oat_ref, gates_ref, who_ref, wmo_ref, wout_ref, gmix_ref,
                gfpre_ref, wfin_ref, wfout_ref, gfpost_ref, y_ref):
    rows, d = x_ref.shape
    hidden = wfout_ref.shape[0]
    halves = [slice(i * rows // OUT_SPLIT, (i + 1) * rows // OUT_SPLIT) for i in range(OUT_SPLIT)]

    def merge(idx, r):
        ya = _dot(oh_ref[r, :], who_ref[...])
        yb = _dot_tn(oat_ref[idx], wmo_ref[...])
        gates = gates_ref[r, :].astype(F32)
        return (gates[:, :d] * ya + gates[:, d:] * yb).astype(BF16)

    def mix_residual(r, merged):
        h1 = x_ref[r, :] + _rms(_dot(merged, wout_ref[...]), gmix_ref[...])
        return h1, _rms(h1, gfpre_ref[...]).astype(BF16)

    def ffn_act(u):
        gu = _dot(u, wfin_ref[...])
        gt = gu[:, :hidden]
        return (gt * _sigmoid(gt) * gu[:, hidden:]).astype(BF16)

    merged = [merge(idx, r) for idx, r in enumerate(halves)]
    h1_u = [mix_residual(r, m) for r, m in zip(halves, merged)]
    acts = [ffn_act(u) for _, u in h1_u]
    for r, (h1, _), act in zip(halves, h1_u, acts):
        y_ref[r, :] = h1 + _rms(_dot(act, wfout_ref[...]), gfpost_ref[...])


def _out_block(x2d, oh, oa, gates, params):
    t, d = x2d.shape
    n = t // OUT_ROWS
    row = lambda w: pl.BlockSpec((OUT_ROWS, w), lambda i: (i, 0))
    return pl.pallas_call(
        _out_kernel,
        grid=(n,),
        in_specs=[row(d), row(d),
                  pl.BlockSpec((OUT_SPLIT, oa.shape[1], OUT_ROWS // OUT_SPLIT), lambda i: (i, 0, 0)),
                  row(2 * d)] + [_resident(p.shape) for p in params],
        out_specs=row(d),
        out_shape=jax.ShapeDtypeStruct((t, d), F32),
        compiler_params=pltpu.CompilerParams(
            dimension_semantics=("parallel",), vmem_limit_bytes=V7X_VMEM_LIMIT_BYTES),
        name="merge_ffn",
    )(x2d, oh, oa, gates, *params)


def _rope_tables(length):
    pos = jnp.arange(length, dtype=F32)
    inv_freq = 1.0 / (ROPE_THETA ** (jnp.arange(0, QK_ROPE, 2, dtype=F32) / QK_ROPE))
    ang = pos[:, None] * inv_freq[None, :]
    cos, sin = jnp.cos(ang), jnp.sin(ang)
    zero = jnp.zeros((length, V7X_LANES - QK_ROPE), F32)
    return (jnp.concatenate([cos, cos, zero], axis=1),
            jnp.concatenate([-sin, sin, zero], axis=1))


def _swap_halves(w):
    half = w.shape[-1] // 2
    return jnp.concatenate([w[..., half:], w[..., :half]], axis=-1)


def kernel(x, meta_tokens, w_in, b_gate, lb_logits, hg_norm_g, w_hg_o, q_a_norm_g, w_q_b,
           kv_a_norm_g, w_kv_b, w_mla_o, w_out, mix_pre_g, mix_post_g, ffn_pre_g, ffn_post_g,
           w_ffn_in, w_ffn_out):
    batch, seq, d = x.shape
    assert w_in.shape[0] == 1, "single-layer block"
    assert 2 * QK_ROPE == V7X_LANES, "rotary slab [t | rot(t)] must fill one lane tile"
    assert seq % INPROJ_ROWS == 0 and seq % ATTN_Q == 0 and seq % HG_STEP == 0
    assert INPROJ_ROWS % HG_BLOCK == 0 and HG_STEP % HG_BLOCK == 0
    assert (batch * seq) % OUT_ROWS == 0
    hgw = HG_HEADS * HG_D
    row = lambda a: a.reshape(1, -1).astype(F32)

    bf = lambda a: a.astype(BF16)
    assert hgw == d, "in-projection column layout assumes HG_WIDTH == D_MODEL"
    wi = _to_bf16(w_in[0])
    o = 4 * hgw + Q_LORA + KV_LORA
    wkpe = wi[:, o:o + QK_ROPE]
    wgate = wi[:, o + QK_ROPE:]
    wpe = jnp.concatenate([wkpe, _swap_halves(wkpe)], axis=1)
    wq = w_q_b[0].reshape(Q_LORA, MLA_HEADS, QK_NOPE + QK_ROPE)
    wqn = wq[:, :, :QK_NOPE].reshape(Q_LORA, MLA_HEADS * QK_NOPE)
    wq_pe = wq[:, :, QK_NOPE:]
    wqp = jnp.concatenate([wq_pe, _swap_halves(wq_pe)], axis=2).reshape(
        Q_LORA, MLA_HEADS * V7X_LANES)
    wkv = w_kv_b[0].reshape(KV_LORA, MLA_HEADS, QK_NOPE + V_HEAD)
    wkn = wkv[:, :, :QK_NOPE].reshape(KV_LORA, MLA_HEADS * QK_NOPE)
    wv = wkv[:, :, QK_NOPE:].reshape(KV_LORA, MLA_HEADS * V_HEAD).T
    inproj_params = [row(mix_pre_g[0]), wi, wpe,
                     wgate, row(b_gate[0]), lb_logits.astype(F32), row(q_a_norm_g[0]),
                     row(kv_a_norm_g[0]), bf(wqn), bf(wqp), bf(wkn), bf(wv)]
    cos_t, sin_t = _rope_tables(N_META + seq)

    m_out = _inproj(meta_tokens.astype(F32), N_META, 1, inproj_params,
                    cos_t[:N_META], sin_t[:N_META], with_intra=False)
    _, kl_m, _, vh_m, _, _, _, kn_m, kp_m, va_m = m_out

    x2d = x.reshape(batch * seq, d)
    (qi, kl, dec, vh, og, gates, q, kn, kp, va, a_hg) = _inproj(
        x2d, INPROJ_ROWS, seq // INPROJ_ROWS, inproj_params, cos_t[N_META:], sin_t[N_META:],
        with_intra=True)

    o_hg = _hgrn(a_hg, qi, kl, dec, vh, og, kl_m, vh_m, row(hg_norm_g[0]), batch, seq)
    o_at = _attention(q, kn, kp, va, kn_m, kp_m, va_m, batch, seq)

    out_params = [bf(w_hg_o[0]), bf(w_mla_o[0]), bf(w_out[0]), row(mix_post_g[0]),
                  row(ffn_pre_g[0]), bf(w_ffn_in[0]), bf(w_ffn_out[0]), row(ffn_post_g[0])]
    assert OUT_ROWS // OUT_SPLIT == ATTN_Q
    y = _out_block(x2d, o_hg.reshape(batch * seq, hgw),
                   o_at.reshape(batch * (seq // ATTN_Q), -1, ATTN_Q), gates, out_params)
    return y.reshape(batch, seq, d)
```
